```python
import math
import jax, jax.numpy as jnp
from jax import lax
import numpy as np

D_MODEL = 1024
BATCH = 8
SEQ = 4096
DEPTH = 4

N_MEM = 256
N_MIXERS = 4
HEAD_DIM = 64
GROUP_WIDTH = D_MODEL // N_MIXERS
HEADS_PER_GROUP = GROUP_WIDTH // HEAD_DIM
ROPE_THETA = 500000.0
Q_BLOCK = 128
MAX_POS_OFFSET = 1024

MLA_Q_LORA = 192
MLA_KV_LORA = 128
MLA_NOPE = HEAD_DIM
MLA_ROPE = HEAD_DIM // 2
MLA_V = HEAD_DIM

MOBA_BLOCK = 256
MOBA_TOPK = 3
MOBA_Q_CHUNK = 32
MOBA_ROT = HEAD_DIM // 4

DIFF_HALF = HEAD_DIM // 2
DIFF_ROT = DIFF_HALF // 4

XATTN_HEADS = 4
XATTN_HEAD_DIM = D_MODEL // XATTN_HEADS

N_GROUPS = 8
EXPERTS_PER_GROUP = 4
N_EXPERTS = N_GROUPS * EXPERTS_PER_GROUP
TOP_K_IN_GROUP = 2
EXPERT_FF = D_MODEL // 4

DEEPNORM_ALPHA = (2 * DEPTH) ** 0.25
DEEPNORM_BETA = (8 * DEPTH) ** -0.25
LN_EPS = 1e-5
RMS_EPS = 1e-6

IN_SIZES = (MLA_Q_LORA, MLA_KV_LORA, MLA_ROPE, 3 * GROUP_WIDTH, 3 * GROUP_WIDTH, 3 * GROUP_WIDTH)
IN_COLS = sum(IN_SIZES)
IN_SPLIT_POINTS = [int(c) for c in np.cumsum(IN_SIZES)[:-1]]

kernel_name = 'hybrid_mla_sb_moba_diff_hmoe_deepnorm'


def layer_norm(x, g, b):
    xf = x.astype(jnp.float32)
    mu = jnp.mean(xf, axis=-1, keepdims=True)
    var = jnp.mean(jnp.square(xf - mu), axis=-1, keepdims=True)
    return ((xf - mu) * lax.rsqrt(var + LN_EPS) * g + b).astype(x.dtype)


def rms_norm(x, g):
    xf = x.astype(jnp.float32)
    return (xf * lax.rsqrt(jnp.mean(jnp.square(xf), axis=-1, keepdims=True) + RMS_EPS) * g).astype(x.dtype)


def to_heads(t, n_heads):
    b, s, _ = t.shape
    return t.reshape(b, s, n_heads, -1).transpose(0, 2, 1, 3)


def from_heads(o):
    b, h, s, d = o.shape
    return o.transpose(0, 2, 1, 3).reshape(b, s, h * d)


def rope_tables(positions, rot_dim):
    inv = ROPE_THETA ** (-jnp.arange(0, rot_dim, 2, dtype=jnp.float32) / rot_dim)
    ang = positions.astype(jnp.float32)[..., None] * inv
    return jnp.cos(ang), jnp.sin(ang)


def apply_rope(t, cos, sin):
    c = cos[:, None].astype(t.dtype)
    s = sin[:, None].astype(t.dtype)
    t1, t2 = jnp.split(t, 2, axis=-1)
    return jnp.concatenate([t1 * c - t2 * s, t1 * s + t2 * c], axis=-1)


def partial_rope(t, cos, sin):
    rot = 2 * cos.shape[-1]
    return jnp.concatenate([apply_rope(t[..., :rot], cos, sin), t[..., rot:]], axis=-1)


def query_blocks(t, qb):
    b, h, s, d = t.shape
    return t.reshape(b, h, s // qb, qb, d).transpose(2, 0, 1, 3, 4)


def merge_blocks(o):
    n, b, h, qb, d = o.shape
    return o.transpose(1, 2, 0, 3, 4).reshape(b, h, n * qb, d)


def causal_softmax_attention(q, k, v, scale):
    s = q.shape[2]
    key_pos = jnp.arange(s)

    def block(args):
        qb, i = args
        q_pos = i * Q_BLOCK + jnp.arange(Q_BLOCK)
        logits = jnp.einsum('bhqd,bhkd->bhqk', qb, k).astype(jnp.float32) * scale
        logits = jnp.where(key_pos[None, :] <= q_pos[:, None], logits, -jnp.inf)
        p = jax.nn.softmax(logits, axis=-1).astype(v.dtype)
        return jnp.einsum('bhqk,bhkd->bhqd', p, v)

    return merge_blocks(lax.map(block, (query_blocks(q, Q_BLOCK), jnp.arange(s // Q_BLOCK))))


def mla_attention(q_lat, kv_lat, k_rope, q_norm_g, w_uq, kv_norm_g, w_ukv, rope):
    cos, sin = rope
    q = to_heads(rms_norm(q_lat, q_norm_g) @ w_uq, HEADS_PER_GROUP)
    q = jnp.concatenate([q[..., :MLA_NOPE], apply_rope(q[..., MLA_NOPE:], cos, sin)], axis=-1)
    kv = to_heads(rms_norm(kv_lat, kv_norm_g) @ w_ukv, HEADS_PER_GROUP)
    k_nope, v = kv[..., :MLA_NOPE], kv[..., MLA_NOPE:]
    k_pe = apply_rope(k_rope[:, None], cos, sin)
    k = jnp.concatenate([k_nope, jnp.broadcast_to(k_pe, k_nope.shape[:3] + (MLA_ROPE,))], axis=-1)
    return causal_softmax_attention(q, k, v, (MLA_NOPE + MLA_ROPE) ** -0.5)


def stick_breaking_attention(q, k, v):
    s, d = q.shape[2], q.shape[3]
    key_pos = jnp.arange(s)

    def block(args):
        qb, i = args
        q_pos = i * Q_BLOCK + jnp.arange(Q_BLOCK)
        z = jnp.einsum('bhqd,bhkd->bhqk', qb, k).astype(jnp.float32) * (d ** -0.5)
        past = key_pos[None, :] < q_pos[:, None]
        log_one_minus = jnp.where(past, jax.nn.log_sigmoid(-z), 0.0)
        after = lax.cumsum(log_one_minus, axis=log_one_minus.ndim - 1, reverse=True) - log_one_minus
        a = jnp.where(past, jnp.exp(jax.nn.log_sigmoid(z) + after), 0.0).astype(v.dtype)
        return jnp.einsum('bhqk,bhkd->bhqd', a, v)

    return merge_blocks(lax.map(block, (query_blocks(q, Q_BLOCK), jnp.arange(s // Q_BLOCK))))


def moba_attention(q, k, v):
    b, h, s, d = q.shape
    nb = -(-s // MOBA_BLOCK)
    pad = nb * MOBA_BLOCK - s
    k_blk = jnp.pad(k, ((0, 0), (0, 0), (0, pad), (0, 0))).reshape(b, h, nb, MOBA_BLOCK, d)
    v_blk = jnp.pad(v, ((0, 0), (0, 0), (0, pad), (0, 0))).reshape(b, h, nb, MOBA_BLOCK, d)
    k_mean = jnp.mean(k_blk.astype(jnp.float32), axis=3).astype(k.dtype)
    n_sel = max(1, min(MOBA_TOPK, nb - 1))
    scale = d ** -0.5
    bi = jnp.arange(b)[:, None, None, None]
    hi = jnp.arange(h)[None, :, None, None]
    blk_ids = jnp.arange(nb)
    offs = jnp.arange(MOBA_BLOCK)

    def chunk(args):
        qc, i = args
        q_pos = i * MOBA_Q_CHUNK + jnp.arange(MOBA_Q_CHUNK)
        own = (i * MOBA_Q_CHUNK) // MOBA_BLOCK
        gate = jnp.einsum('bhqd,bhnd->bhqn', qc, k_mean).astype(jnp.float32)
        gate = jnp.where(blk_ids < own, gate, -jnp.inf)
        g_val, g_idx = lax.top_k(gate, n_sel)
        valid = g_val > -jnp.inf
        k_sel = k_blk[bi, hi, g_idx]
        v_sel = v_blk[bi, hi, g_idx]
        l_past = jnp.einsum('bhqd,bhqnkd->bhqnk', qc, k_sel).astype(jnp.float32) * scale
        l_past = jnp.where(valid[..., None], l_past, -jnp.inf).reshape(b, h, MOBA_Q_CHUNK, n_sel * MOBA_BLOCK)
        k_own = lax.dynamic_index_in_dim(k_blk, own, axis=2, keepdims=False)
        v_own = lax.dynamic_index_in_dim(v_blk, own, axis=2, keepdims=False)
        l_own = jnp.einsum('bhqd,bhkd->bhqk', qc, k_own).astype(jnp.float32) * scale
        l_own = jnp.where((own * MOBA_BLOCK + offs)[None, :] <= q_pos[:, None], l_own, -jnp.inf)
        p = jax.nn.softmax(jnp.concatenate([l_past, l_own], axis=-1), axis=-1).astype(v.dtype)
        p_past = p[..., :n_sel * MOBA_BLOCK].reshape(b, h, MOBA_Q_CHUNK, n_sel, MOBA_BLOCK)
        p_own = p[..., n_sel * MOBA_BLOCK:]
        return (jnp.einsum('bhqnk,bhqnkd->bhqd', p_past, v_sel)
                + jnp.einsum('bhqk,bhkd->bhqd', p_own, v_own))

    return merge_blocks(lax.map(chunk, (query_blocks(q, MOBA_Q_CHUNK), jnp.arange(s // MOBA_Q_CHUNK))))


def diff_attention(q, k, v, lam_params, subln_g, rope, lambda_init):
    cos, sin = rope
    s = q.shape[2]
    q1, q2 = partial_rope(q[..., :DIFF_HALF], cos, sin), partial_rope(q[..., DIFF_HALF:], cos, sin)
    k1, k2 = partial_rope(k[..., :DIFF_HALF], cos, sin), partial_rope(k[..., DIFF_HALF:], cos, sin)
    lp = lam_params.astype(jnp.float32)
    lam = jnp.exp(jnp.sum(lp[0] * lp[1])) - jnp.exp(jnp.sum(lp[2] * lp[3])) + lambda_init
    scale = DIFF_HALF ** -0.5
    key_pos = jnp.arange(s)

    def block(args):
        q1b, q2b, i = args
        q_pos = i * Q_BLOCK + jnp.arange(Q_BLOCK)
        mask = key_pos[None, :] <= q_pos[:, None]
        l1 = jnp.where(mask, jnp.einsum('bhqd,bhkd->bhqk', q1b, k1).astype(jnp.float32) * scale, -jnp.inf)
        l2 = jnp.where(mask, jnp.einsum('bhqd,bhkd->bhqk', q2b, k2).astype(jnp.float32) * scale, -jnp.inf)
        p = (jax.nn.softmax(l1, axis=-1) - lam * jax.nn.softmax(l2, axis=-1)).astype(v.dtype)
        return jnp.einsum('bhqk,bhkd->bhqd', p, v)

    o = merge_blocks(lax.map(block, (query_blocks(q1, Q_BLOCK), query_blocks(q2, Q_BLOCK), jnp.arange(s // Q_BLOCK))))
    return rms_norm(o, subln_g) * (1.0 - lambda_init)


def hybrid_mixer(x, w_in, mla_q_norm, w_uq, mla_kv_norm, w_ukv, diff_lambda, diff_subln, w_out,
                 rope_mla, rope_moba, rope_diff, lambda_init):
    h = x @ w_in
    q_lat, kv_lat, k_rope, sb_qkv, moba_qkv, diff_qkv = jnp.split(h, IN_SPLIT_POINTS, axis=-1)
    o_a = mla_attention(q_lat, kv_lat, k_rope, mla_q_norm, w_uq, mla_kv_norm, w_ukv, rope_mla)
    qb, kb, vb = [to_heads(t, HEADS_PER_GROUP) for t in jnp.split(sb_qkv, 3, axis=-1)]
    o_b = stick_breaking_attention(qb, kb, vb)
    qc, kc, vc = [to_heads(t, HEADS_PER_GROUP) for t in jnp.split(moba_qkv, 3, axis=-1)]
    o_c = moba_attention(partial_rope(qc, *rope_moba), partial_rope(kc, *rope_moba), vc)
    qd, kd, vd = [to_heads(t, HEADS_PER_GROUP) for t in jnp.split(diff_qkv, 3, axis=-1)]
    o_d = diff_attention(qd, kd, vd, diff_lambda, diff_subln, rope_diff, lambda_init)
    o = jnp.concatenate([from_heads(o_a), from_heads(o_b), from_heads(o_c), from_heads(o_d)], axis=-1)
    return o @ w_out


def memory_cross_attention(x, mem, wq, wk, wv, wo):
    q = to_heads(x @ wq, XATTN_HEADS)
    k = to_heads(mem @ wk, XATTN_HEADS)
    v = to_heads(mem @ wv, XATTN_HEADS)
    logits = jnp.einsum('bhqd,bhmd->bhqm', q, k).astype(jnp.float32) * (XATTN_HEAD_DIM ** -0.5)
    p = jax.nn.softmax(logits, axis=-1).astype(v.dtype)
    return from_heads(jnp.einsum('bhqm,bhmd->bhqd', p, v)) @ wo


def hierarchical_moe(x, w_group, b_group, w_expert, b_expert, w_gate, w_up, w_down):
    def one_sequence(xs):
        s = xs.shape[0]
        g_logits = (xs @ w_group + b_group).astype(jnp.float32)
        g_idx = jnp.argmax(g_logits, axis=-1)
        g_weight = jnp.take_along_axis(jax.nn.softmax(g_logits, axis=-1), g_idx[:, None], axis=-1)
        e_logits = (xs @ w_expert + b_expert).astype(jnp.float32).reshape(s, N_GROUPS, EXPERTS_PER_GROUP)
        e_logits = e_logits[jnp.arange(s), g_idx]
        top_val, top_idx = lax.top_k(e_logits, TOP_K_IN_GROUP)
        top_w = jax.nn.softmax(top_val, axis=-1) * g_weight
        expert_id = g_idx[:, None] * EXPERTS_PER_GROUP + top_idx
        gates = jnp.einsum('ske,sk->se', jax.nn.one_hot(expert_id, N_EXPERTS, dtype=jnp.float32), top_w).astype(xs.dtype)
        hid = jax.nn.silu(jnp.einsum('sd,edf->sef', xs, w_gate)) * jnp.einsum('sd,edf->sef', xs, w_up)
        return jnp.einsum('sef,efd->sd', hid * gates[:, :, None], w_down)

    return lax.map(one_sequence, x)


def setup_inputs(seed: int = 0) -> dict:
    key = jax.random.key(seed)
    ks = jax.random.split(key, 40)
    L, D = DEPTH, D_MODEL
    f32 = jnp.float32

    def normal(k, shape, std):
        return jax.random.normal(k, shape, f32) * std

    def gain(k, shape):
        return 1.0 + 0.02 * jax.random.normal(k, shape, f32)

    offset = jax.random.randint(ks[2], (BATCH, 1), 0, MAX_POS_OFFSET, dtype=jnp.int32)
    return {
        'x': normal(ks[0], (BATCH, SEQ, D), 1.0),
        'mem': normal(ks[1], (BATCH, N_MEM, D), 1.0),
        'positions': offset + jnp.arange(SEQ, dtype=jnp.int32)[None, :],
        'w_in': normal(ks[3], (L, D, IN_COLS), D ** -0.5),
        'mla_q_norm': gain(ks[4], (L, MLA_Q_LORA)),
        'w_uq': normal(ks[5], (L, MLA_Q_LORA, HEADS_PER_GROUP * (MLA_NOPE + MLA_ROPE)), MLA_Q_LORA ** -0.5),
        'mla_kv_norm': gain(ks[6], (L, MLA_KV_LORA)),
        'w_ukv': normal(ks[7], (L, MLA_KV_LORA, HEADS_PER_GROUP * (MLA_NOPE + MLA_V)), MLA_KV_LORA ** -0.5),
        'diff_lambda': normal(ks[8], (L, 4, DIFF_HALF), 0.1),
        'diff_subln': gain(ks[9], (L, HEAD_DIM)),
        'w_out': normal(ks[10], (L, D, D), DEEPNORM_BETA * D ** -0.5),
        'ln_mix_g': gain(ks[11], (L, D)),
        'ln_mix_b': normal(ks[12], (L, D), 0.02),
        'xattn_wq': normal(ks[13], (L, D, D), D ** -0.5),
        'xattn_wk': normal(ks[14], (L, D, D), D ** -0.5),
        'xattn_wv': normal(ks[15], (L, D, D), D ** -0.5),
        'xattn_wo': normal(ks[16], (L, D, D), DEEPNORM_BETA * D ** -0.5),
        'ln_mem_g': gain(ks[17], (L, D)),
        'ln_mem_b': normal(ks[18], (L, D), 0.02),
        'router_group_w': normal(ks[19], (L, D, N_GROUPS), D ** -0.5),
        'router_group_b': normal(ks[20], (L, N_GROUPS), 0.01),
        'router_expert_w': normal(ks[21], (L, D, N_EXPERTS), D ** -0.5),
        'router_expert_b': normal(ks[22], (L, N_EXPERTS), 0.01),
        'expert_w_gate': normal(ks[23], (L, N_EXPERTS, D, EXPERT_FF), D ** -0.5),
        'expert_w_up': normal(ks[24], (L, N_EXPERTS, D, EXPERT_FF), D ** -0.5),
        'expert_w_down': normal(ks[25], (L, N_EXPERTS, EXPERT_FF, D), DEEPNORM_BETA * EXPERT_FF ** -0.5),
        'ln_ffn_g': gain(ks[26], (L, D)),
        'ln_ffn_b': normal(ks[27], (L, D), 0.02),
    }


def reference(x, mem, positions, w_in, mla_q_norm, w_uq, mla_kv_norm, w_ukv, diff_lambda, diff_subln,
              w_out, ln_mix_g, ln_mix_b, xattn_wq, xattn_wk, xattn_wv, xattn_wo, ln_mem_g, ln_mem_b,
              router_group_w, router_group_b, router_expert_w, router_expert_b,
              expert_w_gate, expert_w_up, expert_w_down, ln_ffn_g, ln_ffn_b):
    rope_mla = rope_tables(positions, MLA_ROPE)
    rope_moba = rope_tables(positions, MOBA_ROT)
    rope_diff = rope_tables(positions, DIFF_ROT)
    for l in range(DEPTH):
        lambda_init = 0.8 - 0.6 * math.exp(-0.3 * l)
        mix = hybrid_mixer(x, w_in[l], mla_q_norm[l], w_uq[l], mla_kv_norm[l], w_ukv[l],
                           diff_lambda[l], diff_subln[l], w_out[l], rope_mla, rope_moba, rope_diff, lambda_init)
        x = layer_norm(DEEPNORM_ALPHA * x + mix, ln_mix_g[l], ln_mix_b[l])
        xa = memory_cross_attention(x, mem, xattn_wq[l], xattn_wk[l], xattn_wv[l], xattn_wo[l])
        x = layer_norm(DEEPNORM_ALPHA * x + xa, ln_mem_g[l], ln_mem_b[l])
        ff = hierarchical_moe(x, router_group_w[l], router_group_b[l], router_expert_w[l], router_expert_b[l],
                              expert_w_gate[l], expert_w_up[l], expert_w_down[l])
        x = layer_norm(DEEPNORM_ALPHA * x + ff, ln_ffn_g[l], ln_ffn_b[l])
    return x
```

```python
import functools
import math

import jax
import jax.numpy as jnp
from jax import lax
from jax.experimental import pallas as pl
from jax.experimental.pallas import tpu as pltpu

F32 = jnp.float32
BF16 = jnp.bfloat16
NEG_INF = float("-inf")

D_MODEL = 1024
DEPTH = 4
N_MEM = 256
HEAD_DIM = 64
GROUP_WIDTH = 256
HEADS = 4
ROPE_THETA = 500000.0
MLA_Q_LORA = 192
MLA_KV_LORA = 128
MLA_NOPE = 64
MLA_ROPE = 32
MOBA_BLOCK = 256
MOBA_TOPK = 3
MOBA_ROT = 16
DIFF_HALF = 32
DIFF_ROT = 8
XATTN_HEADS = 4
XATTN_HEAD_DIM = 256
N_GROUPS = 8
EXPERTS_PER_GROUP = 4
N_EXPERTS = 32
EXPERT_FF = 256
DEEPNORM_ALPHA = (2 * DEPTH) ** 0.25
LN_EPS = 1e-5
RMS_EPS = 1e-6

LANES = 128
VMEM_LIMIT_BYTES = 56 * 1024 * 1024
PROJ_TM = 512
ATT_TQ = 256
ATT_TK = 256
MOE_TM = 1024

C_SB, C_MB, C_DF = 0, 768, 1536
C_QLAT, C_KVLAT, C_KPE = 2304, 2560, 2688
PROJ_IN_COLS = 2816
O_SB, O_MB, O_DF, O_MLV, O_MLQ, O_MLK = 0, 768, 1536, 2304, 2560, 3072
PROJ_OUT_COLS = 3584
ROUTER_LANES = 128


def _dot(a, b):
    return jnp.dot(a, b, preferred_element_type=F32)


def _dot_nt(a, b):
    return lax.dot_general(a, b, (((1,), (1,)), ((), ())), preferred_element_type=F32)


def _split_bf16(x):
    hi = x.astype(BF16)
    lo = (x - hi.astype(F32)).astype(BF16)
    return hi, lo


def _layer_norm(x, g, b):
    mu = jnp.mean(x, axis=-1, keepdims=True)
    xc = x - mu
    var = jnp.mean(xc * xc, axis=-1, keepdims=True)
    return xc * lax.rsqrt(var + LN_EPS) * g + b


def _rope128(t, c, sa, sb, half):
    nxt = pltpu.roll(t, LANES - half, axis=1)
    prv = pltpu.roll(t, half, axis=1)
    return t * c + nxt * sa + prv * sb


def _proj_kernel(x_ref, w1_ref, wuq_ref, wukv_ref, gq_ref, gkv_ref,
                 mbc_ref, mbsa_ref, mbsb_ref, dfc_ref, dfsa_ref, dfsb_ref,
                 mlc_ref, mlsa_ref, mlsb_ref, out_ref, kmean_ref):
    xb = x_ref[...].astype(BF16)

    def mm(c0, c1):
        return _dot(xb, w1_ref[:, c0:c1])

    sb = mm(C_SB, C_SB + 768)
    out_ref[:, O_SB:O_SB + 256] = (sb[:, 0:256] * 0.125).astype(BF16)
    out_ref[:, O_SB + 256:O_SB + 768] = sb[:, 256:768].astype(BF16)

    mb = mm(C_MB, C_MB + 768)
    mbc, mbsa, mbsb = mbc_ref[...], mbsa_ref[...], mbsb_ref[...]
    for t in range(2):
        q = _rope128(mb[:, 128 * t:128 * t + 128], mbc, mbsa, mbsb, MOBA_ROT // 2)
        out_ref[:, O_MB + 128 * t:O_MB + 128 * t + 128] = (q * 0.125).astype(BF16)
        k = _rope128(mb[:, 256 + 128 * t:256 + 128 * t + 128], mbc, mbsa, mbsb, MOBA_ROT // 2)
        out_ref[:, O_MB + 256 + 128 * t:O_MB + 256 + 128 * t + 128] = k.astype(BF16)
        for r in range(PROJ_TM // MOBA_BLOCK):
            kmean_ref[0, r:r + 1, 128 * t:128 * t + 128] = jnp.mean(
                k[r * MOBA_BLOCK:(r + 1) * MOBA_BLOCK], axis=0, keepdims=True)
    out_ref[:, O_MB + 512:O_MB + 768] = mb[:, 512:768].astype(BF16)

    df = mm(C_DF, C_DF + 768)
    dfc, dfsa, dfsb = dfc_ref[...], dfsa_ref[...], dfsb_ref[...]
    dscale = DIFF_HALF ** -0.5
    for t in range(2):
        q = _rope128(df[:, 128 * t:128 * t + 128], dfc, dfsa, dfsb, DIFF_ROT // 2)
        out_ref[:, O_DF + 128 * t:O_DF + 128 * t + 128] = (q * dscale).astype(BF16)
        k = _rope128(df[:, 256 + 128 * t:256 + 128 * t + 128], dfc, dfsa, dfsb, DIFF_ROT // 2)
        out_ref[:, O_DF + 256 + 128 * t:O_DF + 256 + 128 * t + 128] = k.astype(BF16)
    out_ref[:, O_DF + 512:O_DF + 768] = df[:, 512:768].astype(BF16)

    lat = mm(C_QLAT, PROJ_IN_COLS)
    mlc, mlsa, mlsb = mlc_ref[...], mlsa_ref[...], mlsb_ref[...]
    ql = lat[:, 0:256]
    qn = ql * lax.rsqrt(jnp.sum(ql * ql, axis=-1, keepdims=True) * (1.0 / MLA_Q_LORA) + RMS_EPS) * gq_ref[...]
    qm = _dot(qn.astype(BF16), wuq_ref[...])
    kvl = lat[:, 256:384]
    kvn = kvl * lax.rsqrt(jnp.mean(kvl * kvl, axis=-1, keepdims=True) + RMS_EPS) * gkv_ref[...]
    kv = _dot(kvn.astype(BF16), wukv_ref[...])
    kpe = _rope128(lat[:, 384:512], mlc, mlsa, mlsb, MLA_ROPE // 2)
    mscale = (MLA_NOPE + MLA_ROPE) ** -0.5
    for h in range(HEADS):
        q = _rope128(qm[:, 128 * h:128 * h + 128], mlc, mlsa, mlsb, MLA_ROPE // 2)
        out_ref[:, O_MLQ + 128 * h:O_MLQ + 128 * h + 128] = (q * mscale).astype(BF16)
        out_ref[:, O_MLK + 128 * h:O_MLK + 128 * h + 128] = (kv[:, 128 * h:128 * h + 128] + kpe).astype(BF16)
    out_ref[:, O_MLV:O_MLV + 256] = kv[:, 512:768].astype(BF16)


def _proj_call(x, w1, wuq, wukv, gq, gkv, tables):
    t_tokens = x.shape[0]
    n_steps = t_tokens // PROJ_TM
    full = lambda shape: pl.BlockSpec(shape, lambda i: (0,) * len(shape))
    tab_spec = pl.BlockSpec((PROJ_TM, LANES), lambda i: (i, 0))
    return pl.pallas_call(
        _proj_kernel,
        grid=(n_steps,),
        in_specs=[pl.BlockSpec((PROJ_TM, D_MODEL), lambda i: (i, 0)),
                  full((D_MODEL, PROJ_IN_COLS)), full((256, 512)), full((128, 768)),
                  full((1, 256)), full((1, 128))] + [tab_spec] * 9,
        out_specs=[pl.BlockSpec((PROJ_TM, PROJ_OUT_COLS), lambda i: (i, 0)),
                   pl.BlockSpec((1, PROJ_TM // MOBA_BLOCK, 256), lambda i: (i, 0, 0))],
        out_shape=[jax.ShapeDtypeStruct((t_tokens, PROJ_OUT_COLS), BF16),
                   jax.ShapeDtypeStruct((n_steps, PROJ_TM // MOBA_BLOCK, 256), F32)],
        compiler_params=pltpu.CompilerParams(dimension_semantics=("parallel",),
                                             vmem_limit_bytes=VMEM_LIMIT_BYTES),
        name="proj",
    )(x, w1, wuq, wukv, gq, gkv, *tables)


def _lane_iota(shape):
    return lax.broadcasted_iota(jnp.int32, shape, 1)


def _causal_mask(tq, tk):
    return lax.broadcasted_iota(jnp.int32, (tq, tk), 1) <= lax.broadcasted_iota(jnp.int32, (tq, tk), 0)


def _flash_update(s, v_tile, carry):
    m, l, acc = carry
    m_new = jnp.maximum(m, jnp.max(s, axis=-1, keepdims=True))
    alpha = jnp.exp(m - m_new)
    p = jnp.exp(s - m_new)
    l = alpha * l + jnp.sum(p, axis=-1, keepdims=True)
    acc = alpha * acc + _dot(p.astype(BF16), v_tile)
    return m_new, l, acc


def _flash_init(tq):
    return (jnp.full((tq, 1), NEG_INF, F32), jnp.zeros((tq, 1), F32), jnp.zeros((tq, LANES), F32))


def _kv_rows(j):
    return pl.ds(pl.multiple_of(j * ATT_TK, ATT_TK), ATT_TK)


def _merge_pair(o_even, o_odd):
    return jnp.where(_lane_iota(o_even.shape) < HEAD_DIM, o_even, o_odd)


def _head_lane_mask(h, width=HEAD_DIM):
    lane = _lane_iota((1, LANES))
    lo = (h % 2) * HEAD_DIM
    return (lane >= lo) & (lane < lo + width)


def _mla_kernel(q_ref, k_ref, v_ref, o_ref):
    qi = pl.program_id(1)
    tq = ATT_TQ
    diag = _causal_mask(tq, ATT_TK)
    outs = []
    for h in range(HEADS):
        qh = q_ref[0, :, 128 * h:128 * h + 128]
        kc, vc = slice(128 * h, 128 * h + 128), slice(128 * (h // 2), 128 * (h // 2) + 128)
        s = jnp.where(diag, _dot_nt(qh, k_ref[0, _kv_rows(qi), kc]), NEG_INF)
        carry = _flash_update(s, v_ref[0, _kv_rows(qi), vc], _flash_init(tq))

        def body(j, carry, qh=qh, kc=kc, vc=vc):
            s = _dot_nt(qh, k_ref[0, _kv_rows(j), kc])
            return _flash_update(s, v_ref[0, _kv_rows(j), vc], carry)

        _, l, acc = lax.fori_loop(0, qi, body, carry)
        outs.append(acc / l)
    for t in range(2):
        o_ref[0, :, 128 * t:128 * t + 128] = _merge_pair(outs[2 * t], outs[2 * t + 1]).astype(o_ref.dtype)


def _sb_block(z, past, cum, u_tri, v_tile):
    sp = jnp.maximum(z, 0.0) + jnp.log1p(jnp.exp(-jnp.abs(z)))
    lom = -sp if past is None else jnp.where(past, -sp, 0.0)
    hi, lo = _split_bf16(lom)
    after = _dot(hi, u_tri) + _dot(lo, u_tri) + cum
    a = jnp.exp(z - sp + after)
    if past is not None:
        a = jnp.where(past, a, 0.0)
    return _dot(a.astype(BF16), v_tile), jnp.sum(lom, axis=-1, keepdims=True)


def _sb_kernel(q_ref, k_ref, v_ref, o_ref):
    qi = pl.program_id(1)
    tq, tk = ATT_TQ, ATT_TK
    row = lax.broadcasted_iota(jnp.int32, (tq, tk), 0)
    col = lax.broadcasted_iota(jnp.int32, (tq, tk), 1)
    past = col < row
    u_tri = jnp.where(row > col, 1.0, 0.0).astype(BF16)
    outs = []
    for h in range(HEADS):
        tile = slice(128 * (h // 2), 128 * (h // 2) + 128)
        qh = jnp.where(_head_lane_mask(h), q_ref[0, :, tile], jnp.zeros((), BF16))
        z = _dot_nt(qh, k_ref[0, _kv_rows(qi), tile])
        acc, cum = _sb_block(z, past, jnp.zeros((tq, 1), F32), u_tri, v_ref[0, _kv_rows(qi), tile])

        def body(i, carry, qh=qh, tile=tile):
            acc, cum = carry
            j = qi - 1 - i
            z = _dot_nt(qh, k_ref[0, _kv_rows(j), tile])
            d_acc, d_cum = _sb_block(z, None, cum, u_tri, v_ref[0, _kv_rows(j), tile])
            return acc + d_acc, cum + d_cum

        acc, _ = lax.fori_loop(0, qi, body, (acc, cum))
        outs.append(acc)
    for t in range(2):
        o_ref[0, :, 128 * t:128 * t + 128] = _merge_pair(outs[2 * t], outs[2 * t + 1]).astype(o_ref.dtype)


def _moba_kernel(q_ref, k_ref, v_ref, kmean_ref, o_ref):
    qi = pl.program_id(1)
    tq = ATT_TQ
    n_blk = kmean_ref.shape[1]
    diag = _causal_mask(tq, ATT_TK)
    blk = _lane_iota((tq, n_blk))
    outs = []
    for h in range(HEADS):
        tile = slice(128 * (h // 2), 128 * (h // 2) + 128)
        qh = jnp.where(_head_lane_mask(h), q_ref[0, :, tile], jnp.zeros((), BF16))
        km_hi, km_lo = _split_bf16(kmean_ref[0, :, tile])
        gate = _dot_nt(qh, km_hi) + _dot_nt(qh, km_lo)
        gate = jnp.where(blk < qi, gate, NEG_INF)
        bias = jnp.full((tq, n_blk), NEG_INF, F32)
        for _ in range(MOBA_TOPK):
            mx = jnp.max(gate, axis=-1, keepdims=True)
            first = jnp.min(jnp.where(gate == mx, blk, n_blk), axis=-1, keepdims=True)
            pick = (blk == first) & (mx > NEG_INF)
            bias = jnp.where(pick, 0.0, bias)
            gate = jnp.where(pick, NEG_INF, gate)

        s = jnp.where(diag, _dot_nt(qh, k_ref[0, _kv_rows(qi), tile]), NEG_INF)
        carry = _flash_update(s, v_ref[0, _kv_rows(qi), tile], _flash_init(tq))

        def body(j, carry, qh=qh, tile=tile, bias=bias):
            col_bias = jnp.max(jnp.where(blk == j, bias, NEG_INF), axis=-1, keepdims=True)
            s = _dot_nt(qh, k_ref[0, _kv_rows(j), tile]) + col_bias
            return _flash_update(s, v_ref[0, _kv_rows(j), tile], carry)

        _, l, acc = lax.fori_loop(0, qi, body, carry)
        outs.append(acc / l)
    for t in range(2):
        o_ref[0, :, 128 * t:128 * t + 128] = _merge_pair(outs[2 * t], outs[2 * t + 1]).astype(o_ref.dtype)


def _diff_kernel(lambda_init, q_ref, k_ref, v_ref, lam_ref, g_ref, o_ref):
    qi = pl.program_id(1)
    tq = ATT_TQ
    diag = _causal_mask(tq, ATT_TK)
    lp = lam_ref[...]
    lam = (jnp.exp(jnp.sum(lp[0:1] * lp[1:2], axis=-1, keepdims=True))
           - jnp.exp(jnp.sum(lp[2:3] * lp[3:4], axis=-1, keepdims=True)) + lambda_init)
    lane = _lane_iota((1, LANES))
    outs = []
    for h in range(HEADS):
        tile = slice(128 * (h // 2), 128 * (h // 2) + 128)
        lo = (h % 2) * HEAD_DIM
        qt = q_ref[0, :, tile]
        q1 = jnp.where((lane >= lo) & (lane < lo + DIFF_HALF), qt, jnp.zeros((), BF16))
        q2 = jnp.where((lane >= lo + DIFF_HALF) & (lane < lo + HEAD_DIM), qt, jnp.zeros((), BF16))

        kd, vd = k_ref[0, _kv_rows(qi), tile], v_ref[0, _kv_rows(qi), tile]
        c1 = _flash_update(jnp.where(diag, _dot_nt(q1, kd), NEG_INF), vd, _flash_init(tq))
        c2 = _flash_update(jnp.where(diag, _dot_nt(q2, kd), NEG_INF), vd, _flash_init(tq))

        def body(j, carry, q1=q1, q2=q2, tile=tile):
            c1, c2 = carry
            kj, vj = k_ref[0, _kv_rows(j), tile], v_ref[0, _kv_rows(j), tile]
            return _flash_update(_dot_nt(q1, kj), vj, c1), _flash_update(_dot_nt(q2, kj), vj, c2)

        (_, l1, a1), (_, l2, a2) = lax.fori_loop(0, qi, body, (c1, c2))
        o = a1 / l1 - lam * (a2 / l2)
        head = (lane >= lo) & (lane < lo + HEAD_DIM)
        ms = jnp.sum(jnp.where(head, o * o, 0.0), axis=-1, keepdims=True) * (1.0 / HEAD_DIM)
        outs.append(o * lax.rsqrt(ms + RMS_EPS) * g_ref[...] * (1.0 - lambda_init))
    for t in range(2):
        o_ref[0, :, 128 * t:128 * t + 128] = _merge_pair(outs[2 * t], outs[2 * t + 1]).astype(o_ref.dtype)


def _attn_call(kernel, name, proj3, q_blk, k_blk, v_blk, q_width, extra_in=(), extra_specs=()):
    bsz, seq, _ = proj3.shape
    kv_width = q_width
    return pl.pallas_call(
        kernel,
        grid=(bsz, seq // ATT_TQ),
        in_specs=[pl.BlockSpec((1, ATT_TQ, q_width), lambda b, i: (b, i, q_blk)),
                  pl.BlockSpec((1, seq, kv_width), lambda b, i: (b, 0, k_blk)),
                  pl.BlockSpec((1, seq, GROUP_WIDTH), lambda b, i: (b, 0, v_blk))] + list(extra_specs),
        out_specs=pl.BlockSpec((1, ATT_TQ, GROUP_WIDTH), lambda b, i: (b, i, 0)),
        out_shape=jax.ShapeDtypeStruct((bsz, seq, GROUP_WIDTH), BF16),
        compiler_params=pltpu.CompilerParams(dimension_semantics=("parallel", "parallel"),
                                             vmem_limit_bytes=VMEM_LIMIT_BYTES),
        name=name,
    )(proj3, proj3, proj3, *extra_in)


def _memkv_kernel(mem_ref, wk_ref, wv_ref, k_ref, v_ref):
    mb = mem_ref[...].astype(BF16)
    k_ref[...] = _dot(mb, wk_ref[...]).astype(BF16)
    v_ref[...] = _dot(mb, wv_ref[...]).astype(BF16)


def _memkv_call(mem2, wk, wv):
    rows = mem2.shape[0]
    full = pl.BlockSpec((D_MODEL, D_MODEL), lambda i: (0, 0))
    blk = pl.BlockSpec((N_MEM, D_MODEL), lambda i: (i, 0))
    return pl.pallas_call(
        _memkv_kernel,
        grid=(rows // N_MEM,),
        in_specs=[blk, full, full],
        out_specs=[blk, blk],
        out_shape=[jax.ShapeDtypeStruct((rows, D_MODEL), BF16)] * 2,
        compiler_params=pltpu.CompilerParams(dimension_semantics=("parallel",),
                                             vmem_limit_bytes=VMEM_LIMIT_BYTES),
        name="memkv",
    )(mem2, wk, wv)


def _post_kernel(x_ref, oa_ref, ob_ref, oc_ref, od_ref, wout_ref, g1_ref, b1_ref,
                 wq_ref, km_ref, vm_ref, wo_ref, g2_ref, b2_ref, wrh_ref, wrl_ref, br_ref,
                 x2_ref, x2b_ref, gates_ref, oh_ref):
    mix = (_dot(oa_ref[...], wout_ref[0:256, :]) + _dot(ob_ref[...], wout_ref[256:512, :])
           + _dot(oc_ref[...], wout_ref[512:768, :]) + _dot(od_ref[...], wout_ref[768:1024, :]))
    x1 = _layer_norm(DEEPNORM_ALPHA * x_ref[...] + mix, g1_ref[...], b1_ref[...])

    q = (_dot(x1.astype(BF16), wq_ref[...]) * (XATTN_HEAD_DIM ** -0.5)).astype(BF16)
    for h in range(XATTN_HEADS):
        c = slice(XATTN_HEAD_DIM * h, XATTN_HEAD_DIM * (h + 1))
        s = _dot_nt(q[:, c], km_ref[0, :, c])
        p = jnp.exp(s - jnp.max(s, axis=-1, keepdims=True))
        o = _dot(p.astype(BF16), vm_ref[0, :, c]) / jnp.sum(p, axis=-1, keepdims=True)
        oh_ref[:, c] = o.astype(BF16)
    xa = _dot(oh_ref[...], wo_ref[...])
    x2 = _layer_norm(DEEPNORM_ALPHA * x1 + xa, g2_ref[...], b2_ref[...])
    x2_ref[...] = x2
    x2b_ref[...] = x2.astype(BF16)

    hi, lo = _split_bf16(x2)
    logits = _dot(hi, wrh_ref[...]) + _dot(hi, wrl_ref[...]) + _dot(lo, wrh_ref[...]) + br_ref[...]
    lane = _lane_iota(logits.shape)
    gl = jnp.where(lane < N_GROUPS, logits, NEG_INF)
    gmx = jnp.max(gl, axis=-1, keepdims=True)
    gidx = jnp.min(jnp.where(gl == gmx, lane, ROUTER_LANES), axis=-1, keepdims=True)
    gw = 1.0 / jnp.sum(jnp.exp(gl - gmx), axis=-1, keepdims=True)
    e0 = N_GROUPS + EXPERTS_PER_GROUP * gidx
    el = jnp.where((lane >= e0) & (lane < e0 + EXPERTS_PER_GROUP), logits, NEG_INF)
    m1 = jnp.max(el, axis=-1, keepdims=True)
    i1 = jnp.min(jnp.where(el == m1, lane, ROUTER_LANES), axis=-1, keepdims=True)
    el = jnp.where(lane == i1, NEG_INF, el)
    m2 = jnp.max(el, axis=-1, keepdims=True)
    i2 = jnp.min(jnp.where(el == m2, lane, ROUTER_LANES), axis=-1, keepdims=True)
    e = jnp.exp(m2 - m1)
    w1 = gw / (1.0 + e)
    w2 = gw * e / (1.0 + e)
    gates_ref[...] = jnp.where(lane == i1, w1, 0.0) + jnp.where(lane == i2, w2, 0.0)


def _post_call(x, oa, ob, oc, od, wout, g1, b1, wq, kmem, vmem, wo, g2, b2, wrh, wrl, br, seq):
    t_tokens = x.shape[0]
    tm = PROJ_TM
    per_seq = seq // tm
    row = lambda w: pl.BlockSpec((tm, w), lambda i: (i, 0))
    full = lambda shape: pl.BlockSpec(shape, lambda i: (0,) * len(shape))
    mem_spec = pl.BlockSpec((1, N_MEM, D_MODEL), lambda i: (i // per_seq, 0, 0))
    sq = (D_MODEL, D_MODEL)
    return pl.pallas_call(
        _post_kernel,
        grid=(t_tokens // tm,),
        in_specs=[row(D_MODEL), row(256), row(256), row(256), row(256),
                  full(sq), full((1, D_MODEL)), full((1, D_MODEL)),
                  full(sq), mem_spec, mem_spec, full(sq), full((1, D_MODEL)), full((1, D_MODEL)),
                  full((D_MODEL, ROUTER_LANES)), full((D_MODEL, ROUTER_LANES)), full((1, ROUTER_LANES))],
        out_specs=[row(D_MODEL), row(D_MODEL), row(ROUTER_LANES)],
        out_shape=[jax.ShapeDtypeStruct((t_tokens, D_MODEL), F32),
                   jax.ShapeDtypeStruct((t_tokens, D_MODEL), BF16),
                   jax.ShapeDtypeStruct((t_tokens, ROUTER_LANES), F32)],
        scratch_shapes=[pltpu.VMEM((tm, D_MODEL), BF16)],
        compiler_params=pltpu.CompilerParams(dimension_semantics=("parallel",),
                                             vmem_limit_bytes=VMEM_LIMIT_BYTES),
        name="post",
    )(x, oa, ob, oc, od, wout, g1, b1, wq, kmem, vmem, wo, g2, b2, wrh, wrl, br)


def _moe_kernel(xb_ref, x2_ref, gates_ref, wg_ref, wu_ref, wd_ref, g_ref, b_ref, o_ref, acc_ref):
    grp = pl.program_id(1)

    @pl.when(grp == 0)
    def _():
        acc_ref[...] = jnp.zeros_like(acc_ref)

    xb = xb_ref[...]
    gates = gates_ref[...]
    lane = _lane_iota(gates.shape)
    for j in range(EXPERTS_PER_GROUP):
        e = N_GROUPS + EXPERTS_PER_GROUP * grp + j
        gate_col = jnp.sum(jnp.where(lane == e, gates, 0.0), axis=-1, keepdims=True)
        gg = _dot(xb, wg_ref[j])
        uu = _dot(xb, wu_ref[j])
        hid = gg * (1.0 / (1.0 + jnp.exp(-gg))) * uu * gate_col
        acc_ref[...] += _dot(hid.astype(BF16), wd_ref[j])

    @pl.when(grp == N_GROUPS - 1)
    def _():
        o_ref[...] = _layer_norm(DEEPNORM_ALPHA * x2_ref[...] + acc_ref[...], g_ref[...], b_ref[...])


def _moe_call(x2b, x2, gates, wg, wu, wd, g, b):
    t_tokens = x2.shape[0]
    tm = MOE_TM
    row = lambda w: pl.BlockSpec((tm, w), lambda i, e: (i, 0))
    vec = pl.BlockSpec((1, D_MODEL), lambda i, e: (0, 0))
    return pl.pallas_call(
        _moe_kernel,
        grid=(t_tokens // tm, N_GROUPS),
        in_specs=[row(D_MODEL), row(D_MODEL), row(ROUTER_LANES),
                  pl.BlockSpec((EXPERTS_PER_GROUP, D_MODEL, EXPERT_FF), lambda i, e: (e, 0, 0)),
                  pl.BlockSpec((EXPERTS_PER_GROUP, D_MODEL, EXPERT_FF), lambda i, e: (e, 0, 0)),
                  pl.BlockSpec((EXPERTS_PER_GROUP, EXPERT_FF, D_MODEL), lambda i, e: (e, 0, 0)),
                  vec, vec],
        out_specs=row(D_MODEL),
        out_shape=jax.ShapeDtypeStruct((t_tokens, D_MODEL), F32),
        scratch_shapes=[pltpu.VMEM((tm, D_MODEL), F32)],
        compiler_params=pltpu.CompilerParams(dimension_semantics=("parallel", "arbitrary"),
                                             vmem_limit_bytes=VMEM_LIMIT_BYTES),
        name="moe",
    )(x2b, x2, gates, wg, wu, wd, g, b)


def _rope_lane_tables(positions, rot, period, offset):
    half = rot // 2
    inv = ROPE_THETA ** (-jnp.arange(0, rot, 2, dtype=F32) / rot)
    ang = positions.astype(F32).reshape(-1)[:, None] * inv
    cos, sin = jnp.cos(ang), jnp.sin(ang)
    p = jnp.arange(LANES) % period - offset
    first = (p >= 0) & (p < half)
    second = (p >= half) & (p < rot)
    idx = jnp.clip(jnp.where(second, p - half, p), 0, half - 1)
    cg, sg = cos[:, idx], sin[:, idx]
    c = jnp.where(first | second, cg, 1.0)
    sa = jnp.where(first, -sg, 0.0)
    sb = jnp.where(second, sg, 0.0)
    return c, sa, sb


def _prep_proj_weights(w_in, w_uq, w_ukv, q_norm, kv_norm):
    q_lat, kv_lat, k_rope = w_in[:, 0:192], w_in[:, 192:320], w_in[:, 320:352]
    sb, mb, df = w_in[:, 352:1120], w_in[:, 1120:1888], w_in[:, 1888:2656]
    zeros = lambda n: jnp.zeros((D_MODEL, n), w_in.dtype)
    w1 = jnp.concatenate([sb, mb, df, q_lat, zeros(64), kv_lat, zeros(64), k_rope, zeros(32)], axis=1)
    wuq = w_uq.reshape(MLA_Q_LORA, HEADS, MLA_NOPE + MLA_ROPE)
    wuq = jnp.pad(wuq, ((0, 256 - MLA_Q_LORA), (0, 0), (0, LANES - MLA_NOPE - MLA_ROPE))).reshape(256, HEADS * LANES)
    wukv = w_ukv.reshape(MLA_KV_LORA, HEADS, 2, HEAD_DIM)
    wk = jnp.pad(wukv[:, :, 0], ((0, 0), (0, 0), (0, LANES - HEAD_DIM))).reshape(MLA_KV_LORA, HEADS * LANES)
    wv = wukv[:, :, 1].reshape(MLA_KV_LORA, HEADS * HEAD_DIM)
    gq = jnp.pad(q_norm, (0, 256 - MLA_Q_LORA)).reshape(1, 256)
    return (w1.astype(BF16), wuq.astype(BF16), jnp.concatenate([wk, wv], axis=1).astype(BF16),
            gq, kv_norm.reshape(1, MLA_KV_LORA))


def kernel(x, mem, positions, w_in, mla_q_norm, w_uq, mla_kv_norm, w_ukv, diff_lambda, diff_subln,
           w_out, ln_mix_g, ln_mix_b, xattn_wq, xattn_wk, xattn_wv, xattn_wo, ln_mem_g, ln_mem_b,
           router_group_w, router_group_b, router_expert_w, router_expert_b,
           expert_w_gate, expert_w_up, expert_w_down, ln_ffn_g, ln_ffn_b):
    bsz, seq, _ = x.shape
    t_tokens = bsz * seq
    n_blk = seq // MOBA_BLOCK
    assert seq % PROJ_TM == 0 and t_tokens % MOE_TM == 0 and mem.shape[1] == N_MEM

    tables = (_rope_lane_tables(positions, MOBA_ROT, HEAD_DIM, 0)
              + _rope_lane_tables(positions, DIFF_ROT, DIFF_HALF, 0)
              + _rope_lane_tables(positions, MLA_ROPE, LANES, MLA_NOPE))
    mem2 = mem.reshape(bsz * N_MEM, D_MODEL)
    row = lambda v: v.reshape(1, -1)
    xf = x.reshape(t_tokens, D_MODEL)

    for l in range(DEPTH):
        lambda_init = 0.8 - 0.6 * math.exp(-0.3 * l)
        w1, wuq, wukv, gq, gkv = _prep_proj_weights(w_in[l], w_uq[l], w_ukv[l], mla_q_norm[l], mla_kv_norm[l])
        proj, kmean = _proj_call(xf, w1, wuq, wukv, gq, gkv, tables)
        proj3 = proj.reshape(bsz, seq, PROJ_OUT_COLS)
        kmean3 = kmean.reshape(bsz, n_blk, GROUP_WIDTH)

        o_mla = _attn_call(_mla_kernel, "mla_attn", proj3, O_MLQ // 512, O_MLK // 512, O_MLV // 256, 512)
        o_sb = _attn_call(_sb_kernel, "sb_attn", proj3, O_SB // 256, O_SB // 256 + 1, O_SB // 256 + 2, 256)
        o_mb = _attn_call(_moba_kernel, "moba_attn", proj3, O_MB // 256, O_MB // 256 + 1, O_MB // 256 + 2, 256,
                          extra_in=(kmean3,),
                          extra_specs=(pl.BlockSpec((1, n_blk, GROUP_WIDTH), lambda b, i: (b, 0, 0)),))
        subln = jnp.tile(diff_subln[l], LANES // HEAD_DIM).reshape(1, LANES)
        o_df = _attn_call(functools.partial(_diff_kernel, lambda_init), "diff_attn", proj3,
                          O_DF // 256, O_DF // 256 + 1, O_DF // 256 + 2, 256,
                          extra_in=(diff_lambda[l], subln),
                          extra_specs=(pl.BlockSpec((4, DIFF_HALF), lambda b, i: (0, 0)),
                                       pl.BlockSpec((1, LANES), lambda b, i: (0, 0))))

        kmem, vmem = _memkv_call(mem2, xattn_wk[l].astype(BF16), xattn_wv[l].astype(BF16))
        wr = jnp.concatenate([router_group_w[l], router_expert_w[l],
                              jnp.zeros((D_MODEL, ROUTER_LANES - N_GROUPS - N_EXPERTS), F32)], axis=1)
        wr_hi = wr.astype(BF16)
        wr_lo = (wr - wr_hi.astype(F32)).astype(BF16)
        br = jnp.concatenate([router_group_b[l], router_expert_b[l],
                              jnp.zeros((ROUTER_LANES - N_GROUPS - N_EXPERTS,), F32)]).reshape(1, ROUTER_LANES)
        flat = lambda o: o.reshape(t_tokens, GROUP_WIDTH)
        x2, x2b, gates = _post_call(
            xf, flat(o_mla), flat(o_sb), flat(o_mb), flat(o_df), w_out[l].astype(BF16),
            row(ln_mix_g[l]), row(ln_mix_b[l]), xattn_wq[l].astype(BF16),
            kmem.reshape(bsz, N_MEM, D_MODEL), vmem.reshape(bsz, N_MEM, D_MODEL), xattn_wo[l].astype(BF16),
            row(ln_mem_g[l]), row(ln_mem_b[l]), wr_hi, wr_lo, br, seq)

        xf = _moe_call(x2b, x2, gates, expert_w_gate[l].astype(BF16), expert_w_up[l].astype(BF16),
                       expert_w_down[l].astype(BF16), row(ln_ffn_g[l]), row(ln_ffn_b[l]))
    return xf.reshape(bsz, seq, D_MODEL)
```

```python
import functools
import math

import jax
import jax.numpy as jnp
from jax import lax
from jax.experimental import pallas as pl
from jax.experimental.pallas import tpu as pltpu

F32 = jnp.float32
BF16 = jnp.bfloat16
NEG_INF = float("-inf")

D_MODEL = 1024
DEPTH = 4
N_MEM = 256
HEAD_DIM = 64
GROUP_WIDTH = 256
HEADS = 4
ROPE_THETA = 500000.0
MLA_Q_LORA = 192
MLA_KV_LORA = 128
MLA_NOPE = 64
MLA_ROPE = 32
MOBA_BLOCK = 256
MOBA_TOPK = 3
MOBA_ROT = 16
DIFF_HALF = 32
DIFF_ROT = 8
XATTN_HEADS = 4
XATTN_HEAD_DIM = 256
N_GROUPS = 8
EXPERTS_PER_GROUP = 4
N_EXPERTS = 32
EXPERT_FF = 256
DEEPNORM_ALPHA = (2 * DEPTH) ** 0.25
LN_EPS = 1e-5
RMS_EPS = 1e-6

LANES = 128
VMEM_LIMIT_BYTES = 56 * 1024 * 1024
PROJ_TM = 512
ATT_TQ = 256
ATT_TK = 256
MOE_TM = 1024

C_MBQ, C_MBK, C_MBV, C_DFV, C_LAT, C_SB, C_DFQ, C_DFK = 0, 512, 1024, 1536, 2048, 2560, 3328, 3584
PROJ_IN_COLS = 3840
O_MBQ, O_MBK, O_MBV, O_DFV, O_MLQ, O_MLK, O_MLV = 0, 512, 1024, 1536, 2048, 2560, 3072
O_SBQ, O_SBK, O_SBV, O_DFQ, O_DFK = 3584, 3840, 4096, 4352, 4608
PROJ_OUT_COLS = 4864
ROUTER_LANES = 128
LOG2E = 1.4426950408889634
MOBA_MASK_BIAS = 2.0 ** 100


def _dot(a, b):
    return jnp.dot(a, b, preferred_element_type=F32)


def _dot_nt(a, b):
    return lax.dot_general(a, b, (((1,), (1,)), ((), ())), preferred_element_type=F32)


def _split_bf16(x):
    hi = x.astype(BF16)
    lo = (x - hi.astype(F32)).astype(BF16)
    return hi, lo


def _layer_norm(x, g, b):
    mu = jnp.mean(x, axis=-1, keepdims=True)
    xc = x - mu
    var = jnp.mean(xc * xc, axis=-1, keepdims=True)
    return xc * lax.rsqrt(var + LN_EPS) * g + b


def _lane_iota(shape):
    return lax.broadcasted_iota(jnp.int32, shape, 1)


def _rope128(t, c, sa, sb, half):
    nxt = pltpu.roll(t, LANES - half, axis=1)
    prv = pltpu.roll(t, half, axis=1)
    return t * c + nxt * sa + prv * sb


def _proj_kernel(n_blk, x_ref, w1_ref, wuq_ref, wukv_ref, gq_ref, gkv_ref,
                 mbc_ref, mbsa_ref, mbsb_ref, dfc_ref, dfsa_ref, dfsb_ref,
                 mlc_ref, mlsa_ref, mlsb_ref, out_ref, kmean_ref):
    xb = x_ref[...].astype(BF16)
    tm = xb.shape[0]
    lane = _lane_iota((tm, LANES))
    upper = lane >= HEAD_DIM

    def mm(c0, width):
        return _dot(xb, w1_ref[:, c0:c0 + width])

    def put(c0, val):
        out_ref[:, c0:c0 + val.shape[1]] = val.astype(BF16)

    def with_ones(v):
        return jnp.where(upper, 1.0, v)

    mbq, mbk, mbv = mm(C_MBQ, 512), mm(C_MBK, 512), mm(C_MBV, 512)
    mbc, mbsa, mbsb = mbc_ref[...], mbsa_ref[...], mbsb_ref[...]
    per_step = tm // MOBA_BLOCK
    base_blk = lax.rem(pl.program_id(0) * per_step, n_blk)
    row_blk = jnp.right_shift(lax.broadcasted_iota(jnp.int32, (tm, LANES), 0), int(math.log2(MOBA_BLOCK)))
    onehot = lane == HEAD_DIM + base_blk + row_blk
    for h in range(HEADS):
        c = slice(LANES * h, LANES * (h + 1))
        put(O_MBQ + LANES * h, _rope128(mbq[:, c], mbc, mbsa, mbsb, MOBA_ROT // 2) * (HEAD_DIM ** -0.5 * LOG2E))
        k = _rope128(mbk[:, c], mbc, mbsa, mbsb, MOBA_ROT // 2)
        for r in range(per_step):
            kmean_ref[0, r:r + 1, c] = jnp.mean(k[r * MOBA_BLOCK:(r + 1) * MOBA_BLOCK], axis=0, keepdims=True)
        put(O_MBK + LANES * h, jnp.where(onehot, 1.0, k))
        put(O_MBV + LANES * h, with_ones(mbv[:, c]))

    dfq, dfk, dfv = mm(C_DFQ, 256), mm(C_DFK, 256), mm(C_DFV, 512)
    dfc, dfsa, dfsb = dfc_ref[...], dfsa_ref[...], dfsb_ref[...]
    for t in range(2):
        c = slice(LANES * t, LANES * (t + 1))
        put(O_DFQ + LANES * t, _rope128(dfq[:, c], dfc, dfsa, dfsb, DIFF_ROT // 2) * (DIFF_HALF ** -0.5 * LOG2E))
        put(O_DFK + LANES * t, _rope128(dfk[:, c], dfc, dfsa, dfsb, DIFF_ROT // 2))
    for h in range(HEADS):
        put(O_DFV + LANES * h, with_ones(dfv[:, LANES * h:LANES * (h + 1)]))

    sb = mm(C_SB, 768)
    put(O_SBQ, sb[:, 0:256] * 0.125)
    put(O_SBK, sb[:, 256:768])

    lat = mm(C_LAT, 512)
    mlc, mlsa, mlsb = mlc_ref[...], mlsa_ref[...], mlsb_ref[...]
    ql = lat[:, 0:256]
    qn = ql * lax.rsqrt(jnp.sum(ql * ql, axis=-1, keepdims=True) * (1.0 / MLA_Q_LORA) + RMS_EPS) * gq_ref[...]
    qm = _dot(qn.astype(BF16), wuq_ref[...])
    kvl = lat[:, 256:384]
    kvn = kvl * lax.rsqrt(jnp.mean(kvl * kvl, axis=-1, keepdims=True) + RMS_EPS) * gkv_ref[...]
    kv = _dot(kvn.astype(BF16), wukv_ref[...])
    kpe = _rope128(lat[:, 384:512], mlc, mlsa, mlsb, MLA_ROPE // 2)
    mscale = (MLA_NOPE + MLA_ROPE) ** -0.5 * LOG2E
    for h in range(HEADS):
        c = slice(LANES * h, LANES * (h + 1))
        put(O_MLQ + LANES * h, _rope128(qm[:, c], mlc, mlsa, mlsb, MLA_ROPE // 2) * mscale)
        put(O_MLK + LANES * h, kv[:, c] + kpe)
        put(O_MLV + LANES * h, with_ones(kv[:, 512 + LANES * h:512 + LANES * (h + 1)]))


def _proj_call(x, w1, wuq, wukv, gq, gkv, tables, n_blk):
    t_tokens = x.shape[0]
    n_steps = t_tokens // PROJ_TM
    full = lambda shape: pl.BlockSpec(shape, lambda i: (0,) * len(shape))
    tab_spec = pl.BlockSpec((PROJ_TM, LANES), lambda i: (i, 0))
    return pl.pallas_call(
        functools.partial(_proj_kernel, n_blk),
        grid=(n_steps,),
        in_specs=[pl.BlockSpec((PROJ_TM, D_MODEL), lambda i: (i, 0)),
                  full((D_MODEL, PROJ_IN_COLS)), full((256, 512)), full((128, 1024)),
                  full((1, 256)), full((1, 128))] + [tab_spec] * 9,
        out_specs=[pl.BlockSpec((PROJ_TM, PROJ_OUT_COLS), lambda i: (i, 0)),
                   pl.BlockSpec((1, PROJ_TM // MOBA_BLOCK, 512), lambda i: (i, 0, 0))],
        out_shape=[jax.ShapeDtypeStruct((t_tokens, PROJ_OUT_COLS), BF16),
                   jax.ShapeDtypeStruct((n_steps, PROJ_TM // MOBA_BLOCK, 512), F32)],
        compiler_params=pltpu.CompilerParams(dimension_semantics=("parallel",),
                                             vmem_limit_bytes=VMEM_LIMIT_BYTES),
        name="proj",
    )(x, w1, wuq, wukv, gq, gkv, *tables)


def _causal_mask(tq, tk):
    return lax.broadcasted_iota(jnp.int32, (tq, tk), 1) <= lax.broadcasted_iota(jnp.int32, (tq, tk), 0)


def _kv_rows(j):
    return pl.ds(pl.multiple_of(j * ATT_TK, ATT_TK), ATT_TK)


def _flash_update(s, v_tile, m_ref, acc_ref, idx):
    m_prev = m_ref[idx]
    m_new = jnp.maximum(m_prev, jnp.max(s, axis=-1, keepdims=True))
    p = jnp.concatenate([jnp.exp2(s[:, LANES * t:LANES * (t + 1)] - m_new) for t in range(s.shape[1] // LANES)],
                        axis=1)
    acc_ref[idx] = jnp.exp2(m_prev - m_new) * acc_ref[idx] + _dot(p.astype(BF16), v_tile)
    m_ref[idx] = m_new


def _flash_init(m_ref, acc_ref):
    m_ref[...] = jnp.full(m_ref.shape, NEG_INF, F32)
    acc_ref[...] = jnp.zeros(acc_ref.shape, F32)


def _normalized(acc):
    return acc / pltpu.roll(acc, HEAD_DIM, axis=1)


def _store_pairs(o_ref, outs):
    lane = _lane_iota(outs[0].shape)
    for t in range(2):
        pair = jnp.where(lane < HEAD_DIM, outs[2 * t], pltpu.roll(outs[2 * t + 1], HEAD_DIM, axis=1))
        o_ref[0, :, LANES * t:LANES * (t + 1)] = pair.astype(o_ref.dtype)


def _pipelined_sweep(qi, block_of, scores, update, buf_a, buf_b):
    scores(block_of(0), True, buf_a)

    def body(p, carry):
        t = 2 * p
        scores(block_of(t + 1), False, buf_b)
        update(buf_a, block_of(t))
        scores(block_of(t + 2), False, buf_a)
        update(buf_b, block_of(t + 1))
        return carry

    lax.fori_loop(0, jnp.right_shift(qi, 1), body, 0)
    odd = lax.rem(qi, 2)

    @pl.when(odd == 1)
    def _():
        scores(block_of(qi), False, buf_b)

    update(buf_a, block_of(qi - odd))

    @pl.when(odd == 1)
    def _():
        update(buf_b, block_of(qi))


def _diag_then_past(qi):
    return lambda step: jnp.where(step == 0, qi, step - 1)


def _mla_kernel(q_ref, k_ref, v_ref, o_ref, m_ref, acc_ref, sa_ref, sb_ref):
    qi = pl.program_id(1)
    diag = _causal_mask(ATT_TQ, ATT_TK)
    _flash_init(m_ref, acc_ref)

    def scores(j, diagonal, buf):
        for h in range(HEADS):
            c = slice(LANES * h, LANES * (h + 1))
            s = _dot_nt(q_ref[0, :, c], k_ref[0, _kv_rows(j), c])
            buf[h] = jnp.where(diag, s, NEG_INF) if diagonal else s

    def update(buf, j):
        for h in range(HEADS):
            _flash_update(buf[h], v_ref[0, _kv_rows(j), LANES * h:LANES * (h + 1)], m_ref, acc_ref, h)

    _pipelined_sweep(qi, _diag_then_past(qi), scores, update, sa_ref, sb_ref)
    _store_pairs(o_ref, [_normalized(acc_ref[h]) for h in range(HEADS)])


def _moba_kernel(q_ref, k_ref, v_ref, kmean_ref, o_ref, m_ref, acc_ref, sa_ref, sb_ref, qx_ref, km_ref):
    qi = pl.program_id(1)
    tq = ATT_TQ
    n_blk = kmean_ref.shape[1]
    diag = _causal_mask(tq, ATT_TK)
    _flash_init(m_ref, acc_ref)
    km_ref[...] = jnp.zeros_like(km_ref)
    km_ref[HEAD_DIM:HEAD_DIM + n_blk, :] = kmean_ref[0]
    blk = _lane_iota((tq, LANES)) - HEAD_DIM
    slot = (blk >= 0) & (blk < n_blk)
    for h in range(HEADS):
        c = slice(LANES * h, LANES * (h + 1))
        qh = q_ref[0, :, c]
        km_hi, km_lo = _split_bf16(km_ref[:, c])
        gate = _dot_nt(qh, km_hi) + _dot_nt(qh, km_lo)
        gate = jnp.where((blk >= 0) & (blk < qi), gate, NEG_INF)
        sel = blk == qi
        for _ in range(MOBA_TOPK):
            mx = jnp.max(gate, axis=-1, keepdims=True)
            first_idx = jnp.min(jnp.where(gate == mx, blk, LANES), axis=-1, keepdims=True)
            pick = (blk == first_idx) & (mx > NEG_INF)
            sel = sel | pick
            gate = jnp.where(pick, NEG_INF, gate)
        bias = jnp.where(sel, 0.0, -MOBA_MASK_BIAS).astype(BF16)
        qx_ref[h] = jnp.where(slot, bias, qh)

    def scores(j, diagonal, buf):
        for h in range(HEADS):
            s = _dot_nt(qx_ref[h], k_ref[0, _kv_rows(j), LANES * h:LANES * (h + 1)])
            buf[h] = jnp.where(diag, s, NEG_INF) if diagonal else s

    def update(buf, j):
        for h in range(HEADS):
            _flash_update(buf[h], v_ref[0, _kv_rows(j), LANES * h:LANES * (h + 1)], m_ref, acc_ref, h)

    _pipelined_sweep(qi, _diag_then_past(qi), scores, update, sa_ref, sb_ref)
    _store_pairs(o_ref, [_normalized(acc_ref[h]) for h in range(HEADS)])


def _diff_kernel(lambda_init, q_ref, k_ref, v_ref, lam_ref, g_ref, o_ref, m_ref, acc_ref, sa_ref, sb_ref, qs_ref):
    qi = pl.program_id(1)
    tq = ATT_TQ
    diag = _causal_mask(tq, ATT_TK)
    _flash_init(m_ref, acc_ref)
    lane = _lane_iota((tq, LANES))
    zero = jnp.zeros((), BF16)
    for t in range(2):
        qt = q_ref[0, :, LANES * t:LANES * (t + 1)]
        for part in range(4):
            keep = (lane >= DIFF_HALF * part) & (lane < DIFF_HALF * (part + 1))
            qs_ref[t, part * tq:(part + 1) * tq, :] = jnp.where(keep, qt, zero)

    def scores(j, diagonal, buf):
        for t in range(2):
            s_all = _dot_nt(qs_ref[t], k_ref[0, _kv_rows(j), LANES * t:LANES * (t + 1)])
            for part in range(4):
                s = s_all[part * tq:(part + 1) * tq]
                buf[4 * t + part] = jnp.where(diag, s, NEG_INF) if diagonal else s

    def update(buf, j):
        for idx in range(2 * HEADS):
            h = idx // 2
            _flash_update(buf[idx], v_ref[0, _kv_rows(j), LANES * h:LANES * (h + 1)], m_ref, acc_ref, idx)

    _pipelined_sweep(qi, _diag_then_past(qi), scores, update, sa_ref, sb_ref)

    lp = lam_ref[...]
    lam = (jnp.exp(jnp.sum(lp[0:1] * lp[1:2], axis=-1, keepdims=True))
           - jnp.exp(jnp.sum(lp[2:3] * lp[3:4], axis=-1, keepdims=True)) + lambda_init)
    outs = []
    for h in range(HEADS):
        o = _normalized(acc_ref[2 * h]) - lam * _normalized(acc_ref[2 * h + 1])
        ms = jnp.sum(jnp.where(lane < HEAD_DIM, o * o, 0.0), axis=-1, keepdims=True) * (1.0 / HEAD_DIM)
        outs.append(o * lax.rsqrt(ms + RMS_EPS) * g_ref[...] * (1.0 - lambda_init))
    _store_pairs(o_ref, outs)


def _sb_kernel(q_ref, k_ref, v_ref, o_ref, cum_ref, acc_ref, za_ref, zb_ref, qs_ref):
    qi = pl.program_id(1)
    tq, tk = ATT_TQ, ATT_TK
    lane = _lane_iota((tq, LANES))
    zero = jnp.zeros((), BF16)
    cum_ref[...] = jnp.zeros(cum_ref.shape, F32)
    acc_ref[...] = jnp.zeros(acc_ref.shape, F32)
    for t in range(2):
        qt = q_ref[0, :, LANES * t:LANES * (t + 1)]
        qs_ref[t, 0:tq, :] = jnp.where(lane < HEAD_DIM, qt, zero)
        qs_ref[t, tq:2 * tq, :] = jnp.where(lane >= HEAD_DIM, qt, zero)
    row = lax.broadcasted_iota(jnp.int32, (tk, tk), 0)
    col = lax.broadcasted_iota(jnp.int32, (tk, tk), 1)
    u_tri = jnp.where(row > col, 1.0, 0.0).astype(BF16)
    row2 = lax.broadcasted_iota(jnp.int32, (2 * tq, tk), 0)
    col2 = lax.broadcasted_iota(jnp.int32, (2 * tq, tk), 1)
    past = col2 < jnp.where(row2 >= tq, row2 - tq, row2)

    def scores(j, diagonal, buf):
        for t in range(2):
            z = _dot_nt(qs_ref[t], k_ref[0, _kv_rows(j), LANES * t:LANES * (t + 1)])
            buf[t] = jnp.where(past, z, NEG_INF) if diagonal else z

    def update(buf, j):
        for t in range(2):
            z = buf[t]
            sp = jnp.maximum(z, 0.0) + jnp.log1p(jnp.exp(-jnp.abs(z)))
            hi, lo = _split_bf16(-sp)
            after = _dot(hi, u_tri) + _dot(lo, u_tri)
            cum = cum_ref[t]
            a = jnp.concatenate(
                [jnp.exp(z[:, LANES * n:LANES * (n + 1)] - sp[:, LANES * n:LANES * (n + 1)]
                         + after[:, LANES * n:LANES * (n + 1)] + cum) for n in range(tk // LANES)], axis=1)
            acc_ref[t] += _dot(a.astype(BF16), v_ref[0, _kv_rows(j), LANES * t:LANES * (t + 1)])
            cum_ref[t] = cum - jnp.sum(sp, axis=-1, keepdims=True)

    _pipelined_sweep(qi, lambda step: qi - step, scores, update, za_ref, zb_ref)
    for t in range(2):
        acc = acc_ref[t]
        o_ref[0, :, LANES * t:LANES * (t + 1)] = jnp.where(lane < HEAD_DIM, acc[0:tq], acc[tq:2 * tq]).astype(o_ref.dtype)


def _attn_call(kernel, name, proj3, q_off, k_off, v_off, q_width, k_width, v_width, scratch,
               extra_in=(), extra_specs=()):
    bsz, seq, _ = proj3.shape
    return pl.pallas_call(
        kernel,
        grid=(bsz, seq // ATT_TQ),
        in_specs=[pl.BlockSpec((1, ATT_TQ, q_width), lambda b, i: (b, i, q_off // q_width)),
                  pl.BlockSpec((1, seq, k_width), lambda b, i: (b, 0, k_off // k_width)),
                  pl.BlockSpec((1, seq, v_width), lambda b, i: (b, 0, v_off // v_width))] + list(extra_specs),
        out_specs=pl.BlockSpec((1, ATT_TQ, GROUP_WIDTH), lambda b, i: (b, i, 0)),
        out_shape=jax.ShapeDtypeStruct((bsz, seq, GROUP_WIDTH), BF16),
        scratch_shapes=scratch,
        compiler_params=pltpu.CompilerParams(dimension_semantics=("parallel", "parallel"),
                                             vmem_limit_bytes=VMEM_LIMIT_BYTES),
        name=name,
    )(proj3, proj3, proj3, *extra_in)


def _state(n):
    return pltpu.VMEM((n, ATT_TQ, LANES), F32)


def _staging(n, rows):
    return [pltpu.VMEM((n, rows, ATT_TK), F32), pltpu.VMEM((n, rows, ATT_TK), F32)]


def _memkv_kernel(mem_ref, wk_ref, wv_ref, k_ref, v_ref):
    mb = mem_ref[...].astype(BF16)
    k_ref[...] = _dot(mb, wk_ref[...]).astype(BF16)
    v_ref[...] = _dot(mb, wv_ref[...]).astype(BF16)


def _memkv_call(mem2, wk, wv):
    rows = mem2.shape[0]
    full = pl.BlockSpec((D_MODEL, D_MODEL), lambda i: (0, 0))
    blk = pl.BlockSpec((N_MEM, D_MODEL), lambda i: (i, 0))
    return pl.pallas_call(
        _memkv_kernel,
        grid=(rows // N_MEM,),
        in_specs=[blk, full, full],
        out_specs=[blk, blk],
        out_shape=[jax.ShapeDtypeStruct((rows, D_MODEL), BF16)] * 2,
        compiler_params=pltpu.CompilerParams(dimension_semantics=("parallel",),
                                             vmem_limit_bytes=VMEM_LIMIT_BYTES),
        name="memkv",
    )(mem2, wk, wv)


def _post_kernel(x_ref, oa_ref, ob_ref, oc_ref, od_ref, wout_ref, g1_ref, b1_ref,
                 wq_ref, km_ref, vm_ref, wo_ref, g2_ref, b2_ref, wrh_ref, wrl_ref, br_ref,
                 x2_ref, x2b_ref, gates_ref, oh_ref):
    mix = (_dot(oa_ref[...], wout_ref[0:256, :]) + _dot(ob_ref[...], wout_ref[256:512, :])
           + _dot(oc_ref[...], wout_ref[512:768, :]) + _dot(od_ref[...], wout_ref[768:1024, :]))
    x1 = _layer_norm(DEEPNORM_ALPHA * x_ref[...] + mix, g1_ref[...], b1_ref[...])

    q = (_dot(x1.astype(BF16), wq_ref[...]) * (XATTN_HEAD_DIM ** -0.5)).astype(BF16)
    for h in range(XATTN_HEADS):
        c = slice(XATTN_HEAD_DIM * h, XATTN_HEAD_DIM * (h + 1))
        s = _dot_nt(q[:, c], km_ref[0, :, c])
        p = jnp.exp(s - jnp.max(s, axis=-1, keepdims=True))
        o = _dot(p.astype(BF16), vm_ref[0, :, c]) / jnp.sum(p, axis=-1, keepdims=True)
        oh_ref[:, c] = o.astype(BF16)
    xa = _dot(oh_ref[...], wo_ref[...])
    x2 = _layer_norm(DEEPNORM_ALPHA * x1 + xa, g2_ref[...], b2_ref[...])
    x2_ref[...] = x2
    x2b_ref[...] = x2.astype(BF16)

    hi, lo = _split_bf16(x2)
    logits = _dot(hi, wrh_ref[...]) + _dot(hi, wrl_ref[...]) + _dot(lo, wrh_ref[...]) + br_ref[...]
    lane = _lane_iota(logits.shape)
    gl = jnp.where(lane < N_GROUPS, logits, NEG_INF)
    gmx = jnp.max(gl, axis=-1, keepdims=True)
    gidx = jnp.min(jnp.where(gl == gmx, lane, ROUTER_LANES), axis=-1, keepdims=True)
    gw = 1.0 / jnp.sum(jnp.exp(gl - gmx), axis=-1, keepdims=True)
    e0 = N_GROUPS + EXPERTS_PER_GROUP * gidx
    el = jnp.where((lane >= e0) & (lane < e0 + EXPERTS_PER_GROUP), logits, NEG_INF)
    m1 = jnp.max(el, axis=-1, keepdims=True)
    i1 = jnp.min(jnp.where(el == m1, lane, ROUTER_LANES), axis=-1, keepdims=True)
    el = jnp.where(lane == i1, NEG_INF, el)
    m2 = jnp.max(el, axis=-1, keepdims=True)
    i2 = jnp.min(jnp.where(el == m2, lane, ROUTER_LANES), axis=-1, keepdims=True)
    e = jnp.exp(m2 - m1)
    w1 = gw / (1.0 + e)
    w2 = gw * e / (1.0 + e)
    gates_ref[...] = jnp.where(lane == i1, w1, 0.0) + jnp.where(lane == i2, w2, 0.0)


def _post_call(x, oa, ob, oc, od, wout, g1, b1, wq, kmem, vmem, wo, g2, b2, wrh, wrl, br, seq):
    t_tokens = x.shape[0]
    tm = PROJ_TM
    per_seq = seq // tm
    row = lambda w: pl.BlockSpec((tm, w), lambda i: (i, 0))
    full = lambda shape: pl.BlockSpec(shape, lambda i: (0,) * len(shape))
    mem_spec = pl.BlockSpec((1, N_MEM, D_MODEL), lambda i: (i // per_seq, 0, 0))
    sq = (D_MODEL, D_MODEL)
    return pl.pallas_call(
        _post_kernel,
        grid=(t_tokens // tm,),
        in_specs=[row(D_MODEL), row(256), row(256), row(256), row(256),
                  full(sq), full((1, D_MODEL)), full((1, D_MODEL)),
                  full(sq), mem_spec, mem_spec, full(sq), full((1, D_MODEL)), full((1, D_MODEL)),
                  full((D_MODEL, ROUTER_LANES)), full((D_MODEL, ROUTER_LANES)), full((1, ROUTER_LANES))],
        out_specs=[row(D_MODEL), row(D_MODEL), row(ROUTER_LANES)],
        out_shape=[jax.ShapeDtypeStruct((t_tokens, D_MODEL), F32),
                   jax.ShapeDtypeStruct((t_tokens, D_MODEL), BF16),
                   jax.ShapeDtypeStruct((t_tokens, ROUTER_LANES), F32)],
        scratch_shapes=[pltpu.VMEM((tm, D_MODEL), BF16)],
        compiler_params=pltpu.CompilerParams(dimension_semantics=("parallel",),
                                             vmem_limit_bytes=VMEM_LIMIT_BYTES),
        name="post",
    )(x, oa, ob, oc, od, wout, g1, b1, wq, kmem, vmem, wo, g2, b2, wrh, wrl, br)


def _moe_kernel(xb_ref, x2_ref, gates_ref, wg_ref, wu_ref, wd_ref, g_ref, b_ref, o_ref, acc_ref):
    grp = pl.program_id(1)

    @pl.when(grp == 0)
    def _():
        acc_ref[...] = jnp.zeros_like(acc_ref)

    xb = xb_ref[...]
    gates = gates_ref[...]
    lane = _lane_iota(gates.shape)
    for j in range(EXPERTS_PER_GROUP):
        e = N_GROUPS + EXPERTS_PER_GROUP * grp + j
        gate_col = jnp.sum(jnp.where(lane == e, gates, 0.0), axis=-1, keepdims=True)
        gg = _dot(xb, wg_ref[j])
        uu = _dot(xb, wu_ref[j])
        hid = gg * (1.0 / (1.0 + jnp.exp(-gg))) * uu * gate_col
        acc_ref[...] += _dot(hid.astype(BF16), wd_ref[j])

    @pl.when(grp == N_GROUPS - 1)
    def _():
        o_ref[...] = _layer_norm(DEEPNORM_ALPHA * x2_ref[...] + acc_ref[...], g_ref[...], b_ref[...])


def _moe_call(x2b, x2, gates, wg, wu, wd, g, b):
    t_tokens = x2.shape[0]
    tm = MOE_TM
    row = lambda w: pl.BlockSpec((tm, w), lambda i, e: (i, 0))
    vec = pl.BlockSpec((1, D_MODEL), lambda i, e: (0, 0))
    return pl.pallas_call(
        _moe_kernel,
        grid=(t_tokens // tm, N_GROUPS),
        in_specs=[row(D_MODEL), row(D_MODEL), row(ROUTER_LANES),
                  pl.BlockSpec((EXPERTS_PER_GROUP, D_MODEL, EXPERT_FF), lambda i, e: (e, 0, 0)),
                  pl.BlockSpec((EXPERTS_PER_GROUP, D_MODEL, EXPERT_FF), lambda i, e: (e, 0, 0)),
                  pl.BlockSpec((EXPERTS_PER_GROUP, EXPERT_FF, D_MODEL), lambda i, e: (e, 0, 0)),
                  vec, vec],
        out_specs=row(D_MODEL),
        out_shape=jax.ShapeDtypeStruct((t_tokens, D_MODEL), F32),
        scratch_shapes=[pltpu.VMEM((tm, D_MODEL), F32)],
        compiler_params=pltpu.CompilerParams(dimension_semantics=("parallel", "arbitrary"),
                                             vmem_limit_bytes=VMEM_LIMIT_BYTES),
        name="moe",
    )(x2b, x2, gates, wg, wu, wd, g, b)


def _rope_lane_tables(positions, rot, period, offset):
    half = rot // 2
    inv = ROPE_THETA ** (-jnp.arange(0, rot, 2, dtype=F32) / rot)
    ang = positions.astype(F32).reshape(-1)[:, None] * inv
    cos, sin = jnp.cos(ang), jnp.sin(ang)
    p = jnp.arange(LANES) % period - offset
    first = (p >= 0) & (p < half)
    second = (p >= half) & (p < rot)
    idx = jnp.clip(jnp.where(second, p - half, p), 0, half - 1)
    cg, sg = cos[:, idx], sin[:, idx]
    c = jnp.where(first | second, cg, 1.0)
    sa = jnp.where(first, -sg, 0.0)
    sb = jnp.where(second, sg, 0.0)
    return c, sa, sb


def _prep_proj_weights(w_in, w_uq, w_ukv, q_norm, kv_norm):
    q_lat, kv_lat, k_rope = w_in[:, 0:192], w_in[:, 192:320], w_in[:, 320:352]
    sb, mb, df = w_in[:, 352:1120], w_in[:, 1120:1888], w_in[:, 1888:2656]
    zeros = lambda n: jnp.zeros((D_MODEL, n), w_in.dtype)

    def head_tiles(w):
        w = w.reshape(w.shape[0], HEADS, HEAD_DIM)
        return jnp.pad(w, ((0, 0), (0, 0), (0, LANES - HEAD_DIM))).reshape(w.shape[0], HEADS * LANES)

    w1 = jnp.concatenate([head_tiles(mb[:, 0:256]), head_tiles(mb[:, 256:512]), head_tiles(mb[:, 512:768]),
                          head_tiles(df[:, 512:768]),
                          q_lat, zeros(64), kv_lat, zeros(64), k_rope, zeros(32),
                          sb, df[:, 0:512]], axis=1)
    wuq = w_uq.reshape(MLA_Q_LORA, HEADS, MLA_NOPE + MLA_ROPE)
    wuq = jnp.pad(wuq, ((0, 256 - MLA_Q_LORA), (0, 0), (0, LANES - MLA_NOPE - MLA_ROPE))).reshape(256, HEADS * LANES)
    wukv = w_ukv.reshape(MLA_KV_LORA, HEADS, 2, HEAD_DIM)
    wk = head_tiles(wukv[:, :, 0].reshape(MLA_KV_LORA, HEADS * HEAD_DIM))
    wv = head_tiles(wukv[:, :, 1].reshape(MLA_KV_LORA, HEADS * HEAD_DIM))
    gq = jnp.pad(q_norm, (0, 256 - MLA_Q_LORA)).reshape(1, 256)
    return (w1.astype(BF16), wuq.astype(BF16), jnp.concatenate([wk, wv], axis=1).astype(BF16),
            gq, kv_norm.reshape(1, MLA_KV_LORA))


def kernel(x, mem, positions, w_in, mla_q_norm, w_uq, mla_kv_norm, w_ukv, diff_lambda, diff_subln,
           w_out, ln_mix_g, ln_mix_b, xattn_wq, xattn_wk, xattn_wv, xattn_wo, ln_mem_g, ln_mem_b,
           router_group_w, router_group_b, router_expert_w, router_expert_b,
           expert_w_gate, expert_w_up, expert_w_down, ln_ffn_g, ln_ffn_b):
    bsz, seq, _ = x.shape
    t_tokens = bsz * seq
    n_blk = seq // MOBA_BLOCK
    assert seq % PROJ_TM == 0 and t_tokens % MOE_TM == 0 and mem.shape[1] == N_MEM

    tables = (_rope_lane_tables(positions, MOBA_ROT, LANES, 0)
              + _rope_lane_tables(positions, DIFF_ROT, DIFF_HALF, 0)
              + _rope_lane_tables(positions, MLA_ROPE, LANES, MLA_NOPE))
    mem2 = mem.reshape(bsz * N_MEM, D_MODEL)
    row = lambda v: v.reshape(1, -1)
    xf = x.reshape(t_tokens, D_MODEL)

    for l in range(DEPTH):
        lambda_init = 0.8 - 0.6 * math.exp(-0.3 * l)
        w1, wuq, wukv, gq, gkv = _prep_proj_weights(w_in[l], w_uq[l], w_ukv[l], mla_q_norm[l], mla_kv_norm[l])
        proj, kmean = _proj_call(xf, w1, wuq, wukv, gq, gkv, tables, n_blk)
        proj3 = proj.reshape(bsz, seq, PROJ_OUT_COLS)
        kmean3 = kmean.reshape(bsz, n_blk, HEADS * LANES)

        o_mla = _attn_call(_mla_kernel, "mla_attn", proj3, O_MLQ, O_MLK, O_MLV, 512, 512, 512,
                           [_state(HEADS), _state(HEADS)] + _staging(HEADS, ATT_TQ))
        o_sb = _attn_call(_sb_kernel, "sb_attn", proj3, O_SBQ, O_SBK, O_SBV, 256, 256, 256,
                          [pltpu.VMEM((2, 2 * ATT_TQ, LANES), F32), pltpu.VMEM((2, 2 * ATT_TQ, LANES), F32)]
                          + _staging(2, 2 * ATT_TQ) + [pltpu.VMEM((2, 2 * ATT_TQ, LANES), BF16)])
        o_mb = _attn_call(_moba_kernel, "moba_attn", proj3, O_MBQ, O_MBK, O_MBV, 512, 512, 512,
                          [_state(HEADS), _state(HEADS)] + _staging(HEADS, ATT_TQ)
                          + [pltpu.VMEM((HEADS, ATT_TQ, LANES), BF16), pltpu.VMEM((LANES, HEADS * LANES), F32)],
                          extra_in=(kmean3,),
                          extra_specs=(pl.BlockSpec((1, n_blk, HEADS * LANES), lambda b, i: (b, 0, 0)),))
        subln = jnp.tile(diff_subln[l], LANES // HEAD_DIM).reshape(1, LANES)
        o_df = _attn_call(functools.partial(_diff_kernel, lambda_init), "diff_attn", proj3,
                          O_DFQ, O_DFK, O_DFV, 256, 256, 512,
                          [_state(2 * HEADS), _state(2 * HEADS)] + _staging(2 * HEADS, ATT_TQ)
                          + [pltpu.VMEM((2, 4 * ATT_TQ, LANES), BF16)],
                          extra_in=(diff_lambda[l], subln),
                          extra_specs=(pl.BlockSpec((4, DIFF_HALF), lambda b, i: (0, 0)),
                                       pl.BlockSpec((1, LANES), lambda b, i: (0, 0))))

        kmem, vmem = _memkv_call(mem2, xattn_wk[l].astype(BF16), xattn_wv[l].astype(BF16))
        wr = jnp.concatenate([router_group_w[l], router_expert_w[l],
                              jnp.zeros((D_MODEL, ROUTER_LANES - N_GROUPS - N_EXPERTS), F32)], axis=1)
        wr_hi = wr.astype(BF16)
        wr_lo = (wr - wr_hi.astype(F32)).astype(BF16)
        br = jnp.concatenate([router_group_b[l], router_expert_b[l],
                              jnp.zeros((ROUTER_LANES - N_GROUPS - N_EXPERTS,), F32)]).reshape(1, ROUTER_LANES)
        flat = lambda o: o.reshape(t_tokens, GROUP_WIDTH)
        x2, x2b, gates = _post_call(
            xf, flat(o_mla), flat(o_sb), flat(o_mb), flat(o_df), w_out[l].astype(BF16),
            row(ln_mix_g[l]), row(ln_mix_b[l]), xattn_wq[l].astype(BF16),
            kmem.reshape(bsz, N_MEM, D_MODEL), vmem.reshape(bsz, N_MEM, D_MODEL), xattn_wo[l].astype(BF16),
            row(ln_mem_g[l]), row(ln_mem_b[l]), wr_hi, wr_lo, br, seq)

        xf = _moe_call(x2b, x2, gates, expert_w_gate[l].astype(BF16), expert_w_up[l].astype(BF16),
                       expert_w_down[l].astype(BF16), row(ln_ffn_g[l]), row(ln_ffn_b[l]))
    return xf.reshape(bsz, seq, D_MODEL)
```

```python
import functools
import math

import jax
import jax.numpy as jnp
from jax import lax
from jax.experimental import pallas as pl
from jax.experimental.pallas import tpu as pltpu

F32 = jnp.float32
BF16 = jnp.bfloat16
NEG_INF = float("-inf")

D_MODEL = 1024
DEPTH = 4
N_MEM = 256
HEAD_DIM = 64
GROUP_WIDTH = 256
HEADS = 4
ROPE_THETA = 500000.0
MLA_Q_LORA = 192
MLA_KV_LORA = 128
MLA_NOPE = 64
MLA_ROPE = 32
MOBA_BLOCK = 256
MOBA_TOPK = 3
MOBA_ROT = 16
DIFF_HALF = 32
DIFF_ROT = 8
XATTN_HEADS = 4
XATTN_HEAD_DIM = 256
N_GROUPS = 8
EXPERTS_PER_GROUP = 4
N_EXPERTS = 32
EXPERT_FF = 256
DEEPNORM_ALPHA = (2 * DEPTH) ** 0.25
LN_EPS = 1e-5
RMS_EPS = 1e-6

LANES = 128
VMEM_LIMIT_BYTES = 56 * 1024 * 1024
PROJ_TM = 512
ATT_TQ = 256
ATT_TK = 256
MOE_TM = 512
MOE_ROWS_PER_STEP = 512

C_MBQ, C_MBK, C_MBV, C_DFV, C_LAT, C_SB, C_DFQ, C_DFK = 0, 512, 1024, 1536, 2048, 2560, 3328, 3584
PROJ_IN_COLS = 3840
O_MBQ, O_MBK, O_MBV, O_DFV, O_MLQ, O_MLK, O_MLV = 0, 512, 1024, 1536, 2048, 2560, 3072
O_SBQ, O_SBK, O_SBV, O_DFQ, O_DFK = 3584, 3840, 4096, 4352, 4608
PROJ_OUT_COLS = 4864
ROUTER_LANES = 128
LOG2E = 1.4426950408889634
MOBA_MASK_BIAS = 2.0 ** 100


def _dot(a, b):
    return jnp.dot(a, b, preferred_element_type=F32)


def _dot_nt(a, b):
    return lax.dot_general(a, b, (((1,), (1,)), ((), ())), preferred_element_type=F32)


def _split_bf16(x):
    hi = x.astype(BF16)
    lo = (x - hi.astype(F32)).astype(BF16)
    return hi, lo


def _layer_norm(x, g, b):
    mu = jnp.mean(x, axis=-1, keepdims=True)
    xc = x - mu
    var = jnp.mean(xc * xc, axis=-1, keepdims=True)
    return xc * lax.rsqrt(var + LN_EPS) * g + b


def _lane_iota(shape):
    return lax.broadcasted_iota(jnp.int32, shape, 1)


def _rope128(t, c, sa, sb, half):
    nxt = pltpu.roll(t, LANES - half, axis=1)
    prv = pltpu.roll(t, half, axis=1)
    return t * c + nxt * sa + prv * sb


def _proj_kernel(n_blk, x_ref, w1_ref, wuq_ref, wukv_ref, gq_ref, gkv_ref,
                 mbc_ref, mbsa_ref, mbsb_ref, dfc_ref, dfsa_ref, dfsb_ref,
                 mlc_ref, mlsa_ref, mlsb_ref, out_ref, kmean_ref):
    xb = x_ref[...].astype(BF16)
    tm = xb.shape[0]
    lane = _lane_iota((tm, LANES))
    upper = lane >= HEAD_DIM

    def mm(c0, width):
        return _dot(xb, w1_ref[:, c0:c0 + width])

    def put(c0, val):
        out_ref[:, c0:c0 + val.shape[1]] = val.astype(BF16)

    def with_ones(v):
        return jnp.where(upper, 1.0, v)

    mbq, mbk, mbv = mm(C_MBQ, 512), mm(C_MBK, 512), mm(C_MBV, 512)
    mbc, mbsa, mbsb = mbc_ref[...], mbsa_ref[...], mbsb_ref[...]
    per_step = tm // MOBA_BLOCK
    base_blk = lax.rem(pl.program_id(0) * per_step, n_blk)
    row_blk = jnp.right_shift(lax.broadcasted_iota(jnp.int32, (tm, LANES), 0), int(math.log2(MOBA_BLOCK)))
    onehot = lane == HEAD_DIM + base_blk + row_blk
    for h in range(HEADS):
        c = slice(LANES * h, LANES * (h + 1))
        put(O_MBQ + LANES * h, _rope128(mbq[:, c], mbc, mbsa, mbsb, MOBA_ROT // 2) * (HEAD_DIM ** -0.5 * LOG2E))
        k = _rope128(mbk[:, c], mbc, mbsa, mbsb, MOBA_ROT // 2)
        for r in range(per_step):
            kmean_ref[0, r:r + 1, c] = jnp.mean(k[r * MOBA_BLOCK:(r + 1) * MOBA_BLOCK], axis=0, keepdims=True)
        put(O_MBK + LANES * h, jnp.where(onehot, 1.0, k))
        put(O_MBV + LANES * h, with_ones(mbv[:, c]))

    dfq, dfk, dfv = mm(C_DFQ, 256), mm(C_DFK, 256), mm(C_DFV, 512)
    dfc, dfsa, dfsb = dfc_ref[...], dfsa_ref[...], dfsb_ref[...]
    for t in range(2):
        c = slice(LANES * t, LANES * (t + 1))
        put(O_DFQ + LANES * t, _rope128(dfq[:, c], dfc, dfsa, dfsb, DIFF_ROT // 2) * (DIFF_HALF ** -0.5 * LOG2E))
        put(O_DFK + LANES * t, _rope128(dfk[:, c], dfc, dfsa, dfsb, DIFF_ROT // 2))
    for h in range(HEADS):
        put(O_DFV + LANES * h, with_ones(dfv[:, LANES * h:LANES * (h + 1)]))

    sb = mm(C_SB, 768)
    put(O_SBQ, sb[:, 0:256] * 0.125)
    put(O_SBK, sb[:, 256:768])

    lat = mm(C_LAT, 512)
    mlc, mlsa, mlsb = mlc_ref[...], mlsa_ref[...], mlsb_ref[...]
    ql = lat[:, 0:256]
    qn = ql * lax.rsqrt(jnp.sum(ql * ql, axis=-1, keepdims=True) * (1.0 / MLA_Q_LORA) + RMS_EPS) * gq_ref[...]
    qm = _dot(qn.astype(BF16), wuq_ref[...])
    kvl = lat[:, 256:384]
    kvn = kvl * lax.rsqrt(jnp.mean(kvl * kvl, axis=-1, keepdims=True) + RMS_EPS) * gkv_ref[...]
    kv = _dot(kvn.astype(BF16), wukv_ref[...])
    kpe = _rope128(lat[:, 384:512], mlc, mlsa, mlsb, MLA_ROPE // 2)
    mscale = (MLA_NOPE + MLA_ROPE) ** -0.5 * LOG2E
    for h in range(HEADS):
        c = slice(LANES * h, LANES * (h + 1))
        put(O_MLQ + LANES * h, _rope128(qm[:, c], mlc, mlsa, mlsb, MLA_ROPE // 2) * mscale)
        put(O_MLK + LANES * h, kv[:, c] + kpe)
        put(O_MLV + LANES * h, with_ones(kv[:, 512 + LANES * h:512 + LANES * (h + 1)]))


def _proj_call(x, w1, wuq, wukv, gq, gkv, tables, n_blk):
    t_tokens = x.shape[0]
    n_steps = t_tokens // PROJ_TM
    full = lambda shape: pl.BlockSpec(shape, lambda i: (0,) * len(shape))
    tab_spec = pl.BlockSpec((PROJ_TM, LANES), lambda i: (i, 0))
    return pl.pallas_call(
        functools.partial(_proj_kernel, n_blk),
        grid=(n_steps,),
        in_specs=[pl.BlockSpec((PROJ_TM, D_MODEL), lambda i: (i, 0)),
                  full((D_MODEL, PROJ_IN_COLS)), full((256, 512)), full((128, 1024)),
                  full((1, 256)), full((1, 128))] + [tab_spec] * 9,
        out_specs=[pl.BlockSpec((PROJ_TM, PROJ_OUT_COLS), lambda i: (i, 0)),
                   pl.BlockSpec((1, PROJ_TM // MOBA_BLOCK, 512), lambda i: (i, 0, 0))],
        out_shape=[jax.ShapeDtypeStruct((t_tokens, PROJ_OUT_COLS), BF16),
                   jax.ShapeDtypeStruct((n_steps, PROJ_TM // MOBA_BLOCK, 512), F32)],
        compiler_params=pltpu.CompilerParams(dimension_semantics=("parallel",),
                                             vmem_limit_bytes=VMEM_LIMIT_BYTES),
        name="proj",
    )(x, w1, wuq, wukv, gq, gkv, *tables)


def _causal_mask(tq, tk):
    return lax.broadcasted_iota(jnp.int32, (tq, tk), 1) <= lax.broadcasted_iota(jnp.int32, (tq, tk), 0)


def _kv_rows(j):
    return pl.ds(pl.multiple_of(j * ATT_TK, ATT_TK), ATT_TK)


def _flash_update(s, v_tile, m_ref, acc_ref, idx):
    m_prev = m_ref[idx]
    m_new = jnp.maximum(m_prev, jnp.max(s, axis=-1, keepdims=True))
    p = jnp.concatenate([jnp.exp2(s[:, LANES * t:LANES * (t + 1)] - m_new) for t in range(s.shape[1] // LANES)],
                        axis=1)
    acc_ref[idx] = jnp.exp2(m_prev - m_new) * acc_ref[idx] + _dot(p.astype(BF16), v_tile)
    m_ref[idx] = m_new


def _flash_init(m_ref, acc_ref):
    m_ref[...] = jnp.full(m_ref.shape, NEG_INF, F32)
    acc_ref[...] = jnp.zeros(acc_ref.shape, F32)


def _normalized(acc):
    return acc / pltpu.roll(acc, HEAD_DIM, axis=1)


def _store_pairs(o_ref, outs):
    lane = _lane_iota(outs[0].shape)
    for t in range(2):
        pair = jnp.where(lane < HEAD_DIM, outs[2 * t], pltpu.roll(outs[2 * t + 1], HEAD_DIM, axis=1))
        o_ref[0, :, LANES * t:LANES * (t + 1)] = pair.astype(o_ref.dtype)


def _pipelined_sweep(qi, block_of, scores, update, buf_a, buf_b):
    scores(block_of(0), True, buf_a)

    def body(p, carry):
        t = 2 * p
        scores(block_of(t + 1), False, buf_b)
        update(buf_a, block_of(t))
        scores(block_of(t + 2), False, buf_a)
        update(buf_b, block_of(t + 1))
        return carry

    lax.fori_loop(0, jnp.right_shift(qi, 1), body, 0)
    odd = lax.rem(qi, 2)

    @pl.when(odd == 1)
    def _():
        scores(block_of(qi), False, buf_b)

    update(buf_a, block_of(qi - odd))

    @pl.when(odd == 1)
    def _():
        update(buf_b, block_of(qi))


def _diag_then_past(qi):
    return lambda step: jnp.where(step == 0, qi, step - 1)


def _mla_kernel(q_ref, k_ref, v_ref, o_ref, m_ref, acc_ref, sa_ref, sb_ref):
    qi = pl.program_id(1)
    diag = _causal_mask(ATT_TQ, ATT_TK)
    _flash_init(m_ref, acc_ref)

    def scores(j, diagonal, buf):
        for h in range(HEADS):
            c = slice(LANES * h, LANES * (h + 1))
            s = _dot_nt(q_ref[0, :, c], k_ref[0, _kv_rows(j), c])
            buf[h] = jnp.where(diag, s, NEG_INF) if diagonal else s

    def update(buf, j):
        for h in range(HEADS):
            _flash_update(buf[h], v_ref[0, _kv_rows(j), LANES * h:LANES * (h + 1)], m_ref, acc_ref, h)

    _pipelined_sweep(qi, _diag_then_past(qi), scores, update, sa_ref, sb_ref)
    _store_pairs(o_ref, [_normalized(acc_ref[h]) for h in range(HEADS)])


def _moba_kernel(q_ref, k_ref, v_ref, kmean_ref, o_ref, m_ref, acc_ref, sa_ref, sb_ref, qx_ref, km_ref):
    qi = pl.program_id(1)
    tq = ATT_TQ
    n_blk = kmean_ref.shape[1]
    diag = _causal_mask(tq, ATT_TK)
    _flash_init(m_ref, acc_ref)
    km_ref[...] = jnp.zeros_like(km_ref)
    km_ref[HEAD_DIM:HEAD_DIM + n_blk, :] = kmean_ref[0]
    blk = _lane_iota((tq, LANES)) - HEAD_DIM
    slot = (blk >= 0) & (blk < n_blk)
    for h in range(HEADS):
        c = slice(LANES * h, LANES * (h + 1))
        qh = q_ref[0, :, c]
        km_hi, km_lo = _split_bf16(km_ref[:, c])
        gate = _dot_nt(qh, km_hi) + _dot_nt(qh, km_lo)
        gate = jnp.where((blk >= 0) & (blk < qi), gate, NEG_INF)
        sel = blk == qi
        for _ in range(MOBA_TOPK):
            mx = jnp.max(gate, axis=-1, keepdims=True)
            first_idx = jnp.min(jnp.where(gate == mx, blk, LANES), axis=-1, keepdims=True)
            pick = (blk == first_idx) & (mx > NEG_INF)
            sel = sel | pick
            gate = jnp.where(pick, NEG_INF, gate)
        bias = jnp.where(sel, 0.0, -MOBA_MASK_BIAS).astype(BF16)
        qx_ref[h] = jnp.where(slot, bias, qh)

    def scores(j, diagonal, buf):
        for h in range(HEADS):
            s = _dot_nt(qx_ref[h], k_ref[0, _kv_rows(j), LANES * h:LANES * (h + 1)])
            buf[h] = jnp.where(diag, s, NEG_INF) if diagonal else s

    def update(buf, j):
        for h in range(HEADS):
            _flash_update(buf[h], v_ref[0, _kv_rows(j), LANES * h:LANES * (h + 1)], m_ref, acc_ref, h)

    _pipelined_sweep(qi, _diag_then_past(qi), scores, update, sa_ref, sb_ref)
    _store_pairs(o_ref, [_normalized(acc_ref[h]) for h in range(HEADS)])


def _diff_kernel(lambda_init, q_ref, k_ref, v_ref, lam_ref, g_ref, o_ref, m_ref, acc_ref, sa_ref, sb_ref, qs_ref):
    qi = pl.program_id(1)
    tq = ATT_TQ
    diag = _causal_mask(tq, ATT_TK)
    _flash_init(m_ref, acc_ref)
    lane = _lane_iota((tq, LANES))
    zero = jnp.zeros((), BF16)
    for t in range(2):
        qt = q_ref[0, :, LANES * t:LANES * (t + 1)]
        for part in range(4):
            keep = (lane >= DIFF_HALF * part) & (lane < DIFF_HALF * (part + 1))
            qs_ref[t, part * tq:(part + 1) * tq, :] = jnp.where(keep, qt, zero)

    def scores(j, diagonal, buf):
        for t in range(2):
            s_all = _dot_nt(qs_ref[t], k_ref[0, _kv_rows(j), LANES * t:LANES * (t + 1)])
            for part in range(4):
                s = s_all[part * tq:(part + 1) * tq]
                buf[4 * t + part] = jnp.where(diag, s, NEG_INF) if diagonal else s

    def update(buf, j):
        for idx in range(2 * HEADS):
            h = idx // 2
            _flash_update(buf[idx], v_ref[0, _kv_rows(j), LANES * h:LANES * (h + 1)], m_ref, acc_ref, idx)

    _pipelined_sweep(qi, _diag_then_past(qi), scores, update, sa_ref, sb_ref)

    lp = lam_ref[...]
    lam = (jnp.exp(jnp.sum(lp[0:1] * lp[1:2], axis=-1, keepdims=True))
           - jnp.exp(jnp.sum(lp[2:3] * lp[3:4], axis=-1, keepdims=True)) + lambda_init)
    outs = []
    for h in range(HEADS):
        o = _normalized(acc_ref[2 * h]) - lam * _normalized(acc_ref[2 * h + 1])
        ms = jnp.sum(jnp.where(lane < HEAD_DIM, o * o, 0.0), axis=-1, keepdims=True) * (1.0 / HEAD_DIM)
        outs.append(o * lax.rsqrt(ms + RMS_EPS) * g_ref[...] * (1.0 - lambda_init))
    _store_pairs(o_ref, outs)


def _sb_kernel(q_ref, k_ref, v_ref, o_ref, cum_ref, acc_ref, za_ref, zb_ref, qs_ref):
    qi = pl.program_id(1)
    tq, tk = ATT_TQ, ATT_TK
    lane = _lane_iota((tq, LANES))
    zero = jnp.zeros((), BF16)
    cum_ref[...] = jnp.zeros(cum_ref.shape, F32)
    acc_ref[...] = jnp.zeros(acc_ref.shape, F32)
    for t in range(2):
        qt = q_ref[0, :, LANES * t:LANES * (t + 1)]
        qs_ref[t, 0:tq, :] = jnp.where(lane < HEAD_DIM, qt, zero)
        qs_ref[t, tq:2 * tq, :] = jnp.where(lane >= HEAD_DIM, qt, zero)
    row = lax.broadcasted_iota(jnp.int32, (tk, tk), 0)
    col = lax.broadcasted_iota(jnp.int32, (tk, tk), 1)
    u_tri = jnp.where(row > col, 1.0, 0.0).astype(BF16)
    row2 = lax.broadcasted_iota(jnp.int32, (2 * tq, tk), 0)
    col2 = lax.broadcasted_iota(jnp.int32, (2 * tq, tk), 1)
    past = col2 < jnp.where(row2 >= tq, row2 - tq, row2)

    def scores(j, diagonal, buf):
        for t in range(2):
            z = _dot_nt(qs_ref[t], k_ref[0, _kv_rows(j), LANES * t:LANES * (t + 1)])
            buf[t] = jnp.where(past, z, NEG_INF) if diagonal else z

    def update(buf, j):
        for t in range(2):
            z = buf[t]
            sp = jnp.maximum(z, 0.0) + jnp.log1p(jnp.exp(-jnp.abs(z)))
            hi, lo = _split_bf16(-sp)
            after = _dot(hi, u_tri) + _dot(lo, u_tri)
            cum = cum_ref[t]
            a = jnp.concatenate(
                [jnp.exp(z[:, LANES * n:LANES * (n + 1)] - sp[:, LANES * n:LANES * (n + 1)]
                         + after[:, LANES * n:LANES * (n + 1)] + cum) for n in range(tk // LANES)], axis=1)
            acc_ref[t] += _dot(a.astype(BF16), v_ref[0, _kv_rows(j), LANES * t:LANES * (t + 1)])
            cum_ref[t] = cum - jnp.sum(sp, axis=-1, keepdims=True)

    _pipelined_sweep(qi, lambda step: qi - step, scores, update, za_ref, zb_ref)
    for t in range(2):
        acc = acc_ref[t]
        o_ref[0, :, LANES * t:LANES * (t + 1)] = jnp.where(lane < HEAD_DIM, acc[0:tq], acc[tq:2 * tq]).astype(o_ref.dtype)


def _attn_call(kernel, name, proj3, q_off, k_off, v_off, q_width, k_width, v_width, scratch,
               extra_in=(), extra_specs=()):
    bsz, seq, _ = proj3.shape
    return pl.pallas_call(
        kernel,
        grid=(bsz, seq // ATT_TQ),
        in_specs=[pl.BlockSpec((1, ATT_TQ, q_width), lambda b, i: (b, i, q_off // q_width)),
                  pl.BlockSpec((1, seq, k_width), lambda b, i: (b, 0, k_off // k_width)),
                  pl.BlockSpec((1, seq, v_width), lambda b, i: (b, 0, v_off // v_width))] + list(extra_specs),
        out_specs=pl.BlockSpec((1, ATT_TQ, GROUP_WIDTH), lambda b, i: (b, i, 0)),
        out_shape=jax.ShapeDtypeStruct((bsz, seq, GROUP_WIDTH), BF16),
        scratch_shapes=scratch,
        compiler_params=pltpu.CompilerParams(dimension_semantics=("parallel", "parallel"),
                                             vmem_limit_bytes=VMEM_LIMIT_BYTES),
        name=name,
    )(proj3, proj3, proj3, *extra_in)


def _state(n):
    return pltpu.VMEM((n, ATT_TQ, LANES), F32)


def _staging(n, rows):
    return [pltpu.VMEM((n, rows, ATT_TK), F32), pltpu.VMEM((n, rows, ATT_TK), F32)]


def _memkv_kernel(mem_ref, wk_ref, wv_ref, k_ref, v_ref):
    mb = mem_ref[...].astype(BF16)
    k_ref[...] = _dot(mb, wk_ref[...]).astype(BF16)
    v_ref[...] = _dot(mb, wv_ref[...]).astype(BF16)


def _memkv_call(mem2, wk, wv):
    rows = mem2.shape[0]
    full = pl.BlockSpec((D_MODEL, D_MODEL), lambda i: (0, 0))
    blk = pl.BlockSpec((N_MEM, D_MODEL), lambda i: (i, 0))
    return pl.pallas_call(
        _memkv_kernel,
        grid=(rows // N_MEM,),
        in_specs=[blk, full, full],
        out_specs=[blk, blk],
        out_shape=[jax.ShapeDtypeStruct((rows, D_MODEL), BF16)] * 2,
        compiler_params=pltpu.CompilerParams(dimension_semantics=("parallel",),
                                             vmem_limit_bytes=VMEM_LIMIT_BYTES),
        name="memkv",
    )(mem2, wk, wv)


def _post_kernel(x_ref, oa_ref, ob_ref, oc_ref, od_ref, wout_ref, g1_ref, b1_ref,
                 wq_ref, km_ref, vm_ref, wo_ref, g2_ref, b2_ref, wrh_ref, wrl_ref, br_ref,
                 x2_ref, gates_ref, counts_ref, oh_ref, cnt_ref):
    mix = (_dot(oa_ref[...], wout_ref[0:256, :]) + _dot(ob_ref[...], wout_ref[256:512, :])
           + _dot(oc_ref[...], wout_ref[512:768, :]) + _dot(od_ref[...], wout_ref[768:1024, :]))
    x1 = _layer_norm(DEEPNORM_ALPHA * x_ref[...] + mix, g1_ref[...], b1_ref[...])

    q = (_dot(x1.astype(BF16), wq_ref[...]) * (XATTN_HEAD_DIM ** -0.5)).astype(BF16)
    for h in range(XATTN_HEADS):
        c = slice(XATTN_HEAD_DIM * h, XATTN_HEAD_DIM * (h + 1))
        s = _dot_nt(q[:, c], km_ref[0, :, c])
        p = jnp.exp(s - jnp.max(s, axis=-1, keepdims=True))
        o = _dot(p.astype(BF16), vm_ref[0, :, c]) / jnp.sum(p, axis=-1, keepdims=True)
        oh_ref[:, c] = o.astype(BF16)
    xa = _dot(oh_ref[...], wo_ref[...])
    x2 = _layer_norm(DEEPNORM_ALPHA * x1 + xa, g2_ref[...], b2_ref[...])
    x2_ref[...] = x2

    hi, lo = _split_bf16(x2)
    logits = _dot(hi, wrh_ref[...]) + _dot(hi, wrl_ref[...]) + _dot(lo, wrh_ref[...]) + br_ref[...]
    lane = _lane_iota(logits.shape)
    gl = jnp.where(lane < N_GROUPS, logits, NEG_INF)
    gmx = jnp.max(gl, axis=-1, keepdims=True)
    gidx = jnp.min(jnp.where(gl == gmx, lane, ROUTER_LANES), axis=-1, keepdims=True)
    gw = 1.0 / jnp.sum(jnp.exp(gl - gmx), axis=-1, keepdims=True)
    e0 = N_GROUPS + EXPERTS_PER_GROUP * gidx
    el = jnp.where((lane >= e0) & (lane < e0 + EXPERTS_PER_GROUP), logits, NEG_INF)
    m1 = jnp.max(el, axis=-1, keepdims=True)
    i1 = jnp.min(jnp.where(el == m1, lane, ROUTER_LANES), axis=-1, keepdims=True)
    el = jnp.where(lane == i1, NEG_INF, el)
    m2 = jnp.max(el, axis=-1, keepdims=True)
    i2 = jnp.min(jnp.where(el == m2, lane, ROUTER_LANES), axis=-1, keepdims=True)
    e = jnp.exp(m2 - m1)
    w1 = gw / (1.0 + e)
    w2 = gw * e / (1.0 + e)

    @pl.when(pl.program_id(0) == 0)
    def _():
        cnt_ref[...] = jnp.zeros_like(cnt_ref)

    tm = logits.shape[0]
    onehot = jnp.where(lane == gidx, 1.0, 0.0)
    earlier = (lax.broadcasted_iota(jnp.int32, (tm, tm), 1) < lax.broadcasted_iota(jnp.int32, (tm, tm), 0))
    before = _dot(jnp.where(earlier, 1.0, 0.0).astype(BF16), onehot.astype(BF16)) + cnt_ref[...]
    rank = jnp.sum(onehot * before, axis=-1, keepdims=True)
    cnt_ref[...] += jnp.sum(onehot, axis=0, keepdims=True)
    counts_ref[...] = cnt_ref[...]
    gates_ref[...] = (jnp.where(lane == i1, w1, 0.0) + jnp.where(lane == i2, w2, 0.0)
                      + jnp.where(lane == 0, gidx.astype(F32), 0.0) + jnp.where(lane == 1, rank, 0.0))


def _post_call(x, oa, ob, oc, od, wout, g1, b1, wq, kmem, vmem, wo, g2, b2, wrh, wrl, br, seq):
    t_tokens = x.shape[0]
    tm = PROJ_TM
    per_seq = seq // tm
    row = lambda w: pl.BlockSpec((tm, w), lambda i: (i, 0))
    full = lambda shape: pl.BlockSpec(shape, lambda i: (0,) * len(shape))
    mem_spec = pl.BlockSpec((1, N_MEM, D_MODEL), lambda i: (i // per_seq, 0, 0))
    sq = (D_MODEL, D_MODEL)
    return pl.pallas_call(
        _post_kernel,
        grid=(t_tokens // tm,),
        in_specs=[row(D_MODEL), row(256), row(256), row(256), row(256),
                  full(sq), full((1, D_MODEL)), full((1, D_MODEL)),
                  full(sq), mem_spec, mem_spec, full(sq), full((1, D_MODEL)), full((1, D_MODEL)),
                  full((D_MODEL, ROUTER_LANES)), full((D_MODEL, ROUTER_LANES)), full((1, ROUTER_LANES))],
        out_specs=[row(D_MODEL), row(ROUTER_LANES), full((1, ROUTER_LANES))],
        out_shape=[jax.ShapeDtypeStruct((t_tokens, D_MODEL), F32),
                   jax.ShapeDtypeStruct((t_tokens, ROUTER_LANES), F32),
                   jax.ShapeDtypeStruct((1, ROUTER_LANES), F32)],
        scratch_shapes=[pltpu.VMEM((tm, D_MODEL), BF16), pltpu.VMEM((1, ROUTER_LANES), F32)],
        compiler_params=pltpu.CompilerParams(dimension_semantics=("arbitrary",),
                                             vmem_limit_bytes=VMEM_LIMIT_BYTES),
        name="post",
    )(x, oa, ob, oc, od, wout, g1, b1, wq, kmem, vmem, wo, g2, b2, wrh, wrl, br)


def _row_dma_wait(src_ref, dst_ref, sem, rows):
    pltpu.make_async_copy(src_ref.at[pl.ds(0, rows)], dst_ref.at[pl.ds(0, rows)], sem).wait()


def _gather_kernel(pos_ref, x_ref, r_ref, xs_init_ref, rs_init_ref, xs_ref, rs_ref, sem_x, sem_r):
    del xs_init_ref, rs_init_ref
    base = pl.program_id(0) * MOE_ROWS_PER_STEP

    def body(r, carry):
        p = pos_ref[0, 0, r]
        pltpu.make_async_copy(x_ref.at[base + r], xs_ref.at[p], sem_x).start()
        pltpu.make_async_copy(r_ref.at[base + r], rs_ref.at[p], sem_r).start()
        return carry

    lax.fori_loop(0, MOE_ROWS_PER_STEP, body, 0, unroll=8)
    _row_dma_wait(x_ref, xs_ref, sem_x, MOE_ROWS_PER_STEP)
    _row_dma_wait(r_ref, rs_ref, sem_r, MOE_ROWS_PER_STEP)


def _scatter_kernel(pos_ref, ys_ref, y_ref, sem):
    base = pl.program_id(0) * MOE_ROWS_PER_STEP

    def body(r, carry):
        pltpu.make_async_copy(ys_ref.at[pos_ref[0, 0, r]], y_ref.at[base + r], sem).start()
        return carry

    lax.fori_loop(0, MOE_ROWS_PER_STEP, body, 0, unroll=8)
    _row_dma_wait(ys_ref, y_ref, sem, MOE_ROWS_PER_STEP)


def _pos_spec():
    return pl.BlockSpec((1, 1, MOE_ROWS_PER_STEP), lambda i: (i, 0, 0), memory_space=pltpu.SMEM)


def _gather_call(pos3, x3d, route3d, cap_rows):
    n_steps = pos3.shape[0]
    any_spec = pl.BlockSpec(memory_space=pl.ANY)
    xs0 = jnp.zeros((cap_rows,) + x3d.shape[1:], x3d.dtype)
    rs0 = jnp.zeros((cap_rows,) + route3d.shape[1:], route3d.dtype)
    return pl.pallas_call(
        _gather_kernel,
        grid=(n_steps,),
        in_specs=[_pos_spec(), any_spec, any_spec, any_spec, any_spec],
        out_specs=[any_spec, any_spec],
        out_shape=[jax.ShapeDtypeStruct(xs0.shape, xs0.dtype), jax.ShapeDtypeStruct(rs0.shape, rs0.dtype)],
        input_output_aliases={3: 0, 4: 1},
        scratch_shapes=[pltpu.SemaphoreType.DMA(()), pltpu.SemaphoreType.DMA(())],
        compiler_params=pltpu.CompilerParams(dimension_semantics=("arbitrary",), has_side_effects=True),
        name="moe_gather",
    )(pos3, x3d, route3d, xs0, rs0)


def _scatter_call(pos3, ys3d, t_tokens):
    n_steps = pos3.shape[0]
    any_spec = pl.BlockSpec(memory_space=pl.ANY)
    return pl.pallas_call(
        _scatter_kernel,
        grid=(n_steps,),
        in_specs=[_pos_spec(), any_spec],
        out_specs=any_spec,
        out_shape=jax.ShapeDtypeStruct((t_tokens,) + ys3d.shape[1:], ys3d.dtype),
        scratch_shapes=[pltpu.SemaphoreType.DMA(())],
        compiler_params=pltpu.CompilerParams(dimension_semantics=("arbitrary",), has_side_effects=True),
        name="moe_scatter",
    )(pos3, ys3d)


def _moe_kernel(tile_grp_ref, n_used_ref, xs_ref, route_ref, wg_ref, wu_ref, wd_ref, g_ref, b_ref, o_ref):
    i = pl.program_id(0)

    @pl.when(i < n_used_ref[0])
    def _():
        grp = tile_grp_ref[i]
        x2 = xs_ref[...]
        xb = x2.astype(BF16)
        route = route_ref[...]
        lane = _lane_iota(route.shape)
        acc = jnp.zeros(x2.shape, F32)
        for j in range(EXPERTS_PER_GROUP):
            e = N_GROUPS + EXPERTS_PER_GROUP * grp + j
            gate_col = jnp.sum(jnp.where(lane == e, route, 0.0), axis=-1, keepdims=True)
            gg = _dot(xb, wg_ref[j])
            uu = _dot(xb, wu_ref[j])
            hid = gg * (1.0 / (1.0 + jnp.exp(-gg))) * uu * gate_col
            acc = acc + _dot(hid.astype(BF16), wd_ref[j])
        o_ref[...] = _layer_norm(DEEPNORM_ALPHA * x2 + acc, g_ref[...], b_ref[...])

    @pl.when(i >= n_used_ref[0])
    def _():
        o_ref[...] = jnp.zeros_like(o_ref)


def _moe_call(tile_grp, n_used, xs, routes, wg, wu, wd, g, b):
    cap_rows = xs.shape[0]
    tm = MOE_TM
    row = lambda w: pl.BlockSpec((tm, w), lambda i, tg, nu: (i, 0))
    vec = pl.BlockSpec((1, D_MODEL), lambda i, tg, nu: (0, 0))
    wspec = lambda shape: pl.BlockSpec(shape, lambda i, tg, nu: (tg[i], 0, 0))
    return pl.pallas_call(
        _moe_kernel,
        grid_spec=pltpu.PrefetchScalarGridSpec(
            num_scalar_prefetch=2,
            grid=(cap_rows // tm,),
            in_specs=[row(D_MODEL), row(ROUTER_LANES),
                      wspec((EXPERTS_PER_GROUP, D_MODEL, EXPERT_FF)),
                      wspec((EXPERTS_PER_GROUP, D_MODEL, EXPERT_FF)),
                      wspec((EXPERTS_PER_GROUP, EXPERT_FF, D_MODEL)),
                      vec, vec],
            out_specs=row(D_MODEL)),
        out_shape=jax.ShapeDtypeStruct((cap_rows, D_MODEL), F32),
        compiler_params=pltpu.CompilerParams(dimension_semantics=("arbitrary",),
                                             vmem_limit_bytes=VMEM_LIMIT_BYTES),
        name="moe",
    )(tile_grp, n_used, xs, routes, wg, wu, wd, g, b)


def _moe_layer(x2, route, counts, wg, wu, wd, g, b):
    t_tokens = x2.shape[0]
    tm = MOE_TM
    n_tiles = t_tokens // tm + N_GROUPS
    cap_rows = n_tiles * tm
    gid = route[:, 0].astype(jnp.int32)
    rank = route[:, 1].astype(jnp.int32)
    cnt = counts[0, :N_GROUPS].astype(jnp.int32)
    tiles_per_grp = (cnt + tm - 1) // tm
    tile_end = jnp.cumsum(tiles_per_grp)
    pos = ((tile_end - tiles_per_grp) * tm)[gid] + rank
    tile_grp = jnp.minimum(jnp.searchsorted(tile_end, jnp.arange(n_tiles, dtype=jnp.int32), side="right"),
                           N_GROUPS - 1).astype(jnp.int32)
    n_used = tile_end[-1:].astype(jnp.int32)
    pos3 = pos.reshape(t_tokens // MOE_ROWS_PER_STEP, 1, MOE_ROWS_PER_STEP)

    xs, rs = _gather_call(pos3, x2.reshape(t_tokens, D_MODEL // LANES, LANES),
                          route.reshape(t_tokens, 1, LANES), cap_rows)
    ys = _moe_call(tile_grp, n_used, xs.reshape(cap_rows, D_MODEL), rs.reshape(cap_rows, LANES),
                   wg, wu, wd, g, b)
    y = _scatter_call(pos3, ys.reshape(cap_rows, D_MODEL // LANES, LANES), t_tokens)
    return y.reshape(t_tokens, D_MODEL)


def _rope_lane_tables(positions, rot, period, offset):
    half = rot // 2
    inv = ROPE_THETA ** (-jnp.arange(0, rot, 2, dtype=F32) / rot)
    ang = positions.astype(F32).reshape(-1)[:, None] * inv
    cos, sin = jnp.cos(ang), jnp.sin(ang)
    p = jnp.arange(LANES) % period - offset
    first = (p >= 0) & (p < half)
    second = (p >= half) & (p < rot)
    idx = jnp.clip(jnp.where(second, p - half, p), 0, half - 1)
    cg, sg = cos[:, idx], sin[:, idx]
    c = jnp.where(first | second, cg, 1.0)
    sa = jnp.where(first, -sg, 0.0)
    sb = jnp.where(second, sg, 0.0)
    return c, sa, sb


def _prep_proj_weights(w_in, w_uq, w_ukv, q_norm, kv_norm):
    q_lat, kv_lat, k_rope = w_in[:, 0:192], w_in[:, 192:320], w_in[:, 320:352]
    sb, mb, df = w_in[:, 352:1120], w_in[:, 1120:1888], w_in[:, 1888:2656]
    zeros = lambda n: jnp.zeros((D_MODEL, n), w_in.dtype)

    def head_tiles(w):
        w = w.reshape(w.shape[0], HEADS, HEAD_DIM)
        return jnp.pad(w, ((0, 0), (0, 0), (0, LANES - HEAD_DIM))).reshape(w.shape[0], HEADS * LANES)

    w1 = jnp.concatenate([head_tiles(mb[:, 0:256]), head_tiles(mb[:, 256:512]), head_tiles(mb[:, 512:768]),
                          head_tiles(df[:, 512:768]),
                          q_lat, zeros(64), kv_lat, zeros(64), k_rope, zeros(32),
                          sb, df[:, 0:512]], axis=1)
    wuq = w_uq.reshape(MLA_Q_LORA, HEADS, MLA_NOPE + MLA_ROPE)
    wuq = jnp.pad(wuq, ((0, 256 - MLA_Q_LORA), (0, 0), (0, LANES - MLA_NOPE - MLA_ROPE))).reshape(256, HEADS * LANES)
    wukv = w_ukv.reshape(MLA_KV_LORA, HEADS, 2, HEAD_DIM)
    wk = head_tiles(wukv[:, :, 0].reshape(MLA_KV_LORA, HEADS * HEAD_DIM))
    wv = head_tiles(wukv[:, :, 1].reshape(MLA_KV_LORA, HEADS * HEAD_DIM))
    gq = jnp.pad(q_norm, (0, 256 - MLA_Q_LORA)).reshape(1, 256)
    return (w1.astype(BF16), wuq.astype(BF16), jnp.concatenate([wk, wv], axis=1).astype(BF16),
            gq, kv_norm.reshape(1, MLA_KV_LORA))


def kernel(x, mem, positions, w_in, mla_q_norm, w_uq, mla_kv_norm, w_ukv, diff_lambda, diff_subln,
           w_out, ln_mix_g, ln_mix_b, xattn_wq, xattn_wk, xattn_wv, xattn_wo, ln_mem_g, ln_mem_b,
           router_group_w, router_group_b, router_expert_w, router_expert_b,
           expert_w_gate, expert_w_up, expert_w_down, ln_ffn_g, ln_ffn_b):
    bsz, seq, _ = x.shape
    t_tokens = bsz * seq
    n_blk = seq // MOBA_BLOCK
    assert seq % PROJ_TM == 0 and t_tokens % MOE_TM == 0 and t_tokens % MOE_ROWS_PER_STEP == 0
    assert mem.shape[1] == N_MEM and n_blk <= LANES - HEAD_DIM

    tables = (_rope_lane_tables(positions, MOBA_ROT, LANES, 0)
              + _rope_lane_tables(positions, DIFF_ROT, DIFF_HALF, 0)
              + _rope_lane_tables(positions, MLA_ROPE, LANES, MLA_NOPE))
    mem2 = mem.reshape(bsz * N_MEM, D_MODEL)
    row = lambda v: v.reshape(1, -1)
    xf = x.reshape(t_tokens, D_MODEL)

    for l in range(DEPTH):
        lambda_init = 0.8 - 0.6 * math.exp(-0.3 * l)
        w1, wuq, wukv, gq, gkv = _prep_proj_weights(w_in[l], w_uq[l], w_ukv[l], mla_q_norm[l], mla_kv_norm[l])
        proj, kmean = _proj_call(xf, w1, wuq, wukv, gq, gkv, tables, n_blk)
        proj3 = proj.reshape(bsz, seq, PROJ_OUT_COLS)
        kmean3 = kmean.reshape(bsz, n_blk, HEADS * LANES)

        o_mla = _attn_call(_mla_kernel, "mla_attn", proj3, O_MLQ, O_MLK, O_MLV, 512, 512, 512,
                           [_state(HEADS), _state(HEADS)] + _staging(HEADS, ATT_TQ))
        o_sb = _attn_call(_sb_kernel, "sb_attn", proj3, O_SBQ, O_SBK, O_SBV, 256, 256, 256,
                          [pltpu.VMEM((2, 2 * ATT_TQ, LANES), F32), pltpu.VMEM((2, 2 * ATT_TQ, LANES), F32)]
                          + _staging(2, 2 * ATT_TQ) + [pltpu.VMEM((2, 2 * ATT_TQ, LANES), BF16)])
        o_mb = _attn_call(_moba_kernel, "moba_attn", proj3, O_MBQ, O_MBK, O_MBV, 512, 512, 512,
                          [_state(HEADS), _state(HEADS)] + _staging(HEADS, ATT_TQ)
                          + [pltpu.VMEM((HEADS, ATT_TQ, LANES), BF16), pltpu.VMEM((LANES, HEADS * LANES), F32)],
                          extra_in=(kmean3,),
                          extra_specs=(pl.BlockSpec((1, n_blk, HEADS * LANES), lambda b, i: (b, 0, 0)),))
        subln = jnp.tile(diff_subln[l], LANES // HEAD_DIM).reshape(1, LANES)
        o_df = _attn_call(functools.partial(_diff_kernel, lambda_init), "diff_attn", proj3,
                          O_DFQ, O_DFK, O_DFV, 256, 256, 512,
                          [_state(2 * HEADS), _state(2 * HEADS)] + _staging(2 * HEADS, ATT_TQ)
                          + [pltpu.VMEM((2, 4 * ATT_TQ, LANES), BF16)],
                          extra_in=(diff_lambda[l], subln),
                          extra_specs=(pl.BlockSpec((4, DIFF_HALF), lambda b, i: (0, 0)),
                                       pl.BlockSpec((1, LANES), lambda b, i: (0, 0))))

        kmem, vmem = _memkv_call(mem2, xattn_wk[l].astype(BF16), xattn_wv[l].astype(BF16))
        wr = jnp.concatenate([router_group_w[l], router_expert_w[l],
                              jnp.zeros((D_MODEL, ROUTER_LANES - N_GROUPS - N_EXPERTS), F32)], axis=1)
        wr_hi = wr.astype(BF16)
        wr_lo = (wr - wr_hi.astype(F32)).astype(BF16)
        br = jnp.concatenate([router_group_b[l], router_expert_b[l],
                              jnp.zeros((ROUTER_LANES - N_GROUPS - N_EXPERTS,), F32)]).reshape(1, ROUTER_LANES)
        flat = lambda o: o.reshape(t_tokens, GROUP_WIDTH)
        x2, route, counts = _post_call(
            xf, flat(o_mla), flat(o_sb), flat(o_mb), flat(o_df), w_out[l].astype(BF16),
            row(ln_mix_g[l]), row(ln_mix_b[l]), xattn_wq[l].astype(BF16),
            kmem.reshape(bsz, N_MEM, D_MODEL), vmem.reshape(bsz, N_MEM, D_MODEL), xattn_wo[l].astype(BF16),
            row(ln_mem_g[l]), row(ln_mem_b[l]), wr_hi, wr_lo, br, seq)

        xf = _moe_layer(x2, route, counts, expert_w_gate[l].astype(BF16), expert_w_up[l].astype(BF16),
                        expert_w_down[l].astype(BF16), row(ln_ffn_g[l]), row(ln_ffn_b[l]))
    return xf.reshape(bsz, seq, D_MODEL)
```

```python
import functools
import math

import jax
import jax.numpy as jnp
from jax import lax
from jax.experimental import pallas as pl
from jax.experimental.pallas import tpu as pltpu

F32 = jnp.float32
BF16 = jnp.bfloat16
NEG_INF = float("-inf")

D_MODEL = 1024
DEPTH = 4
N_MEM = 256
HEAD_DIM = 64
GROUP_WIDTH = 256
HEADS = 4
ROPE_THETA = 500000.0
MLA_Q_LORA = 192
MLA_KV_LORA = 128
MLA_NOPE = 64
MLA_ROPE = 32
MOBA_BLOCK = 256
MOBA_TOPK = 3
MOBA_ROT = 16
DIFF_HALF = 32
DIFF_ROT = 8
XATTN_HEADS = 4
XATTN_HEAD_DIM = 256
N_GROUPS = 8
EXPERTS_PER_GROUP = 4
N_EXPERTS = 32
EXPERT_FF = 256
DEEPNORM_ALPHA = (2 * DEPTH) ** 0.25
LN_EPS = 1e-5
RMS_EPS = 1e-6

LANES = 128
VMEM_LIMIT_BYTES = 56 * 1024 * 1024
PROJ_TM = 512
ATT_TQ = 256
ATT_TK = 256
MOE_TM = 512
MOE_ROWS_PER_STEP = 512

C_MBQ, C_MBK, C_MBV, C_DFV, C_LAT, C_SB, C_DFQ, C_DFK = 0, 512, 1024, 1536, 2048, 2560, 3328, 3584
PROJ_IN_COLS = 3840
O_MBQ, O_MBK, O_MBV, O_DFV, O_MLQ, O_MLK, O_MLV = 0, 512, 1024, 1536, 2048, 2560, 3072
O_SBQ, O_SBK, O_SBV, O_DFQ, O_DFK = 3584, 3840, 4096, 4352, 4608
PROJ_OUT_COLS = 4864
ROUTER_LANES = 128
LOG2E = 1.4426950408889634
MOBA_MASK_BIAS = 2.0 ** 100


def _dot(a, b):
    return jnp.dot(a, b, preferred_element_type=F32)


def _dot_nt(a, b):
    return lax.dot_general(a, b, (((1,), (1,)), ((), ())), preferred_element_type=F32)


def _split_bf16(x):
    hi = x.astype(BF16)
    lo = (x - hi.astype(F32)).astype(BF16)
    return hi, lo


def _layer_norm(x, g, b):
    mu = jnp.mean(x, axis=-1, keepdims=True)
    xc = x - mu
    var = jnp.mean(xc * xc, axis=-1, keepdims=True)
    return xc * lax.rsqrt(var + LN_EPS) * g + b


def _lane_iota(shape):
    return lax.broadcasted_iota(jnp.int32, shape, 1)


def _rope128(t, c, sa, sb, half):
    nxt = pltpu.roll(t, LANES - half, axis=1)
    prv = pltpu.roll(t, half, axis=1)
    return t * c + nxt * sa + prv * sb


def _proj_kernel(n_blk, x_ref, w1_ref, wuq_ref, wukv_ref, gq_ref, gkv_ref,
                 mbc_ref, mbsa_ref, mbsb_ref, dfc_ref, dfsa_ref, dfsb_ref,
                 mlc_ref, mlsa_ref, mlsb_ref, out_ref, kmean_ref):
    xb = x_ref[...].astype(BF16)
    tm = xb.shape[0]
    lane = _lane_iota((tm, LANES))
    upper = lane >= HEAD_DIM

    def mm(c0, width):
        return _dot(xb, w1_ref[:, c0:c0 + width])

    def put(c0, val):
        out_ref[:, c0:c0 + val.shape[1]] = val.astype(BF16)

    def with_ones(v):
        return jnp.where(upper, 1.0, v)

    mbq, mbk, mbv = mm(C_MBQ, 512), mm(C_MBK, 512), mm(C_MBV, 512)
    mbc, mbsa, mbsb = mbc_ref[...], mbsa_ref[...], mbsb_ref[...]
    per_step = tm // MOBA_BLOCK
    base_blk = lax.rem(pl.program_id(0) * per_step, n_blk)
    row_blk = jnp.right_shift(lax.broadcasted_iota(jnp.int32, (tm, LANES), 0), int(math.log2(MOBA_BLOCK)))
    onehot = lane == HEAD_DIM + base_blk + row_blk
    for h in range(HEADS):
        c = slice(LANES * h, LANES * (h + 1))
        put(O_MBQ + LANES * h, _rope128(mbq[:, c], mbc, mbsa, mbsb, MOBA_ROT // 2) * (HEAD_DIM ** -0.5 * LOG2E))
        k = _rope128(mbk[:, c], mbc, mbsa, mbsb, MOBA_ROT // 2)
        for r in range(per_step):
            kmean_ref[0, r:r + 1, c] = jnp.mean(k[r * MOBA_BLOCK:(r + 1) * MOBA_BLOCK], axis=0, keepdims=True)
        put(O_MBK + LANES * h, jnp.where(onehot, 1.0, k))
        put(O_MBV + LANES * h, with_ones(mbv[:, c]))

    dfq, dfk, dfv = mm(C_DFQ, 256), mm(C_DFK, 256), mm(C_DFV, 512)
    dfc, dfsa, dfsb = dfc_ref[...], dfsa_ref[...], dfsb_ref[...]
    for t in range(2):
        c = slice(LANES * t, LANES * (t + 1))
        put(O_DFQ + LANES * t, _rope128(dfq[:, c], dfc, dfsa, dfsb, DIFF_ROT // 2) * (DIFF_HALF ** -0.5 * LOG2E))
        put(O_DFK + LANES * t, _rope128(dfk[:, c], dfc, dfsa, dfsb, DIFF_ROT // 2))
    for h in range(HEADS):
        put(O_DFV + LANES * h, with_ones(dfv[:, LANES * h:LANES * (h + 1)]))

    sb = mm(C_SB, 768)
    put(O_SBQ, sb[:, 0:256] * 0.125)
    put(O_SBK, sb[:, 256:768])

    lat = mm(C_LAT, 512)
    mlc, mlsa, mlsb = mlc_ref[...], mlsa_ref[...], mlsb_ref[...]
    ql = lat[:, 0:256]
    qn = ql * lax.rsqrt(jnp.sum(ql * ql, axis=-1, keepdims=True) * (1.0 / MLA_Q_LORA) + RMS_EPS) * gq_ref[...]
    qm = _dot(qn.astype(BF16), wuq_ref[...])
    kvl = lat[:, 256:384]
    kvn = kvl * lax.rsqrt(jnp.mean(kvl * kvl, axis=-1, keepdims=True) + RMS_EPS) * gkv_ref[...]
    kv = _dot(kvn.astype(BF16), wukv_ref[...])
    kpe = _rope128(lat[:, 384:512], mlc, mlsa, mlsb, MLA_ROPE // 2)
    mscale = (MLA_NOPE + MLA_ROPE) ** -0.5 * LOG2E
    for h in range(HEADS):
        c = slice(LANES * h, LANES * (h + 1))
        put(O_MLQ + LANES * h, _rope128(qm[:, c], mlc, mlsa, mlsb, MLA_ROPE // 2) * mscale)
        put(O_MLK + LANES * h, kv[:, c] + kpe)
        put(O_MLV + LANES * h, with_ones(kv[:, 512 + LANES * h:512 + LANES * (h + 1)]))


def _proj_call(x, w1, wuq, wukv, gq, gkv, tables, n_blk):
    t_tokens = x.shape[0]
    n_steps = t_tokens // PROJ_TM
    full = lambda shape: pl.BlockSpec(shape, lambda i: (0,) * len(shape))
    tab_spec = pl.BlockSpec((PROJ_TM, LANES), lambda i: (i, 0))
    return pl.pallas_call(
        functools.partial(_proj_kernel, n_blk),
        grid=(n_steps,),
        in_specs=[pl.BlockSpec((PROJ_TM, D_MODEL), lambda i: (i, 0)),
                  full((D_MODEL, PROJ_IN_COLS)), full((256, 512)), full((128, 1024)),
                  full((1, 256)), full((1, 128))] + [tab_spec] * 9,
        out_specs=[pl.BlockSpec((PROJ_TM, PROJ_OUT_COLS), lambda i: (i, 0)),
                   pl.BlockSpec((1, PROJ_TM // MOBA_BLOCK, 512), lambda i: (i, 0, 0))],
        out_shape=[jax.ShapeDtypeStruct((t_tokens, PROJ_OUT_COLS), BF16),
                   jax.ShapeDtypeStruct((n_steps, PROJ_TM // MOBA_BLOCK, 512), F32)],
        compiler_params=pltpu.CompilerParams(dimension_semantics=("parallel",),
                                             vmem_limit_bytes=VMEM_LIMIT_BYTES),
        name="proj",
    )(x, w1, wuq, wukv, gq, gkv, *tables)


def _causal_mask(tq, tk):
    return lax.broadcasted_iota(jnp.int32, (tq, tk), 1) <= lax.broadcasted_iota(jnp.int32, (tq, tk), 0)


def _kv_rows(j):
    return pl.ds(pl.multiple_of(j * ATT_TK, ATT_TK), ATT_TK)


def _flash_update(s, v_tile, m_ref, acc_ref, idx):
    m_prev = m_ref[idx]
    m_new = jnp.maximum(m_prev, jnp.max(s, axis=-1, keepdims=True))
    p = jnp.concatenate([jnp.exp2(s[:, LANES * t:LANES * (t + 1)] - m_new) for t in range(s.shape[1] // LANES)],
                        axis=1)
    acc_ref[idx] = jnp.exp2(m_prev - m_new) * acc_ref[idx] + _dot(p.astype(BF16), v_tile)
    m_ref[idx] = m_new


def _flash_init(m_ref, acc_ref):
    m_ref[...] = jnp.full(m_ref.shape, NEG_INF, F32)
    acc_ref[...] = jnp.zeros(acc_ref.shape, F32)


def _normalized(acc):
    return acc / pltpu.roll(acc, HEAD_DIM, axis=1)


def _store_pairs(o_ref, outs):
    lane = _lane_iota(outs[0].shape)
    for t in range(2):
        pair = jnp.where(lane < HEAD_DIM, outs[2 * t], pltpu.roll(outs[2 * t + 1], HEAD_DIM, axis=1))
        o_ref[0, :, LANES * t:LANES * (t + 1)] = pair.astype(o_ref.dtype)


def _pipelined_sweep(qi, block_of, scores, update, buf_a, buf_b):
    scores(block_of(0), True, buf_a)

    def body(p, carry):
        t = 2 * p
        scores(block_of(t + 1), False, buf_b)
        update(buf_a, block_of(t))
        scores(block_of(t + 2), False, buf_a)
        update(buf_b, block_of(t + 1))
        return carry

    lax.fori_loop(0, jnp.right_shift(qi, 1), body, 0)
    odd = lax.rem(qi, 2)

    @pl.when(odd == 1)
    def _():
        scores(block_of(qi), False, buf_b)

    update(buf_a, block_of(qi - odd))

    @pl.when(odd == 1)
    def _():
        update(buf_b, block_of(qi))


def _diag_then_past(qi):
    return lambda step: jnp.where(step == 0, qi, step - 1)


def _mla_kernel(q_ref, k_ref, v_ref, o_ref, m_ref, acc_ref, sa_ref, sb_ref):
    qi = pl.program_id(1)
    diag = _causal_mask(ATT_TQ, ATT_TK)
    _flash_init(m_ref, acc_ref)

    def scores(j, diagonal, buf):
        for h in range(HEADS):
            c = slice(LANES * h, LANES * (h + 1))
            s = _dot_nt(q_ref[0, :, c], k_ref[0, _kv_rows(j), c])
            buf[h] = jnp.where(diag, s, NEG_INF) if diagonal else s

    def update(buf, j):
        for h in range(HEADS):
            _flash_update(buf[h], v_ref[0, _kv_rows(j), LANES * h:LANES * (h + 1)], m_ref, acc_ref, h)

    _pipelined_sweep(qi, _diag_then_past(qi), scores, update, sa_ref, sb_ref)
    _store_pairs(o_ref, [_normalized(acc_ref[h]) for h in range(HEADS)])


def _moba_kernel(q_ref, k_ref, v_ref, kmean_ref, o_ref, m_ref, acc_ref, sa_ref, sb_ref, qx_ref, km_ref):
    qi = pl.program_id(1)
    tq = ATT_TQ
    n_blk = kmean_ref.shape[1]
    diag = _causal_mask(tq, ATT_TK)
    _flash_init(m_ref, acc_ref)
    km_ref[...] = jnp.zeros_like(km_ref)
    km_ref[HEAD_DIM:HEAD_DIM + n_blk, :] = kmean_ref[0]
    blk = _lane_iota((tq, LANES)) - HEAD_DIM
    slot = (blk >= 0) & (blk < n_blk)
    for h in range(HEADS):
        c = slice(LANES * h, LANES * (h + 1))
        qh = q_ref[0, :, c]
        km_hi, km_lo = _split_bf16(km_ref[:, c])
        gate = _dot_nt(qh, km_hi) + _dot_nt(qh, km_lo)
        gate = jnp.where((blk >= 0) & (blk < qi), gate, NEG_INF)
        sel = blk == qi
        for _ in range(MOBA_TOPK):
            mx = jnp.max(gate, axis=-1, keepdims=True)
            first_idx = jnp.min(jnp.where(gate == mx, blk, LANES), axis=-1, keepdims=True)
            pick = (blk == first_idx) & (mx > NEG_INF)
            sel = sel | pick
            gate = jnp.where(pick, NEG_INF, gate)
        bias = jnp.where(sel, 0.0, -MOBA_MASK_BIAS).astype(BF16)
        qx_ref[h] = jnp.where(slot, bias, qh)

    def scores(j, diagonal, buf):
        for h in range(HEADS):
            s = _dot_nt(qx_ref[h], k_ref[0, _kv_rows(j), LANES * h:LANES * (h + 1)])
            buf[h] = jnp.where(diag, s, NEG_INF) if diagonal else s

    def update(buf, j):
        for h in range(HEADS):
            _flash_update(buf[h], v_ref[0, _kv_rows(j), LANES * h:LANES * (h + 1)], m_ref, acc_ref, h)

    _pipelined_sweep(qi, _diag_then_past(qi), scores, update, sa_ref, sb_ref)
    _store_pairs(o_ref, [_normalized(acc_ref[h]) for h in range(HEADS)])


def _diff_kernel(lambda_init, q_ref, k_ref, v_ref, lam_ref, g_ref, o_ref, m_ref, acc_ref, sa_ref, sb_ref, qs_ref):
    qi = pl.program_id(1)
    tq = ATT_TQ
    diag = _causal_mask(tq, ATT_TK)
    _flash_init(m_ref, acc_ref)
    lane = _lane_iota((tq, LANES))
    zero = jnp.zeros((), BF16)
    for t in range(2):
        qt = q_ref[0, :, LANES * t:LANES * (t + 1)]
        for part in range(4):
            keep = (lane >= DIFF_HALF * part) & (lane < DIFF_HALF * (part + 1))
            qs_ref[t, part * tq:(part + 1) * tq, :] = jnp.where(keep, qt, zero)

    def scores(j, diagonal, buf):
        for t in range(2):
            s_all = _dot_nt(qs_ref[t], k_ref[0, _kv_rows(j), LANES * t:LANES * (t + 1)])
            for part in range(4):
                s = s_all[part * tq:(part + 1) * tq]
                buf[4 * t + part] = jnp.where(diag, s, NEG_INF) if diagonal else s

    def update(buf, j):
        for idx in range(2 * HEADS):
            h = idx // 2
            _flash_update(buf[idx], v_ref[0, _kv_rows(j), LANES * h:LANES * (h + 1)], m_ref, acc_ref, idx)

    _pipelined_sweep(qi, _diag_then_past(qi), scores, update, sa_ref, sb_ref)

    lp = lam_ref[...]
    lam = (jnp.exp(jnp.sum(lp[0:1] * lp[1:2], axis=-1, keepdims=True))
           - jnp.exp(jnp.sum(lp[2:3] * lp[3:4], axis=-1, keepdims=True)) + lambda_init)
    outs = []
    for h in range(HEADS):
        o = _normalized(acc_ref[2 * h]) - lam * _normalized(acc_ref[2 * h + 1])
        ms = jnp.sum(jnp.where(lane < HEAD_DIM, o * o, 0.0), axis=-1, keepdims=True) * (1.0 / HEAD_DIM)
        outs.append(o * lax.rsqrt(ms + RMS_EPS) * g_ref[...] * (1.0 - lambda_init))
    _store_pairs(o_ref, outs)


def _sb_kernel(q_ref, k_ref, v_ref, o_ref, cum_ref, acc_ref, za_ref, zb_ref, qs_ref):
    qi = pl.program_id(1)
    tq, tk = ATT_TQ, ATT_TK
    lane = _lane_iota((tq, LANES))
    zero = jnp.zeros((), BF16)
    cum_ref[...] = jnp.zeros(cum_ref.shape, F32)
    acc_ref[...] = jnp.zeros(acc_ref.shape, F32)
    for t in range(2):
        qt = q_ref[0, :, LANES * t:LANES * (t + 1)]
        qs_ref[t, 0:tq, :] = jnp.where(lane < HEAD_DIM, qt, zero)
        qs_ref[t, tq:2 * tq, :] = jnp.where(lane >= HEAD_DIM, qt, zero)
    row = lax.broadcasted_iota(jnp.int32, (tk, tk), 0)
    col = lax.broadcasted_iota(jnp.int32, (tk, tk), 1)
    u_tri = jnp.where(row > col, 1.0, 0.0).astype(BF16)
    u2_tri = jnp.concatenate([u_tri, u_tri], axis=0)
    row2 = lax.broadcasted_iota(jnp.int32, (2 * tq, tk), 0)
    col2 = lax.broadcasted_iota(jnp.int32, (2 * tq, tk), 1)
    past = col2 < jnp.where(row2 >= tq, row2 - tq, row2)

    def scores(j, diagonal, buf):
        for t in range(2):
            z = _dot_nt(qs_ref[t], k_ref[0, _kv_rows(j), LANES * t:LANES * (t + 1)])
            buf[t] = jnp.where(past, z, NEG_INF) if diagonal else z

    def update(buf, j):
        for t in range(2):
            z = buf[t]
            sp = jnp.maximum(z, 0.0) + jnp.log(1.0 + jnp.exp(-jnp.abs(z)))
            hi, lo = _split_bf16(sp)
            later = _dot(jnp.concatenate([hi, lo], axis=1), u2_tri)
            cum = cum_ref[t]
            a = jnp.concatenate(
                [jnp.exp(z[:, LANES * n:LANES * (n + 1)] - sp[:, LANES * n:LANES * (n + 1)]
                         - later[:, LANES * n:LANES * (n + 1)] - cum) for n in range(tk // LANES)], axis=1)
            acc_ref[t] += _dot(a.astype(BF16), v_ref[0, _kv_rows(j), LANES * t:LANES * (t + 1)])
            cum_ref[t] = cum + jnp.sum(sp, axis=-1, keepdims=True)

    _pipelined_sweep(qi, lambda step: qi - step, scores, update, za_ref, zb_ref)
    for t in range(2):
        acc = acc_ref[t]
        o_ref[0, :, LANES * t:LANES * (t + 1)] = jnp.where(lane < HEAD_DIM, acc[0:tq], acc[tq:2 * tq]).astype(o_ref.dtype)


def _attn_call(kernel, name, proj3, q_off, k_off, v_off, q_width, k_width, v_width, scratch,
               extra_in=(), extra_specs=()):
    bsz, seq, _ = proj3.shape
    return pl.pallas_call(
        kernel,
        grid=(bsz, seq // ATT_TQ),
        in_specs=[pl.BlockSpec((1, ATT_TQ, q_width), lambda b, i: (b, i, q_off // q_width)),
                  pl.BlockSpec((1, seq, k_width), lambda b, i: (b, 0, k_off // k_width)),
                  pl.BlockSpec((1, seq, v_width), lambda b, i: (b, 0, v_off // v_width))] + list(extra_specs),
        out_specs=pl.BlockSpec((1, ATT_TQ, GROUP_WIDTH), lambda b, i: (b, i, 0)),
        out_shape=jax.ShapeDtypeStruct((bsz, seq, GROUP_WIDTH), BF16),
        scratch_shapes=scratch,
        compiler_params=pltpu.CompilerParams(dimension_semantics=("parallel", "parallel"),
                                             vmem_limit_bytes=VMEM_LIMIT_BYTES),
        name=name,
    )(proj3, proj3, proj3, *extra_in)


def _state(n):
    return pltpu.VMEM((n, ATT_TQ, LANES), F32)


def _staging(n, rows):
    return [pltpu.VMEM((n, rows, ATT_TK), F32), pltpu.VMEM((n, rows, ATT_TK), F32)]


def _memkv_kernel(mem_ref, wk_ref, wv_ref, k_ref, v_ref):
    mb = mem_ref[...].astype(BF16)
    k_ref[...] = _dot(mb, wk_ref[...]).astype(BF16)
    v_ref[...] = _dot(mb, wv_ref[...]).astype(BF16)


def _memkv_call(mem2, wk, wv):
    rows = mem2.shape[0]
    full = pl.BlockSpec((D_MODEL, D_MODEL), lambda i: (0, 0))
    blk = pl.BlockSpec((N_MEM, D_MODEL), lambda i: (i, 0))
    return pl.pallas_call(
        _memkv_kernel,
        grid=(rows // N_MEM,),
        in_specs=[blk, full, full],
        out_specs=[blk, blk],
        out_shape=[jax.ShapeDtypeStruct((rows, D_MODEL), BF16)] * 2,
        compiler_params=pltpu.CompilerParams(dimension_semantics=("parallel",),
                                             vmem_limit_bytes=VMEM_LIMIT_BYTES),
        name="memkv",
    )(mem2, wk, wv)


def _post_kernel(x_ref, oa_ref, ob_ref, oc_ref, od_ref, wout_ref, g1_ref, b1_ref,
                 wq_ref, km_ref, vm_ref, wo_ref, g2_ref, b2_ref, wrh_ref, wrl_ref, br_ref,
                 x2_ref, gates_ref, counts_ref, oh_ref, cnt_ref):
    mix = (_dot(oa_ref[...], wout_ref[0:256, :]) + _dot(ob_ref[...], wout_ref[256:512, :])
           + _dot(oc_ref[...], wout_ref[512:768, :]) + _dot(od_ref[...], wout_ref[768:1024, :]))
    x1 = _layer_norm(DEEPNORM_ALPHA * x_ref[...] + mix, g1_ref[...], b1_ref[...])

    q = (_dot(x1.astype(BF16), wq_ref[...]) * (XATTN_HEAD_DIM ** -0.5)).astype(BF16)
    for h in range(XATTN_HEADS):
        c = slice(XATTN_HEAD_DIM * h, XATTN_HEAD_DIM * (h + 1))
        s = _dot_nt(q[:, c], km_ref[0, :, c])
        p = jnp.exp(s - jnp.max(s, axis=-1, keepdims=True))
        o = _dot(p.astype(BF16), vm_ref[0, :, c]) / jnp.sum(p, axis=-1, keepdims=True)
        oh_ref[:, c] = o.astype(BF16)
    xa = _dot(oh_ref[...], wo_ref[...])
    x2 = _layer_norm(DEEPNORM_ALPHA * x1 + xa, g2_ref[...], b2_ref[...])
    x2_ref[...] = x2

    hi, lo = _split_bf16(x2)
    logits = _dot(hi, wrh_ref[...]) + _dot(hi, wrl_ref[...]) + _dot(lo, wrh_ref[...]) + br_ref[...]
    lane = _lane_iota(logits.shape)
    gl = jnp.where(lane < N_GROUPS, logits, NEG_INF)
    gmx = jnp.max(gl, axis=-1, keepdims=True)
    gidx = jnp.min(jnp.where(gl == gmx, lane, ROUTER_LANES), axis=-1, keepdims=True)
    gw = 1.0 / jnp.sum(jnp.exp(gl - gmx), axis=-1, keepdims=True)
    e0 = N_GROUPS + EXPERTS_PER_GROUP * gidx
    el = jnp.where((lane >= e0) & (lane < e0 + EXPERTS_PER_GROUP), logits, NEG_INF)
    m1 = jnp.max(el, axis=-1, keepdims=True)
    i1 = jnp.min(jnp.where(el == m1, lane, ROUTER_LANES), axis=-1, keepdims=True)
    el = jnp.where(lane == i1, NEG_INF, el)
    m2 = jnp.max(el, axis=-1, keepdims=True)
    i2 = jnp.min(jnp.where(el == m2, lane, ROUTER_LANES), axis=-1, keepdims=True)
    e = jnp.exp(m2 - m1)
    w1 = gw / (1.0 + e)
    w2 = gw * e / (1.0 + e)

    @pl.when(pl.program_id(0) == 0)
    def _():
        cnt_ref[...] = jnp.zeros_like(cnt_ref)

    tm = logits.shape[0]
    onehot = jnp.where(lane == gidx, 1.0, 0.0)
    earlier = (lax.broadcasted_iota(jnp.int32, (tm, tm), 1) < lax.broadcasted_iota(jnp.int32, (tm, tm), 0))
    before = _dot(jnp.where(earlier, 1.0, 0.0).astype(BF16), onehot.astype(BF16)) + cnt_ref[...]
    rank = jnp.sum(onehot * before, axis=-1, keepdims=True)
    cnt_ref[...] += jnp.sum(onehot, axis=0, keepdims=True)
    counts_ref[...] = cnt_ref[...]
    gates_ref[...] = (jnp.where(lane == i1, w1, 0.0) + jnp.where(lane == i2, w2, 0.0)
                      + jnp.where(lane == 0, gidx.astype(F32), 0.0) + jnp.where(lane == 1, rank, 0.0))


def _post_call(x, oa, ob, oc, od, wout, g1, b1, wq, kmem, vmem, wo, g2, b2, wrh, wrl, br, seq):
    t_tokens = x.shape[0]
    tm = PROJ_TM
    per_seq = seq // tm
    row = lambda w: pl.BlockSpec((tm, w), lambda i: (i, 0))
    full = lambda shape: pl.BlockSpec(shape, lambda i: (0,) * len(shape))
    mem_spec = pl.BlockSpec((1, N_MEM, D_MODEL), lambda i: (i // per_seq, 0, 0))
    sq = (D_MODEL, D_MODEL)
    return pl.pallas_call(
        _post_kernel,
        grid=(t_tokens // tm,),
        in_specs=[row(D_MODEL), row(256), row(256), row(256), row(256),
                  full(sq), full((1, D_MODEL)), full((1, D_MODEL)),
                  full(sq), mem_spec, mem_spec, full(sq), full((1, D_MODEL)), full((1, D_MODEL)),
                  full((D_MODEL, ROUTER_LANES)), full((D_MODEL, ROUTER_LANES)), full((1, ROUTER_LANES))],
        out_specs=[row(D_MODEL), row(ROUTER_LANES), full((1, ROUTER_LANES))],
        out_shape=[jax.ShapeDtypeStruct((t_tokens, D_MODEL), F32),
                   jax.ShapeDtypeStruct((t_tokens, ROUTER_LANES), F32),
                   jax.ShapeDtypeStruct((1, ROUTER_LANES), F32)],
        scratch_shapes=[pltpu.VMEM((tm, D_MODEL), BF16), pltpu.VMEM((1, ROUTER_LANES), F32)],
        compiler_params=pltpu.CompilerParams(dimension_semantics=("arbitrary",),
                                             vmem_limit_bytes=VMEM_LIMIT_BYTES),
        name="post",
    )(x, oa, ob, oc, od, wout, g1, b1, wq, kmem, vmem, wo, g2, b2, wrh, wrl, br)


def _row_dma_wait(src_ref, dst_ref, sem, rows):
    pltpu.make_async_copy(src_ref.at[pl.ds(0, rows)], dst_ref.at[pl.ds(0, rows)], sem).wait()


def _gather_kernel(pos_ref, x_ref, r_ref, xs_init_ref, rs_init_ref, xs_ref, rs_ref, sem_x, sem_r):
    del xs_init_ref, rs_init_ref

    def body(r, carry):
        p = pos_ref[0, 0, r]
        pltpu.make_async_copy(x_ref.at[r], xs_ref.at[p], sem_x).start()
        pltpu.make_async_copy(r_ref.at[r], rs_ref.at[p], sem_r).start()
        return carry

    lax.fori_loop(0, MOE_ROWS_PER_STEP, body, 0, unroll=8)
    _row_dma_wait(x_ref, xs_ref, sem_x, MOE_ROWS_PER_STEP)
    _row_dma_wait(r_ref, rs_ref, sem_r, MOE_ROWS_PER_STEP)


def _scatter_kernel(pos_ref, ys_ref, y_ref, sem):
    def body(r, carry):
        pltpu.make_async_copy(ys_ref.at[pos_ref[0, 0, r]], y_ref.at[r], sem).start()
        return carry

    lax.fori_loop(0, MOE_ROWS_PER_STEP, body, 0, unroll=8)
    _row_dma_wait(ys_ref, y_ref, sem, MOE_ROWS_PER_STEP)


def _pos_spec():
    return pl.BlockSpec((1, 1, MOE_ROWS_PER_STEP), lambda i: (i, 0, 0), memory_space=pltpu.SMEM)


def _token_rows_spec(slab):
    return pl.BlockSpec((MOE_ROWS_PER_STEP,) + slab, lambda i: (i, 0, 0))


def _gather_call(pos3, x3d, route3d, cap_rows):
    n_steps = pos3.shape[0]
    any_spec = pl.BlockSpec(memory_space=pl.ANY)
    xs0 = jnp.zeros((cap_rows,) + x3d.shape[1:], x3d.dtype)
    rs0 = jnp.zeros((cap_rows,) + route3d.shape[1:], route3d.dtype)
    return pl.pallas_call(
        _gather_kernel,
        grid=(n_steps,),
        in_specs=[_pos_spec(), _token_rows_spec(x3d.shape[1:]), _token_rows_spec(route3d.shape[1:]),
                  any_spec, any_spec],
        out_specs=[any_spec, any_spec],
        out_shape=[jax.ShapeDtypeStruct(xs0.shape, xs0.dtype), jax.ShapeDtypeStruct(rs0.shape, rs0.dtype)],
        input_output_aliases={3: 0, 4: 1},
        scratch_shapes=[pltpu.SemaphoreType.DMA(()), pltpu.SemaphoreType.DMA(())],
        compiler_params=pltpu.CompilerParams(dimension_semantics=("arbitrary",), has_side_effects=True),
        name="moe_gather",
    )(pos3, x3d, route3d, xs0, rs0)


def _scatter_call(pos3, ys3d, t_tokens):
    n_steps = pos3.shape[0]
    any_spec = pl.BlockSpec(memory_space=pl.ANY)
    return pl.pallas_call(
        _scatter_kernel,
        grid=(n_steps,),
        in_specs=[_pos_spec(), any_spec],
        out_specs=_token_rows_spec(ys3d.shape[1:]),
        out_shape=jax.ShapeDtypeStruct((t_tokens,) + ys3d.shape[1:], ys3d.dtype),
        scratch_shapes=[pltpu.SemaphoreType.DMA(())],
        compiler_params=pltpu.CompilerParams(dimension_semantics=("arbitrary",)),
        name="moe_scatter",
    )(pos3, ys3d)


def _moe_kernel(tile_grp_ref, n_used_ref, xs_ref, route_ref, wg_ref, wu_ref, wd_ref, g_ref, b_ref, o_ref):
    i = pl.program_id(0)

    @pl.when(i < n_used_ref[0])
    def _():
        grp = tile_grp_ref[i]
        x2 = xs_ref[...]
        xb = x2.astype(BF16)
        route = route_ref[...]
        lane = _lane_iota(route.shape)
        acc = jnp.zeros(x2.shape, F32)
        for j in range(EXPERTS_PER_GROUP):
            e = N_GROUPS + EXPERTS_PER_GROUP * grp + j
            gate_col = jnp.sum(jnp.where(lane == e, route, 0.0), axis=-1, keepdims=True)
            gg = _dot(xb, wg_ref[j])
            uu = _dot(xb, wu_ref[j])
            hid = gg * (1.0 / (1.0 + jnp.exp(-gg))) * uu * gate_col
            acc = acc + _dot(hid.astype(BF16), wd_ref[j])
        o_ref[...] = _layer_norm(DEEPNORM_ALPHA * x2 + acc, g_ref[...], b_ref[...])

    @pl.when(i >= n_used_ref[0])
    def _():
        o_ref[...] = jnp.zeros_like(o_ref)


def _moe_call(tile_grp, n_used, xs, routes, wg, wu, wd, g, b):
    cap_rows = xs.shape[0]
    tm = MOE_TM
    row = lambda w: pl.BlockSpec((tm, w), lambda i, tg, nu: (i, 0))
    vec = pl.BlockSpec((1, D_MODEL), lambda i, tg, nu: (0, 0))
    wspec = lambda shape: pl.BlockSpec(shape, lambda i, tg, nu: (tg[i], 0, 0))
    return pl.pallas_call(
        _moe_kernel,
        grid_spec=pltpu.PrefetchScalarGridSpec(
            num_scalar_prefetch=2,
            grid=(cap_rows // tm,),
            in_specs=[row(D_MODEL), row(ROUTER_LANES),
                      wspec((EXPERTS_PER_GROUP, D_MODEL, EXPERT_FF)),
                      wspec((EXPERTS_PER_GROUP, D_MODEL, EXPERT_FF)),
                      wspec((EXPERTS_PER_GROUP, EXPERT_FF, D_MODEL)),
                      vec, vec],
            out_specs=row(D_MODEL)),
        out_shape=jax.ShapeDtypeStruct((cap_rows, D_MODEL), F32),
        compiler_params=pltpu.CompilerParams(dimension_semantics=("arbitrary",),
                                             vmem_limit_bytes=VMEM_LIMIT_BYTES),
        name="moe",
    )(tile_grp, n_used, xs, routes, wg, wu, wd, g, b)


def _moe_layer(x2, route, counts, wg, wu, wd, g, b):
    t_tokens = x2.shape[0]
    tm = MOE_TM
    n_tiles = t_tokens // tm + N_GROUPS
    cap_rows = n_tiles * tm
    gid = route[:, 0].astype(jnp.int32)
    rank = route[:, 1].astype(jnp.int32)
    cnt = counts[0, :N_GROUPS].astype(jnp.int32)
    tiles_per_grp = (cnt + tm - 1) // tm
    tile_end = jnp.cumsum(tiles_per_grp)
    pos = ((tile_end - tiles_per_grp) * tm)[gid] + rank
    tile_ids = jnp.arange(n_tiles, dtype=jnp.int32)
    tile_grp = jnp.minimum(jnp.sum((tile_ids[:, None] >= tile_end[None, :]).astype(jnp.int32), axis=1),
                           N_GROUPS - 1)
    n_used = tile_end[-1:].astype(jnp.int32)
    pos3 = pos.reshape(t_tokens // MOE_ROWS_PER_STEP, 1, MOE_ROWS_PER_STEP)

    xs, rs = _gather_call(pos3, x2.reshape(t_tokens, D_MODEL // LANES, LANES),
                          route.reshape(t_tokens, 1, LANES), cap_rows)
    ys = _moe_call(tile_grp, n_used, xs.reshape(cap_rows, D_MODEL), rs.reshape(cap_rows, LANES),
                   wg, wu, wd, g, b)
    y = _scatter_call(pos3, ys.reshape(cap_rows, D_MODEL // LANES, LANES), t_tokens)
    return y.reshape(t_tokens, D_MODEL)


def _rope_lane_tables(positions, rot, period, offset):
    half = rot // 2
    inv = ROPE_THETA ** (-jnp.arange(0, rot, 2, dtype=F32) / rot)
    ang = positions.astype(F32).reshape(-1)[:, None] * inv
    cos, sin = jnp.cos(ang), jnp.sin(ang)
    p = jnp.arange(LANES) % period - offset
    first = (p >= 0) & (p < half)
    second = (p >= half) & (p < rot)
    idx = jnp.clip(jnp.where(second, p - half, p), 0, half - 1)
    cg, sg = cos[:, idx], sin[:, idx]
    c = jnp.where(first | second, cg, 1.0)
    sa = jnp.where(first, -sg, 0.0)
    sb = jnp.where(second, sg, 0.0)
    return c, sa, sb


def _prep_proj_weights(w_in, w_uq, w_ukv, q_norm, kv_norm):
    q_lat, kv_lat, k_rope = w_in[:, 0:192], w_in[:, 192:320], w_in[:, 320:352]
    sb, mb, df = w_in[:, 352:1120], w_in[:, 1120:1888], w_in[:, 1888:2656]
    zeros = lambda n: jnp.zeros((D_MODEL, n), w_in.dtype)

    def head_tiles(w):
        w = w.reshape(w.shape[0], HEADS, HEAD_DIM)
        return jnp.pad(w, ((0, 0), (0, 0), (0, LANES - HEAD_DIM))).reshape(w.shape[0], HEADS * LANES)

    w1 = jnp.concatenate([head_tiles(mb[:, 0:256]), head_tiles(mb[:, 256:512]), head_tiles(mb[:, 512:768]),
                          head_tiles(df[:, 512:768]),
                          q_lat, zeros(64), kv_lat, zeros(64), k_rope, zeros(32),
                          sb, df[:, 0:512]], axis=1)
    wuq = w_uq.reshape(MLA_Q_LORA, HEADS, MLA_NOPE + MLA_ROPE)
    wuq = jnp.pad(wuq, ((0, 256 - MLA_Q_LORA), (0, 0), (0, LANES - MLA_NOPE - MLA_ROPE))).reshape(256, HEADS * LANES)
    wukv = w_ukv.reshape(MLA_KV_LORA, HEADS, 2, HEAD_DIM)
    wk = head_tiles(wukv[:, :, 0].reshape(MLA_KV_LORA, HEADS * HEAD_DIM))
    wv = head_tiles(wukv[:, :, 1].reshape(MLA_KV_LORA, HEADS * HEAD_DIM))
    gq = jnp.pad(q_norm, (0, 256 - MLA_Q_LORA)).reshape(1, 256)
    return (w1.astype(BF16), wuq.astype(BF16), jnp.concatenate([wk, wv], axis=1).astype(BF16),
            gq, kv_norm.reshape(1, MLA_KV_LORA))


def kernel(x, mem, positions, w_in, mla_q_norm, w_uq, mla_kv_norm, w_ukv, diff_lambda, diff_subln,
           w_out, ln_mix_g, ln_mix_b, xattn_wq, xattn_wk, xattn_wv, xattn_wo, ln_mem_g, ln_mem_b,
           router_group_w, router_group_b, router_expert_w, router_expert_b,
           expert_w_gate, expert_w_up, expert_w_down, ln_ffn_g, ln_ffn_b):
    bsz, seq, _ = x.shape
    t_tokens = bsz * seq
    n_blk = seq // MOBA_BLOCK
    assert seq % PROJ_TM == 0 and t_tokens % MOE_TM == 0 and t_tokens % MOE_ROWS_PER_STEP == 0
    assert mem.shape[1] == N_MEM and n_blk <= LANES - HEAD_DIM

    tables = (_rope_lane_tables(positions, MOBA_ROT, LANES, 0)
              + _rope_lane_tables(positions, DIFF_ROT, DIFF_HALF, 0)
              + _rope_lane_tables(positions, MLA_ROPE, LANES, MLA_NOPE))
    mem2 = mem.reshape(bsz * N_MEM, D_MODEL)
    row = lambda v: v.reshape(1, -1)
    xf = x.reshape(t_tokens, D_MODEL)

    for l in range(DEPTH):
        lambda_init = 0.8 - 0.6 * math.exp(-0.3 * l)
        w1, wuq, wukv, gq, gkv = _prep_proj_weights(w_in[l], w_uq[l], w_ukv[l], mla_q_norm[l], mla_kv_norm[l])
        proj, kmean = _proj_call(xf, w1, wuq, wukv, gq, gkv, tables, n_blk)
        proj3 = proj.reshape(bsz, seq, PROJ_OUT_COLS)
        kmean3 = kmean.reshape(bsz, n_blk, HEADS * LANES)

        o_mla = _attn_call(_mla_kernel, "mla_attn", proj3, O_MLQ, O_MLK, O_MLV, 512, 512, 512,
                           [_state(HEADS), _state(HEADS)] + _staging(HEADS, ATT_TQ))
        o_sb = _attn_call(_sb_kernel, "sb_attn", proj3, O_SBQ, O_SBK, O_SBV, 256, 256, 256,
                          [pltpu.VMEM((2, 2 * ATT_TQ, LANES), F32), pltpu.VMEM((2, 2 * ATT_TQ, LANES), F32)]
                          + _staging(2, 2 * ATT_TQ) + [pltpu.VMEM((2, 2 * ATT_TQ, LANES), BF16)])
        o_mb = _attn_call(_moba_kernel, "moba_attn", proj3, O_MBQ, O_MBK, O_MBV, 512, 512, 512,
                          [_state(HEADS), _state(HEADS)] + _staging(HEADS, ATT_TQ)
                          + [pltpu.VMEM((HEADS, ATT_TQ, LANES), BF16), pltpu.VMEM((LANES, HEADS * LANES), F32)],
                          extra_in=(kmean3,),
                          extra_specs=(pl.BlockSpec((1, n_blk, HEADS * LANES), lambda b, i: (b, 0, 0)),))
        subln = jnp.tile(diff_subln[l], LANES // HEAD_DIM).reshape(1, LANES)
        o_df = _attn_call(functools.partial(_diff_kernel, lambda_init), "diff_attn", proj3,
                          O_DFQ, O_DFK, O_DFV, 256, 256, 512,
                          [_state(2 * HEADS), _state(2 * HEADS)] + _staging(2 * HEADS, ATT_TQ)
                          + [pltpu.VMEM((2, 4 * ATT_TQ, LANES), BF16)],
                          extra_in=(diff_lambda[l], subln),
                          extra_specs=(pl.BlockSpec((4, DIFF_HALF), lambda b, i: (0, 0)),
                                       pl.BlockSpec((1, LANES), lambda b, i: (0, 0))))

        kmem, vmem = _memkv_call(mem2, xattn_wk[l].astype(BF16), xattn_wv[l].astype(BF16))
        wr = jnp.concatenate([router_group_w[l], router_expert_w[l],
                              jnp.zeros((D_MODEL, ROUTER_LANES - N_GROUPS - N_EXPERTS), F32)], axis=1)
        wr_hi = wr.astype(BF16)
        wr_lo = (wr - wr_hi.astype(F32)).astype(BF16)
        br = jnp.concatenate([router_group_b[l], router_expert_b[l],
                              jnp.zeros((ROUTER_LANES - N_GROUPS - N_EXPERTS,), F32)]).reshape(1, ROUTER_LANES)
        flat = lambda o: o.reshape(t_tokens, GROUP_WIDTH)
        x2, route, counts = _post_call(
            xf, flat(o_mla), flat(o_sb), flat(o_mb), flat(o_df), w_out[l].astype(BF16),
            row(ln_mix_g[l]), row(ln_mix_b[l]), xattn_wq[l].astype(BF16),
            kmem.reshape(bsz, N_MEM, D_MODEL), vmem.reshape(bsz, N_MEM, D_MODEL), xattn_wo[l].astype(BF16),
            row(ln_mem_g[l]), row(ln_mem_b[l]), wr_hi, wr_lo, br, seq)

        xf = _moe_layer(x2, route, counts, expert_w_gate[l].astype(BF16), expert_w_up[l].astype(BF16),
                        expert_w_down[l].astype(BF16), row(ln_ffn_g[l]), row(ln_ffn_b[l]))
    return xf.reshape(bsz, seq, D_MODEL)
```

```python
import functools
import math

import jax
import jax.numpy as jnp
from jax import lax
from jax.experimental import pallas as pl
from jax.experimental.pallas import tpu as pltpu

F32 = jnp.float32
BF16 = jnp.bfloat16
NEG_INF = float("-inf")

D_MODEL = 1024
DEPTH = 4
N_MEM = 256
HEAD_DIM = 64
GROUP_WIDTH = 256
HEADS = 4
ROPE_THETA = 500000.0
MLA_Q_LORA = 192
MLA_KV_LORA = 128
MLA_NOPE = 64
MLA_ROPE = 32
MOBA_BLOCK = 256
MOBA_TOPK = 3
MOBA_ROT = 16
DIFF_HALF = 32
DIFF_ROT = 8
XATTN_HEADS = 4
XATTN_HEAD_DIM = 256
N_GROUPS = 8
EXPERTS_PER_GROUP = 4
N_EXPERTS = 32
EXPERT_FF = 256
DEEPNORM_ALPHA = (2 * DEPTH) ** 0.25
LN_EPS = 1e-5
RMS_EPS = 1e-6

LANES = 128
VMEM_LIMIT_BYTES = 56 * 1024 * 1024
PROJ_TM = 512
ATT_TQ = 256
ATT_TK = 256
MOE_TM = 512
MOE_ROWS_PER_STEP = 512

C_MBQ, C_MBK, C_MBV, C_DFV, C_LAT, C_SB, C_DFQ, C_DFK = 0, 512, 1024, 1536, 2048, 2560, 3328, 3584
PROJ_IN_COLS = 3840
O_MBQ, O_MBK, O_MBV, O_DFV, O_MLQ, O_MLK, O_MLV = 0, 512, 1024, 1536, 2048, 2560, 3072
O_SBQ, O_SBK, O_SBV, O_DFQ, O_DFK = 3584, 3840, 4096, 4352, 4608
PROJ_OUT_COLS = 4864
ROUTER_LANES = 128
LOG2E = 1.4426950408889634
MOBA_MASK_BIAS = 2.0 ** 100


def _dot(a, b):
    return jnp.dot(a, b, preferred_element_type=F32)


def _dot_nt(a, b):
    return lax.dot_general(a, b, (((1,), (1,)), ((), ())), preferred_element_type=F32)


def _split_bf16(x):
    hi = x.astype(BF16)
    lo = (x - hi.astype(F32)).astype(BF16)
    return hi, lo


def _layer_norm(x, g, b):
    mu = jnp.mean(x, axis=-1, keepdims=True)
    xc = x - mu
    var = jnp.mean(xc * xc, axis=-1, keepdims=True)
    return xc * lax.rsqrt(var + LN_EPS) * g + b


def _lane_iota(shape):
    return lax.broadcasted_iota(jnp.int32, shape, 1)


def _rope128(t, c, sa, sb, half):
    nxt = pltpu.roll(t, LANES - half, axis=1)
    prv = pltpu.roll(t, half, axis=1)
    return t * c + nxt * sa + prv * sb


def _proj_kernel(n_blk, x_ref, w1_ref, wuq_ref, wukv_ref, gq_ref, gkv_ref,
                 mbc_ref, mbsa_ref, mbsb_ref, dfc_ref, dfsa_ref, dfsb_ref,
                 mlc_ref, mlsa_ref, mlsb_ref, out_ref, kmean_ref):
    xb = x_ref[...].astype(BF16)
    tm = xb.shape[0]
    lane = _lane_iota((tm, LANES))
    upper = lane >= HEAD_DIM

    def mm(c0, width):
        return _dot(xb, w1_ref[:, c0:c0 + width])

    def put(c0, val):
        out_ref[:, c0:c0 + val.shape[1]] = val.astype(BF16)

    def with_ones(v):
        return jnp.where(upper, 1.0, v)

    mbq, mbk, mbv = mm(C_MBQ, 512), mm(C_MBK, 512), mm(C_MBV, 512)
    mbc, mbsa, mbsb = mbc_ref[...], mbsa_ref[...], mbsb_ref[...]
    per_step = tm // MOBA_BLOCK
    base_blk = lax.rem(pl.program_id(0) * per_step, n_blk)
    row_blk = jnp.right_shift(lax.broadcasted_iota(jnp.int32, (tm, LANES), 0), int(math.log2(MOBA_BLOCK)))
    onehot = lane == HEAD_DIM + base_blk + row_blk
    for h in range(HEADS):
        c = slice(LANES * h, LANES * (h + 1))
        put(O_MBQ + LANES * h, _rope128(mbq[:, c], mbc, mbsa, mbsb, MOBA_ROT // 2) * (HEAD_DIM ** -0.5 * LOG2E))
        k = _rope128(mbk[:, c], mbc, mbsa, mbsb, MOBA_ROT // 2)
        for r in range(per_step):
            kmean_ref[0, r:r + 1, c] = jnp.mean(k[r * MOBA_BLOCK:(r + 1) * MOBA_BLOCK], axis=0, keepdims=True)
        put(O_MBK + LANES * h, jnp.where(onehot, 1.0, k))
        put(O_MBV + LANES * h, with_ones(mbv[:, c]))

    dfq, dfk, dfv = mm(C_DFQ, 256), mm(C_DFK, 256), mm(C_DFV, 512)
    dfc, dfsa, dfsb = dfc_ref[...], dfsa_ref[...], dfsb_ref[...]
    for t in range(2):
        c = slice(LANES * t, LANES * (t + 1))
        put(O_DFQ + LANES * t, _rope128(dfq[:, c], dfc, dfsa, dfsb, DIFF_ROT // 2) * (DIFF_HALF ** -0.5 * LOG2E))
        put(O_DFK + LANES * t, _rope128(dfk[:, c], dfc, dfsa, dfsb, DIFF_ROT // 2))
    for h in range(HEADS):
        put(O_DFV + LANES * h, with_ones(dfv[:, LANES * h:LANES * (h + 1)]))

    sb = mm(C_SB, 768)
    put(O_SBQ, sb[:, 0:256] * 0.125)
    put(O_SBK, sb[:, 256:768])

    lat = mm(C_LAT, 512)
    mlc, mlsa, mlsb = mlc_ref[...], mlsa_ref[...], mlsb_ref[...]
    ql = lat[:, 0:256]
    qn = ql * lax.rsqrt(jnp.sum(ql * ql, axis=-1, keepdims=True) * (1.0 / MLA_Q_LORA) + RMS_EPS) * gq_ref[...]
    qm = _dot(qn.astype(BF16), wuq_ref[...])
    kvl = lat[:, 256:384]
    kvn = kvl * lax.rsqrt(jnp.mean(kvl * kvl, axis=-1, keepdims=True) + RMS_EPS) * gkv_ref[...]
    kv = _dot(kvn.astype(BF16), wukv_ref[...])
    kpe = _rope128(lat[:, 384:512], mlc, mlsa, mlsb, MLA_ROPE // 2)
    mscale = (MLA_NOPE + MLA_ROPE) ** -0.5 * LOG2E
    for h in range(HEADS):
        c = slice(LANES * h, LANES * (h + 1))
        put(O_MLQ + LANES * h, _rope128(qm[:, c], mlc, mlsa, mlsb, MLA_ROPE // 2) * mscale)
        put(O_MLK + LANES * h, kv[:, c] + kpe)
        put(O_MLV + LANES * h, with_ones(kv[:, 512 + LANES * h:512 + LANES * (h + 1)]))


def _proj_call(x, w1, wuq, wukv, gq, gkv, tables, n_blk):
    t_tokens = x.shape[0]
    n_steps = t_tokens // PROJ_TM
    full = lambda shape: pl.BlockSpec(shape, lambda i: (0,) * len(shape))
    tab_spec = pl.BlockSpec((PROJ_TM, LANES), lambda i: (i, 0))
    return pl.pallas_call(
        functools.partial(_proj_kernel, n_blk),
        grid=(n_steps,),
        in_specs=[pl.BlockSpec((PROJ_TM, D_MODEL), lambda i: (i, 0)),
                  full((D_MODEL, PROJ_IN_COLS)), full((256, 512)), full((128, 1024)),
                  full((1, 256)), full((1, 128))] + [tab_spec] * 9,
        out_specs=[pl.BlockSpec((PROJ_TM, PROJ_OUT_COLS), lambda i: (i, 0)),
                   pl.BlockSpec((1, PROJ_TM // MOBA_BLOCK, 512), lambda i: (i, 0, 0))],
        out_shape=[jax.ShapeDtypeStruct((t_tokens, PROJ_OUT_COLS), BF16),
                   jax.ShapeDtypeStruct((n_steps, PROJ_TM // MOBA_BLOCK, 512), F32)],
        compiler_params=pltpu.CompilerParams(dimension_semantics=("parallel",),
                                             vmem_limit_bytes=VMEM_LIMIT_BYTES),
        name="proj",
    )(x, w1, wuq, wukv, gq, gkv, *tables)


def _causal_mask(tq, tk):
    return lax.broadcasted_iota(jnp.int32, (tq, tk), 1) <= lax.broadcasted_iota(jnp.int32, (tq, tk), 0)


def _kv_rows(j):
    return pl.ds(pl.multiple_of(j * ATT_TK, ATT_TK), ATT_TK)


def _flash_update(s, v_tile, m_ref, acc_ref, idx):
    m_prev = m_ref[idx]
    m_new = jnp.maximum(m_prev, jnp.max(s, axis=-1, keepdims=True))
    p = jnp.concatenate([jnp.exp2(s[:, LANES * t:LANES * (t + 1)] - m_new) for t in range(s.shape[1] // LANES)],
                        axis=1)
    acc_ref[idx] = jnp.exp2(m_prev - m_new) * acc_ref[idx] + _dot(p.astype(BF16), v_tile)
    m_ref[idx] = m_new


def _flash_init(m_ref, acc_ref):
    m_ref[...] = jnp.full(m_ref.shape, NEG_INF, F32)
    acc_ref[...] = jnp.zeros(acc_ref.shape, F32)


def _normalized(acc):
    return acc / pltpu.roll(acc, HEAD_DIM, axis=1)


def _store_pairs(o_ref, outs):
    lane = _lane_iota(outs[0].shape)
    for t in range(2):
        pair = jnp.where(lane < HEAD_DIM, outs[2 * t], pltpu.roll(outs[2 * t + 1], HEAD_DIM, axis=1))
        o_ref[0, :, LANES * t:LANES * (t + 1)] = pair.astype(o_ref.dtype)


def _pipelined_sweep(qi, block_of, scores, update, buf_a, buf_b):
    scores(block_of(0), True, buf_a)

    def body(p, carry):
        t = 2 * p
        scores(block_of(t + 1), False, buf_b)
        update(buf_a, block_of(t))
        scores(block_of(t + 2), False, buf_a)
        update(buf_b, block_of(t + 1))
        return carry

    lax.fori_loop(0, jnp.right_shift(qi, 1), body, 0)
    odd = lax.rem(qi, 2)

    @pl.when(odd == 1)
    def _():
        scores(block_of(qi), False, buf_b)

    update(buf_a, block_of(qi - odd))

    @pl.when(odd == 1)
    def _():
        update(buf_b, block_of(qi))


def _diag_then_past(qi):
    return lambda step: jnp.where(step == 0, qi, step - 1)


def _mla_kernel(q_ref, k_ref, v_ref, o_ref, m_ref, acc_ref, sa_ref, sb_ref):
    qi = pl.program_id(1)
    diag = _causal_mask(ATT_TQ, ATT_TK)
    _flash_init(m_ref, acc_ref)

    def scores(j, diagonal, buf):
        for h in range(HEADS):
            c = slice(LANES * h, LANES * (h + 1))
            s = _dot_nt(q_ref[0, :, c], k_ref[0, _kv_rows(j), c])
            buf[h] = jnp.where(diag, s, NEG_INF) if diagonal else s

    def update(buf, j):
        for h in range(HEADS):
            _flash_update(buf[h], v_ref[0, _kv_rows(j), LANES * h:LANES * (h + 1)], m_ref, acc_ref, h)

    _pipelined_sweep(qi, _diag_then_past(qi), scores, update, sa_ref, sb_ref)
    _store_pairs(o_ref, [_normalized(acc_ref[h]) for h in range(HEADS)])


def _moba_kernel(q_ref, k_ref, v_ref, kmean_ref, o_ref, m_ref, acc_ref, sa_ref, sb_ref, qx_ref, km_ref):
    qi = pl.program_id(1)
    tq = ATT_TQ
    n_blk = kmean_ref.shape[1]
    diag = _causal_mask(tq, ATT_TK)
    _flash_init(m_ref, acc_ref)
    km_ref[...] = jnp.zeros_like(km_ref)
    km_ref[HEAD_DIM:HEAD_DIM + n_blk, :] = kmean_ref[0]
    blk = _lane_iota((tq, LANES)) - HEAD_DIM
    slot = (blk >= 0) & (blk < n_blk)
    for h in range(HEADS):
        c = slice(LANES * h, LANES * (h + 1))
        qh = q_ref[0, :, c]
        km_hi, km_lo = _split_bf16(km_ref[:, c])
        gate = _dot_nt(qh, km_hi) + _dot_nt(qh, km_lo)
        gate = jnp.where((blk >= 0) & (blk < qi), gate, NEG_INF)
        sel = blk == qi
        for _ in range(MOBA_TOPK):
            mx = jnp.max(gate, axis=-1, keepdims=True)
            first_idx = jnp.min(jnp.where(gate == mx, blk, LANES), axis=-1, keepdims=True)
            pick = (blk == first_idx) & (mx > NEG_INF)
            sel = sel | pick
            gate = jnp.where(pick, NEG_INF, gate)
        bias = jnp.where(sel, 0.0, -MOBA_MASK_BIAS).astype(BF16)
        qx_ref[h] = jnp.where(slot, bias, qh)

    def scores(j, diagonal, buf):
        for h in range(HEADS):
            s = _dot_nt(qx_ref[h], k_ref[0, _kv_rows(j), LANES * h:LANES * (h + 1)])
            buf[h] = jnp.where(diag, s, NEG_INF) if diagonal else s

    def update(buf, j):
        for h in range(HEADS):
            _flash_update(buf[h], v_ref[0, _kv_rows(j), LANES * h:LANES * (h + 1)], m_ref, acc_ref, h)

    _pipelined_sweep(qi, _diag_then_past(qi), scores, update, sa_ref, sb_ref)
    _store_pairs(o_ref, [_normalized(acc_ref[h]) for h in range(HEADS)])


def _diff_kernel(lambda_init, q_ref, k_ref, v_ref, lam_ref, g_ref, o_ref, m_ref, acc_ref, sa_ref, sb_ref, qs_ref):
    qi = pl.program_id(1)
    tq = ATT_TQ
    diag = _causal_mask(tq, ATT_TK)
    _flash_init(m_ref, acc_ref)
    lane = _lane_iota((tq, LANES))
    zero = jnp.zeros((), BF16)
    for t in range(2):
        qt = q_ref[0, :, LANES * t:LANES * (t + 1)]
        for part in range(4):
            keep = (lane >= DIFF_HALF * part) & (lane < DIFF_HALF * (part + 1))
            qs_ref[t, part * tq:(part + 1) * tq, :] = jnp.where(keep, qt, zero)

    def scores(j, diagonal, buf):
        for t in range(2):
            s_all = _dot_nt(qs_ref[t], k_ref[0, _kv_rows(j), LANES * t:LANES * (t + 1)])
            for part in range(4):
                s = s_all[part * tq:(part + 1) * tq]
                buf[4 * t + part] = jnp.where(diag, s, NEG_INF) if diagonal else s

    def update(buf, j):
        for idx in range(2 * HEADS):
            h = idx // 2
            _flash_update(buf[idx], v_ref[0, _kv_rows(j), LANES * h:LANES * (h + 1)], m_ref, acc_ref, idx)

    _pipelined_sweep(qi, _diag_then_past(qi), scores, update, sa_ref, sb_ref)

    lp = lam_ref[...]
    lam = (jnp.exp(jnp.sum(lp[0:1] * lp[1:2], axis=-1, keepdims=True))
           - jnp.exp(jnp.sum(lp[2:3] * lp[3:4], axis=-1, keepdims=True)) + lambda_init)
    outs = []
    for h in range(HEADS):
        o = _normalized(acc_ref[2 * h]) - lam * _normalized(acc_ref[2 * h + 1])
        ms = jnp.sum(jnp.where(lane < HEAD_DIM, o * o, 0.0), axis=-1, keepdims=True) * (1.0 / HEAD_DIM)
        outs.append(o * lax.rsqrt(ms + RMS_EPS) * g_ref[...] * (1.0 - lambda_init))
    _store_pairs(o_ref, outs)


def _sb_kernel(q_ref, k_ref, v_ref, o_ref, cum_ref, acc_ref, za_ref, zb_ref, qs_ref):
    qi = pl.program_id(1)
    tq, tk = ATT_TQ, ATT_TK
    lane = _lane_iota((tq, LANES))
    zero = jnp.zeros((), BF16)
    cum_ref[...] = jnp.zeros(cum_ref.shape, F32)
    acc_ref[...] = jnp.zeros(acc_ref.shape, F32)
    for t in range(2):
        qt = q_ref[0, :, LANES * t:LANES * (t + 1)]
        qs_ref[t, 0:tq, :] = jnp.where(lane < HEAD_DIM, qt, zero)
        qs_ref[t, tq:2 * tq, :] = jnp.where(lane >= HEAD_DIM, qt, zero)
    row = lax.broadcasted_iota(jnp.int32, (tk, tk), 0)
    col = lax.broadcasted_iota(jnp.int32, (tk, tk), 1)
    u_tri = jnp.where(row > col, 1.0, 0.0).astype(BF16)
    u2_tri = jnp.concatenate([u_tri, u_tri], axis=0)
    row2 = lax.broadcasted_iota(jnp.int32, (2 * tq, tk), 0)
    col2 = lax.broadcasted_iota(jnp.int32, (2 * tq, tk), 1)
    past = col2 < jnp.where(row2 >= tq, row2 - tq, row2)

    def scores(j, diagonal, buf):
        for t in range(2):
            z = _dot_nt(qs_ref[t], k_ref[0, _kv_rows(j), LANES * t:LANES * (t + 1)])
            buf[t] = jnp.where(past, z, NEG_INF) if diagonal else z

    def update(buf, j):
        for t in range(2):
            z = buf[t]
            sp = jnp.maximum(z, 0.0) + jnp.log(1.0 + jnp.exp(-jnp.abs(z)))
            hi, lo = _split_bf16(sp)
            later = _dot(jnp.concatenate([hi, lo], axis=1), u2_tri)
            cum = cum_ref[t]
            a = jnp.concatenate(
                [jnp.exp(z[:, LANES * n:LANES * (n + 1)] - sp[:, LANES * n:LANES * (n + 1)]
                         - later[:, LANES * n:LANES * (n + 1)] - cum) for n in range(tk // LANES)], axis=1)
            acc_ref[t] += _dot(a.astype(BF16), v_ref[0, _kv_rows(j), LANES * t:LANES * (t + 1)])
            cum_ref[t] = cum + jnp.sum(sp, axis=-1, keepdims=True)

    _pipelined_sweep(qi, lambda step: qi - step, scores, update, za_ref, zb_ref)
    for t in range(2):
        acc = acc_ref[t]
        o_ref[0, :, LANES * t:LANES * (t + 1)] = jnp.where(lane < HEAD_DIM, acc[0:tq], acc[tq:2 * tq]).astype(o_ref.dtype)


def _attn_call(kernel, name, proj3, q_off, k_off, v_off, q_width, k_width, v_width, scratch,
               extra_in=(), extra_specs=()):
    bsz, seq, _ = proj3.shape
    return pl.pallas_call(
        kernel,
        grid=(bsz, seq // ATT_TQ),
        in_specs=[pl.BlockSpec((1, ATT_TQ, q_width), lambda b, i: (b, i, q_off // q_width)),
                  pl.BlockSpec((1, seq, k_width), lambda b, i: (b, 0, k_off // k_width)),
                  pl.BlockSpec((1, seq, v_width), lambda b, i: (b, 0, v_off // v_width))] + list(extra_specs),
        out_specs=pl.BlockSpec((1, ATT_TQ, GROUP_WIDTH), lambda b, i: (b, i, 0)),
        out_shape=jax.ShapeDtypeStruct((bsz, seq, GROUP_WIDTH), BF16),
        scratch_shapes=scratch,
        compiler_params=pltpu.CompilerParams(dimension_semantics=("parallel", "parallel"),
                                             vmem_limit_bytes=VMEM_LIMIT_BYTES),
        name=name,
    )(proj3, proj3, proj3, *extra_in)


def _state(n):
    return pltpu.VMEM((n, ATT_TQ, LANES), F32)


def _staging(n, rows):
    return [pltpu.VMEM((n, rows, ATT_TK), F32), pltpu.VMEM((n, rows, ATT_TK), F32)]


def _memkv_kernel(mem_ref, wk_ref, wv_ref, k_ref, v_ref):
    mb = mem_ref[...].astype(BF16)
    k_ref[...] = _dot(mb, wk_ref[...]).astype(BF16)
    v_ref[...] = _dot(mb, wv_ref[...]).astype(BF16)


def _memkv_call(mem2, wk, wv):
    rows = mem2.shape[0]
    full = pl.BlockSpec((D_MODEL, D_MODEL), lambda i: (0, 0))
    blk = pl.BlockSpec((N_MEM, D_MODEL), lambda i: (i, 0))
    return pl.pallas_call(
        _memkv_kernel,
        grid=(rows // N_MEM,),
        in_specs=[blk, full, full],
        out_specs=[blk, blk],
        out_shape=[jax.ShapeDtypeStruct((rows, D_MODEL), BF16)] * 2,
        compiler_params=pltpu.CompilerParams(dimension_semantics=("parallel",),
                                             vmem_limit_bytes=VMEM_LIMIT_BYTES),
        name="memkv",
    )(mem2, wk, wv)


def _post_kernel(x_ref, oa_ref, ob_ref, oc_ref, od_ref, wout_ref, g1_ref, b1_ref,
                 wq_ref, km_ref, vm_ref, wo_ref, g2_ref, b2_ref, wrh_ref, wrl_ref, br_ref,
                 x2_ref, gates_ref, counts_ref, oh_ref, cnt_ref):
    mix = (_dot(oa_ref[...], wout_ref[0:256, :]) + _dot(ob_ref[...], wout_ref[256:512, :])
           + _dot(oc_ref[...], wout_ref[512:768, :]) + _dot(od_ref[...], wout_ref[768:1024, :]))
    x1 = _layer_norm(DEEPNORM_ALPHA * x_ref[...] + mix, g1_ref[...], b1_ref[...])

    q = (_dot(x1.astype(BF16), wq_ref[...]) * (XATTN_HEAD_DIM ** -0.5)).astype(BF16)
    for h in range(XATTN_HEADS):
        c = slice(XATTN_HEAD_DIM * h, XATTN_HEAD_DIM * (h + 1))
        s = _dot_nt(q[:, c], km_ref[0, :, c])
        p = jnp.exp(s - jnp.max(s, axis=-1, keepdims=True))
        o = _dot(p.astype(BF16), vm_ref[0, :, c]) / jnp.sum(p, axis=-1, keepdims=True)
        oh_ref[:, c] = o.astype(BF16)
    xa = _dot(oh_ref[...], wo_ref[...])
    x2 = _layer_norm(DEEPNORM_ALPHA * x1 + xa, g2_ref[...], b2_ref[...])
    x2_ref[...] = x2

    hi, lo = _split_bf16(x2)
    logits = _dot(hi, wrh_ref[...]) + _dot(hi, wrl_ref[...]) + _dot(lo, wrh_ref[...]) + br_ref[...]
    lane = _lane_iota(logits.shape)
    gl = jnp.where(lane < N_GROUPS, logits, NEG_INF)
    gmx = jnp.max(gl, axis=-1, keepdims=True)
    gidx = jnp.min(jnp.where(gl == gmx, lane, ROUTER_LANES), axis=-1, keepdims=True)
    gw = 1.0 / jnp.sum(jnp.exp(gl - gmx), axis=-1, keepdims=True)
    e0 = N_GROUPS + EXPERTS_PER_GROUP * gidx
    el = jnp.where((lane >= e0) & (lane < e0 + EXPERTS_PER_GROUP), logits, NEG_INF)
    m1 = jnp.max(el, axis=-1, keepdims=True)
    i1 = jnp.min(jnp.where(el == m1, lane, ROUTER_LANES), axis=-1, keepdims=True)
    el = jnp.where(lane == i1, NEG_INF, el)
    m2 = jnp.max(el, axis=-1, keepdims=True)
    i2 = jnp.min(jnp.where(el == m2, lane, ROUTER_LANES), axis=-1, keepdims=True)
    e = jnp.exp(m2 - m1)
    w1 = gw / (1.0 + e)
    w2 = gw * e / (1.0 + e)

    @pl.when(pl.program_id(0) == 0)
    def _():
        cnt_ref[...] = jnp.zeros_like(cnt_ref)

    tm = logits.shape[0]
    onehot = jnp.where(lane == gidx, 1.0, 0.0)
    earlier = (lax.broadcasted_iota(jnp.int32, (tm, tm), 1) < lax.broadcasted_iota(jnp.int32, (tm, tm), 0))
    before = _dot(jnp.where(earlier, 1.0, 0.0).astype(BF16), onehot.astype(BF16)) + cnt_ref[...]
    rank = jnp.sum(onehot * before, axis=-1, keepdims=True)
    cnt_ref[...] += jnp.sum(onehot, axis=0, keepdims=True)
    counts_ref[...] = cnt_ref[...]
    gates_ref[...] = (jnp.where(lane == i1, w1, 0.0) + jnp.where(lane == i2, w2, 0.0)
                      + jnp.where(lane == 0, gidx.astype(F32), 0.0) + jnp.where(lane == 1, rank, 0.0))


def _post_call(x, oa, ob, oc, od, wout, g1, b1, wq, kmem, vmem, wo, g2, b2, wrh, wrl, br, seq):
    t_tokens = x.shape[0]
    tm = PROJ_TM
    per_seq = seq // tm
    row = lambda w: pl.BlockSpec((tm, w), lambda i: (i, 0))
    full = lambda shape: pl.BlockSpec(shape, lambda i: (0,) * len(shape))
    mem_spec = pl.BlockSpec((1, N_MEM, D_MODEL), lambda i: (i // per_seq, 0, 0))
    sq = (D_MODEL, D_MODEL)
    return pl.pallas_call(
        _post_kernel,
        grid=(t_tokens // tm,),
        in_specs=[row(D_MODEL), row(256), row(256), row(256), row(256),
                  full(sq), full((1, D_MODEL)), full((1, D_MODEL)),
                  full(sq), mem_spec, mem_spec, full(sq), full((1, D_MODEL)), full((1, D_MODEL)),
                  full((D_MODEL, ROUTER_LANES)), full((D_MODEL, ROUTER_LANES)), full((1, ROUTER_LANES))],
        out_specs=[row(D_MODEL), row(ROUTER_LANES), full((1, ROUTER_LANES))],
        out_shape=[jax.ShapeDtypeStruct((t_tokens, D_MODEL), F32),
                   jax.ShapeDtypeStruct((t_tokens, ROUTER_LANES), F32),
                   jax.ShapeDtypeStruct((1, ROUTER_LANES), F32)],
        scratch_shapes=[pltpu.VMEM((tm, D_MODEL), BF16), pltpu.VMEM((1, ROUTER_LANES), F32)],
        compiler_params=pltpu.CompilerParams(dimension_semantics=("arbitrary",),
                                             vmem_limit_bytes=VMEM_LIMIT_BYTES),
        name="post",
    )(x, oa, ob, oc, od, wout, g1, b1, wq, kmem, vmem, wo, g2, b2, wrh, wrl, br)


def _row_dma_wait(src_ref, dst_ref, sem, rows):
    pltpu.make_async_copy(src_ref.at[pl.ds(0, rows)], dst_ref.at[pl.ds(0, rows)], sem).wait()


def _gather_kernel(pos_ref, x_ref, r_ref, xs_init_ref, rs_init_ref, xs_ref, rs_ref, sem_x, sem_r):
    del xs_init_ref, rs_init_ref

    def body(r, carry):
        p = pos_ref[0, 0, r]
        pltpu.make_async_copy(x_ref.at[pl.ds(r, 1)], xs_ref.at[pl.ds(p, 1)], sem_x).start()
        pltpu.make_async_copy(r_ref.at[pl.ds(r, 1)], rs_ref.at[pl.ds(p, 1)], sem_r).start()
        return carry

    lax.fori_loop(0, MOE_ROWS_PER_STEP, body, 0, unroll=8)
    _row_dma_wait(x_ref, xs_ref, sem_x, MOE_ROWS_PER_STEP)
    _row_dma_wait(r_ref, rs_ref, sem_r, MOE_ROWS_PER_STEP)


def _scatter_kernel(pos_ref, ys_ref, y_ref, sem):
    def body(r, carry):
        pltpu.make_async_copy(ys_ref.at[pl.ds(pos_ref[0, 0, r], 1)], y_ref.at[pl.ds(r, 1)], sem).start()
        return carry

    lax.fori_loop(0, MOE_ROWS_PER_STEP, body, 0, unroll=8)
    _row_dma_wait(ys_ref, y_ref, sem, MOE_ROWS_PER_STEP)


def _pos_spec():
    return pl.BlockSpec((1, 1, MOE_ROWS_PER_STEP), lambda i: (i, 0, 0), memory_space=pltpu.SMEM)


def _token_rows_spec(width):
    return pl.BlockSpec((MOE_ROWS_PER_STEP, width), lambda i: (i, 0))


def _gather_call(pos3, x2, route, cap_rows):
    n_steps = pos3.shape[0]
    any_spec = pl.BlockSpec(memory_space=pl.ANY)
    xs0 = jnp.zeros((cap_rows, x2.shape[1]), x2.dtype)
    rs0 = jnp.zeros((cap_rows, route.shape[1]), route.dtype)
    return pl.pallas_call(
        _gather_kernel,
        grid=(n_steps,),
        in_specs=[_pos_spec(), _token_rows_spec(x2.shape[1]), _token_rows_spec(route.shape[1]),
                  any_spec, any_spec],
        out_specs=[any_spec, any_spec],
        out_shape=[jax.ShapeDtypeStruct(xs0.shape, xs0.dtype), jax.ShapeDtypeStruct(rs0.shape, rs0.dtype)],
        input_output_aliases={3: 0, 4: 1},
        scratch_shapes=[pltpu.SemaphoreType.DMA(()), pltpu.SemaphoreType.DMA(())],
        compiler_params=pltpu.CompilerParams(dimension_semantics=("arbitrary",), has_side_effects=True),
        name="moe_gather",
    )(pos3, x2, route, xs0, rs0)


def _scatter_call(pos3, ys, t_tokens):
    n_steps = pos3.shape[0]
    any_spec = pl.BlockSpec(memory_space=pl.ANY)
    return pl.pallas_call(
        _scatter_kernel,
        grid=(n_steps,),
        in_specs=[_pos_spec(), any_spec],
        out_specs=_token_rows_spec(ys.shape[1]),
        out_shape=jax.ShapeDtypeStruct((t_tokens, ys.shape[1]), ys.dtype),
        scratch_shapes=[pltpu.SemaphoreType.DMA(())],
        compiler_params=pltpu.CompilerParams(dimension_semantics=("arbitrary",)),
        name="moe_scatter",
    )(pos3, ys)


def _moe_kernel(tile_grp_ref, n_used_ref, xs_ref, route_ref, wg_ref, wu_ref, wd_ref, g_ref, b_ref, o_ref):
    i = pl.program_id(0)

    @pl.when(i < n_used_ref[0])
    def _():
        grp = tile_grp_ref[i]
        x2 = xs_ref[...]
        xb = x2.astype(BF16)
        route = route_ref[...]
        lane = _lane_iota(route.shape)
        acc = jnp.zeros(x2.shape, F32)
        for j in range(EXPERTS_PER_GROUP):
            e = N_GROUPS + EXPERTS_PER_GROUP * grp + j
            gate_col = jnp.sum(jnp.where(lane == e, route, 0.0), axis=-1, keepdims=True)
            gg = _dot(xb, wg_ref[j])
            uu = _dot(xb, wu_ref[j])
            hid = gg * (1.0 / (1.0 + jnp.exp(-gg))) * uu * gate_col
            acc = acc + _dot(hid.astype(BF16), wd_ref[j])
        o_ref[...] = _layer_norm(DEEPNORM_ALPHA * x2 + acc, g_ref[...], b_ref[...])

    @pl.when(i >= n_used_ref[0])
    def _():
        o_ref[...] = jnp.zeros_like(o_ref)


def _moe_call(tile_grp, n_used, xs, routes, wg, wu, wd, g, b):
    cap_rows = xs.shape[0]
    tm = MOE_TM
    row = lambda w: pl.BlockSpec((tm, w), lambda i, tg, nu: (i, 0))
    vec = pl.BlockSpec((1, D_MODEL), lambda i, tg, nu: (0, 0))
    wspec = lambda shape: pl.BlockSpec(shape, lambda i, tg, nu: (tg[i], 0, 0))
    return pl.pallas_call(
        _moe_kernel,
        grid_spec=pltpu.PrefetchScalarGridSpec(
            num_scalar_prefetch=2,
            grid=(cap_rows // tm,),
            in_specs=[row(D_MODEL), row(ROUTER_LANES),
                      wspec((EXPERTS_PER_GROUP, D_MODEL, EXPERT_FF)),
                      wspec((EXPERTS_PER_GROUP, D_MODEL, EXPERT_FF)),
                      wspec((EXPERTS_PER_GROUP, EXPERT_FF, D_MODEL)),
                      vec, vec],
            out_specs=row(D_MODEL)),
        out_shape=jax.ShapeDtypeStruct((cap_rows, D_MODEL), F32),
        compiler_params=pltpu.CompilerParams(dimension_semantics=("arbitrary",),
                                             vmem_limit_bytes=VMEM_LIMIT_BYTES),
        name="moe",
    )(tile_grp, n_used, xs, routes, wg, wu, wd, g, b)


def _moe_layer(x2, route, counts, wg, wu, wd, g, b):
    t_tokens = x2.shape[0]
    tm = MOE_TM
    n_tiles = t_tokens // tm + N_GROUPS
    cap_rows = n_tiles * tm
    gid = route[:, 0].astype(jnp.int32)
    rank = route[:, 1].astype(jnp.int32)
    cnt = counts[0, :N_GROUPS].astype(jnp.int32)
    tiles_per_grp = (cnt + tm - 1) // tm
    tile_end = jnp.cumsum(tiles_per_grp)
    pos = ((tile_end - tiles_per_grp) * tm)[gid] + rank
    tile_ids = jnp.arange(n_tiles, dtype=jnp.int32)
    tile_grp = jnp.minimum(jnp.sum((tile_ids[:, None] >= tile_end[None, :]).astype(jnp.int32), axis=1),
                           N_GROUPS - 1)
    n_used = tile_end[-1:].astype(jnp.int32)
    pos3 = pos.reshape(t_tokens // MOE_ROWS_PER_STEP, 1, MOE_ROWS_PER_STEP)

    xs, rs = _gather_call(pos3, x2, route, cap_rows)
    ys = _moe_call(tile_grp, n_used, xs, rs, wg, wu, wd, g, b)
    return _scatter_call(pos3, ys, t_tokens)


def _rope_lane_tables(positions, rot, period, offset):
    half = rot // 2
    inv = ROPE_THETA ** (-jnp.arange(0, rot, 2, dtype=F32) / rot)
    ang = positions.astype(F32).reshape(-1)[:, None] * inv
    cos, sin = jnp.cos(ang), jnp.sin(ang)
    p = jnp.arange(LANES) % period - offset
    first = (p >= 0) & (p < half)
    second = (p >= half) & (p < rot)
    idx = jnp.clip(jnp.where(second, p - half, p), 0, half - 1)
    cg, sg = cos[:, idx], sin[:, idx]
    c = jnp.where(first | second, cg, 1.0)
    sa = jnp.where(first, -sg, 0.0)
    sb = jnp.where(second, sg, 0.0)
    return c, sa, sb


def _prep_proj_weights(w_in, w_uq, w_ukv, q_norm, kv_norm):
    q_lat, kv_lat, k_rope = w_in[:, 0:192], w_in[:, 192:320], w_in[:, 320:352]
    sb, mb, df = w_in[:, 352:1120], w_in[:, 1120:1888], w_in[:, 1888:2656]
    zeros = lambda n: jnp.zeros((D_MODEL, n), w_in.dtype)

    def head_tiles(w):
        w = w.reshape(w.shape[0], HEADS, HEAD_DIM)
        return jnp.pad(w, ((0, 0), (0, 0), (0, LANES - HEAD_DIM))).reshape(w.shape[0], HEADS * LANES)

    w1 = jnp.concatenate([head_tiles(mb[:, 0:256]), head_tiles(mb[:, 256:512]), head_tiles(mb[:, 512:768]),
                          head_tiles(df[:, 512:768]),
                          q_lat, zeros(64), kv_lat, zeros(64), k_rope, zeros(32),
                          sb, df[:, 0:512]], axis=1)
    wuq = w_uq.reshape(MLA_Q_LORA, HEADS, MLA_NOPE + MLA_ROPE)
    wuq = jnp.pad(wuq, ((0, 256 - MLA_Q_LORA), (0, 0), (0, LANES - MLA_NOPE - MLA_ROPE))).reshape(256, HEADS * LANES)
    wukv = w_ukv.reshape(MLA_KV_LORA, HEADS, 2, HEAD_DIM)
    wk = head_tiles(wukv[:, :, 0].reshape(MLA_KV_LORA, HEADS * HEAD_DIM))
    wv = head_tiles(wukv[:, :, 1].reshape(MLA_KV_LORA, HEADS * HEAD_DIM))
    gq = jnp.pad(q_norm, (0, 256 - MLA_Q_LORA)).reshape(1, 256)
    return (w1.astype(BF16), wuq.astype(BF16), jnp.concatenate([wk, wv], axis=1).astype(BF16),
            gq, kv_norm.reshape(1, MLA_KV_LORA))


def kernel(x, mem, positions, w_in, mla_q_norm, w_uq, mla_kv_norm, w_ukv, diff_lambda, diff_subln,
           w_out, ln_mix_g, ln_mix_b, xattn_wq, xattn_wk, xattn_wv, xattn_wo, ln_mem_g, ln_mem_b,
           router_group_w, router_group_b, router_expert_w, router_expert_b,
           expert_w_gate, expert_w_up, expert_w_down, ln_ffn_g, ln_ffn_b):
    bsz, seq, _ = x.shape
    t_tokens = bsz * seq
    n_blk = seq // MOBA_BLOCK
    assert seq % PROJ_TM == 0 and t_tokens % MOE_TM == 0 and t_tokens % MOE_ROWS_PER_STEP == 0
    assert mem.shape[1] == N_MEM and n_blk <= LANES - HEAD_DIM

    tables = (_rope_lane_tables(positions, MOBA_ROT, LANES, 0)
              + _rope_lane_tables(positions, DIFF_ROT, DIFF_HALF, 0)
              + _rope_lane_tables(positions, MLA_ROPE, LANES, MLA_NOPE))
    mem2 = mem.reshape(bsz * N_MEM, D_MODEL)
    row = lambda v: v.reshape(1, -1)
    xf = x.reshape(t_tokens, D_MODEL)

    for l in range(DEPTH):
        lambda_init = 0.8 - 0.6 * math.exp(-0.3 * l)
        w1, wuq, wukv, gq, gkv = _prep_proj_weights(w_in[l], w_uq[l], w_ukv[l], mla_q_norm[l], mla_kv_norm[l])
        proj, kmean = _proj_call(xf, w1, wuq, wukv, gq, gkv, tables, n_blk)
        proj3 = proj.reshape(bsz, seq, PROJ_OUT_COLS)
        kmean3 = kmean.reshape(bsz, n_blk, HEADS * LANES)

        o_mla = _attn_call(_mla_kernel, "mla_attn", proj3, O_MLQ, O_MLK, O_MLV, 512, 512, 512,
                           [_state(HEADS), _state(HEADS)] + _staging(HEADS, ATT_TQ))
        o_sb = _attn_call(_sb_kernel, "sb_attn", proj3, O_SBQ, O_SBK, O_SBV, 256, 256, 256,
                          [pltpu.VMEM((2, 2 * ATT_TQ, LANES), F32), pltpu.VMEM((2, 2 * ATT_TQ, LANES), F32)]
                          + _staging(2, 2 * ATT_TQ) + [pltpu.VMEM((2, 2 * ATT_TQ, LANES), BF16)])
        o_mb = _attn_call(_moba_kernel, "moba_attn", proj3, O_MBQ, O_MBK, O_MBV, 512, 512, 512,
                          [_state(HEADS), _state(HEADS)] + _staging(HEADS, ATT_TQ)
                          + [pltpu.VMEM((HEADS, ATT_TQ, LANES), BF16), pltpu.VMEM((LANES, HEADS * LANES), F32)],
                          extra_in=(kmean3,),
                          extra_specs=(pl.BlockSpec((1, n_blk, HEADS * LANES), lambda b, i: (b, 0, 0)),))
        subln = jnp.tile(diff_subln[l], LANES // HEAD_DIM).reshape(1, LANES)
        o_df = _attn_call(functools.partial(_diff_kernel, lambda_init), "diff_attn", proj3,
                          O_DFQ, O_DFK, O_DFV, 256, 256, 512,
                          [_state(2 * HEADS), _state(2 * HEADS)] + _staging(2 * HEADS, ATT_TQ)
                          + [pltpu.VMEM((2, 4 * ATT_TQ, LANES), BF16)],
                          extra_in=(diff_lambda[l], subln),
                          extra_specs=(pl.BlockSpec((4, DIFF_HALF), lambda b, i: (0, 0)),
                                       pl.BlockSpec((1, LANES), lambda b, i: (0, 0))))

        kmem, vmem = _memkv_call(mem2, xattn_wk[l].astype(BF16), xattn_wv[l].astype(BF16))
        wr = jnp.concatenate([router_group_w[l], router_expert_w[l],
                              jnp.zeros((D_MODEL, ROUTER_LANES - N_GROUPS - N_EXPERTS), F32)], axis=1)
        wr_hi = wr.astype(BF16)
        wr_lo = (wr - wr_hi.astype(F32)).astype(BF16)
        br = jnp.concatenate([router_group_b[l], router_expert_b[l],
                              jnp.zeros((ROUTER_LANES - N_GROUPS - N_EXPERTS,), F32)]).reshape(1, ROUTER_LANES)
        flat = lambda o: o.reshape(t_tokens, GROUP_WIDTH)
        x2, route, counts = _post_call(
            xf, flat(o_mla), flat(o_sb), flat(o_mb), flat(o_df), w_out[l].astype(BF16),
            row(ln_mix_g[l]), row(ln_mix_b[l]), xattn_wq[l].astype(BF16),
            kmem.reshape(bsz, N_MEM, D_MODEL), vmem.reshape(bsz, N_MEM, D_MODEL), xattn_wo[l].astype(BF16),
            row(ln_mem_g[l]), row(ln_mem_b[l]), wr_hi, wr_lo, br, seq)

        xf = _moe_layer(x2, route, counts, expert_w_gate[l].astype(BF16), expert_w_up[l].astype(BF16),
                        expert_w_down[l].astype(BF16), row(ln_ffn_g[l]), row(ln_ffn_b[l]))
    return xf.reshape(bsz, seq, D_MODEL)
```

```python
import functools
import math

import jax
import jax.numpy as jnp
from jax import lax
from jax.experimental import pallas as pl
from jax.experimental.pallas import tpu as pltpu

F32 = jnp.float32
BF16 = jnp.bfloat16
NEG_INF = float("-inf")

D_MODEL = 1024
DEPTH = 4
N_MEM = 256
HEAD_DIM = 64
GROUP_WIDTH = 256
HEADS = 4
ROPE_THETA = 500000.0
MLA_Q_LORA = 192
MLA_KV_LORA = 128
MLA_NOPE = 64
MLA_ROPE = 32
MOBA_BLOCK = 256
MOBA_TOPK = 3
MOBA_ROT = 16
DIFF_HALF = 32
DIFF_ROT = 8
XATTN_HEADS = 4
XATTN_HEAD_DIM = 256
N_GROUPS = 8
EXPERTS_PER_GROUP = 4
N_EXPERTS = 32
EXPERT_FF = 256
DEEPNORM_ALPHA = (2 * DEPTH) ** 0.25
LN_EPS = 1e-5
RMS_EPS = 1e-6

LANES = 128
VMEM_LIMIT_BYTES = 56 * 1024 * 1024
PROJ_TM = 512
SB_TILE = 256
FLASH_TILE = 512
MOE_TM = 512
MOE_ROWS_PER_STEP = 512

C_MBQ, C_MBK, C_MBV, C_DFV, C_LAT, C_SB, C_DFQ, C_DFK = 0, 512, 1024, 1536, 2048, 2560, 3328, 3584
PROJ_IN_COLS = 3840
O_MBQ, O_MBK, O_MBV, O_DFV, O_MLQ, O_MLK, O_MLV = 0, 512, 1024, 1536, 2048, 2560, 3072
O_SBQ, O_SBK, O_SBV, O_DFQ, O_DFK = 3584, 3840, 4096, 4352, 4608
PROJ_OUT_COLS = 4864
ROUTER_LANES = 128
LOG2E = 1.4426950408889634
MOBA_MASK_BIAS = 2.0 ** 100


def _dot(a, b):
    return jnp.dot(a, b, preferred_element_type=F32)


def _dot_nt(a, b):
    return lax.dot_general(a, b, (((1,), (1,)), ((), ())), preferred_element_type=F32)


def _split_bf16(x):
    hi = x.astype(BF16)
    lo = (x - hi.astype(F32)).astype(BF16)
    return hi, lo


def _layer_norm(x, g, b):
    mu = jnp.mean(x, axis=-1, keepdims=True)
    xc = x - mu
    var = jnp.mean(xc * xc, axis=-1, keepdims=True)
    return xc * lax.rsqrt(var + LN_EPS) * g + b


def _lane_iota(shape):
    return lax.broadcasted_iota(jnp.int32, shape, 1)


def _rope128(t, c, sa, sb, half):
    nxt = pltpu.roll(t, LANES - half, axis=1)
    prv = pltpu.roll(t, half, axis=1)
    return t * c + nxt * sa + prv * sb


def _proj_kernel(n_blk, x_ref, w1_ref, wuq_ref, wukv_ref, gq_ref, gkv_ref,
                 mbc_ref, mbsa_ref, mbsb_ref, dfc_ref, dfsa_ref, dfsb_ref,
                 mlc_ref, mlsa_ref, mlsb_ref, out_ref, kmean_ref):
    xb = x_ref[...].astype(BF16)
    tm = xb.shape[0]
    lane = _lane_iota((tm, LANES))
    upper = lane >= HEAD_DIM

    def mm(c0, width):
        return _dot(xb, w1_ref[:, c0:c0 + width])

    def put(c0, val):
        out_ref[:, c0:c0 + val.shape[1]] = val.astype(BF16)

    def with_ones(v):
        return jnp.where(upper, 1.0, v)

    mbq, mbk, mbv = mm(C_MBQ, 512), mm(C_MBK, 512), mm(C_MBV, 512)
    mbc, mbsa, mbsb = mbc_ref[...], mbsa_ref[...], mbsb_ref[...]
    per_step = tm // MOBA_BLOCK
    base_blk = lax.rem(pl.program_id(0) * per_step, n_blk)
    row_blk = jnp.right_shift(lax.broadcasted_iota(jnp.int32, (tm, LANES), 0), int(math.log2(MOBA_BLOCK)))
    onehot = lane == HEAD_DIM + base_blk + row_blk
    for h in range(HEADS):
        c = slice(LANES * h, LANES * (h + 1))
        put(O_MBQ + LANES * h, _rope128(mbq[:, c], mbc, mbsa, mbsb, MOBA_ROT // 2) * (HEAD_DIM ** -0.5 * LOG2E))
        k = _rope128(mbk[:, c], mbc, mbsa, mbsb, MOBA_ROT // 2)
        for r in range(per_step):
            kmean_ref[0, r:r + 1, c] = jnp.mean(k[r * MOBA_BLOCK:(r + 1) * MOBA_BLOCK], axis=0, keepdims=True)
        put(O_MBK + LANES * h, jnp.where(onehot, 1.0, k))
        put(O_MBV + LANES * h, with_ones(mbv[:, c]))

    dfq, dfk, dfv = mm(C_DFQ, 256), mm(C_DFK, 256), mm(C_DFV, 512)
    dfc, dfsa, dfsb = dfc_ref[...], dfsa_ref[...], dfsb_ref[...]
    for t in range(2):
        c = slice(LANES * t, LANES * (t + 1))
        put(O_DFQ + LANES * t, _rope128(dfq[:, c], dfc, dfsa, dfsb, DIFF_ROT // 2) * (DIFF_HALF ** -0.5 * LOG2E))
        put(O_DFK + LANES * t, _rope128(dfk[:, c], dfc, dfsa, dfsb, DIFF_ROT // 2))
    for h in range(HEADS):
        put(O_DFV + LANES * h, with_ones(dfv[:, LANES * h:LANES * (h + 1)]))

    sb = mm(C_SB, 768)
    put(O_SBQ, sb[:, 0:256] * 0.125)
    put(O_SBK, sb[:, 256:768])

    lat = mm(C_LAT, 512)
    mlc, mlsa, mlsb = mlc_ref[...], mlsa_ref[...], mlsb_ref[...]
    ql = lat[:, 0:256]
    qn = ql * lax.rsqrt(jnp.sum(ql * ql, axis=-1, keepdims=True) * (1.0 / MLA_Q_LORA) + RMS_EPS) * gq_ref[...]
    qm = _dot(qn.astype(BF16), wuq_ref[...])
    kvl = lat[:, 256:384]
    kvn = kvl * lax.rsqrt(jnp.mean(kvl * kvl, axis=-1, keepdims=True) + RMS_EPS) * gkv_ref[...]
    kv = _dot(kvn.astype(BF16), wukv_ref[...])
    kpe = _rope128(lat[:, 384:512], mlc, mlsa, mlsb, MLA_ROPE // 2)
    mscale = (MLA_NOPE + MLA_ROPE) ** -0.5 * LOG2E
    for h in range(HEADS):
        c = slice(LANES * h, LANES * (h + 1))
        put(O_MLQ + LANES * h, _rope128(qm[:, c], mlc, mlsa, mlsb, MLA_ROPE // 2) * mscale)
        put(O_MLK + LANES * h, kv[:, c] + kpe)
        put(O_MLV + LANES * h, with_ones(kv[:, 512 + LANES * h:512 + LANES * (h + 1)]))


def _proj_call(x, w1, wuq, wukv, gq, gkv, tables, n_blk):
    t_tokens = x.shape[0]
    n_steps = t_tokens // PROJ_TM
    full = lambda shape: pl.BlockSpec(shape, lambda i: (0,) * len(shape))
    tab_spec = pl.BlockSpec((PROJ_TM, LANES), lambda i: (i, 0))
    return pl.pallas_call(
        functools.partial(_proj_kernel, n_blk),
        grid=(n_steps,),
        in_specs=[pl.BlockSpec((PROJ_TM, D_MODEL), lambda i: (i, 0)),
                  full((D_MODEL, PROJ_IN_COLS)), full((256, 512)), full((128, 1024)),
                  full((1, 256)), full((1, 128))] + [tab_spec] * 9,
        out_specs=[pl.BlockSpec((PROJ_TM, PROJ_OUT_COLS), lambda i: (i, 0)),
                   pl.BlockSpec((1, PROJ_TM // MOBA_BLOCK, 512), lambda i: (i, 0, 0))],
        out_shape=[jax.ShapeDtypeStruct((t_tokens, PROJ_OUT_COLS), BF16),
                   jax.ShapeDtypeStruct((n_steps, PROJ_TM // MOBA_BLOCK, 512), F32)],
        compiler_params=pltpu.CompilerParams(dimension_semantics=("parallel",),
                                             vmem_limit_bytes=VMEM_LIMIT_BYTES),
        name="proj",
    )(x, w1, wuq, wukv, gq, gkv, *tables)


def _causal_mask(tq, tk):
    return lax.broadcasted_iota(jnp.int32, (tq, tk), 1) <= lax.broadcasted_iota(jnp.int32, (tq, tk), 0)


def _kv_rows(j, tile):
    return pl.ds(pl.multiple_of(j * tile, tile), tile)


def _flash_update(s, v_tile, m_ref, acc_ref, idx):
    m_prev = m_ref[idx]
    m_new = jnp.maximum(m_prev, jnp.max(s, axis=-1, keepdims=True))
    p = jnp.concatenate([jnp.exp2(s[:, LANES * t:LANES * (t + 1)] - m_new) for t in range(s.shape[1] // LANES)],
                        axis=1)
    acc_ref[idx] = jnp.exp2(m_prev - m_new) * acc_ref[idx] + _dot(p.astype(BF16), v_tile)
    m_ref[idx] = m_new


def _flash_init(m_ref, acc_ref):
    m_ref[...] = jnp.full(m_ref.shape, NEG_INF, F32)
    acc_ref[...] = jnp.zeros(acc_ref.shape, F32)


def _normalized(acc):
    return acc / pltpu.roll(acc, HEAD_DIM, axis=1)


def _store_pairs(o_ref, outs):
    lane = _lane_iota(outs[0].shape)
    for t in range(2):
        pair = jnp.where(lane < HEAD_DIM, outs[2 * t], pltpu.roll(outs[2 * t + 1], HEAD_DIM, axis=1))
        o_ref[0, :, LANES * t:LANES * (t + 1)] = pair.astype(o_ref.dtype)


def _pipelined_sweep(qi, block_of, scores, update, buf_a, buf_b):
    scores(block_of(0), True, buf_a)

    def body(p, carry):
        t = 2 * p
        scores(block_of(t + 1), False, buf_b)
        update(buf_a, block_of(t))
        scores(block_of(t + 2), False, buf_a)
        update(buf_b, block_of(t + 1))
        return carry

    lax.fori_loop(0, jnp.right_shift(qi, 1), body, 0)
    odd = lax.rem(qi, 2)

    @pl.when(odd == 1)
    def _():
        scores(block_of(qi), False, buf_b)
        update(buf_a, block_of(qi - 1))
        update(buf_b, block_of(qi))

    @pl.when(odd == 0)
    def _():
        update(buf_a, block_of(qi))


def _diag_then_past(qi):
    return lambda step: jnp.where(step == 0, qi, step - 1)


def _mla_kernel(q_ref, k_ref, v_ref, o_ref, m_ref, acc_ref, sa_ref, sb_ref):
    qi = pl.program_id(1)
    diag = _causal_mask(FLASH_TILE, FLASH_TILE)
    _flash_init(m_ref, acc_ref)

    def scores(j, diagonal, buf):
        for h in range(HEADS):
            c = slice(LANES * h, LANES * (h + 1))
            s = _dot_nt(q_ref[0, :, c], k_ref[0, _kv_rows(j, FLASH_TILE), c])
            buf[h] = jnp.where(diag, s, NEG_INF) if diagonal else s

    def update(buf, j):
        for h in range(HEADS):
            _flash_update(buf[h], v_ref[0, _kv_rows(j, FLASH_TILE), LANES * h:LANES * (h + 1)], m_ref, acc_ref, h)

    _pipelined_sweep(qi, _diag_then_past(qi), scores, update, sa_ref, sb_ref)
    _store_pairs(o_ref, [_normalized(acc_ref[h]) for h in range(HEADS)])


def _moba_kernel(q_ref, k_ref, v_ref, kmean_ref, o_ref, m_ref, acc_ref, sa_ref, sb_ref, qx_ref, km_ref):
    qi = pl.program_id(1)
    tq = FLASH_TILE
    n_blk = kmean_ref.shape[1]
    diag = _causal_mask(tq, FLASH_TILE)
    _flash_init(m_ref, acc_ref)
    km_ref[...] = jnp.zeros_like(km_ref)
    km_ref[HEAD_DIM:HEAD_DIM + n_blk, :] = kmean_ref[0]
    blk = _lane_iota((tq, LANES)) - HEAD_DIM
    slot = (blk >= 0) & (blk < n_blk)
    own = (tq // MOBA_BLOCK) * qi + jnp.right_shift(lax.broadcasted_iota(jnp.int32, (tq, LANES), 0),
                                                    int(math.log2(MOBA_BLOCK)))
    for h in range(HEADS):
        c = slice(LANES * h, LANES * (h + 1))
        qh = q_ref[0, :, c]
        km_hi, km_lo = _split_bf16(km_ref[:, c])
        gate = _dot_nt(qh, km_hi) + _dot_nt(qh, km_lo)
        gate = jnp.where((blk >= 0) & (blk < own), gate, NEG_INF)
        sel = blk == own
        for _ in range(MOBA_TOPK):
            mx = jnp.max(gate, axis=-1, keepdims=True)
            first_idx = jnp.min(jnp.where(gate == mx, blk, LANES), axis=-1, keepdims=True)
            pick = (blk == first_idx) & (mx > NEG_INF)
            sel = sel | pick
            gate = jnp.where(pick, NEG_INF, gate)
        bias = jnp.where(sel, 0.0, -MOBA_MASK_BIAS).astype(BF16)
        qx_ref[h] = jnp.where(slot, bias, qh)

    def scores(j, diagonal, buf):
        for h in range(HEADS):
            s = _dot_nt(qx_ref[h], k_ref[0, _kv_rows(j, FLASH_TILE), LANES * h:LANES * (h + 1)])
            buf[h] = jnp.where(diag, s, NEG_INF) if diagonal else s

    def update(buf, j):
        for h in range(HEADS):
            _flash_update(buf[h], v_ref[0, _kv_rows(j, FLASH_TILE), LANES * h:LANES * (h + 1)], m_ref, acc_ref, h)

    _pipelined_sweep(qi, _diag_then_past(qi), scores, update, sa_ref, sb_ref)
    _store_pairs(o_ref, [_normalized(acc_ref[h]) for h in range(HEADS)])


def _diff_kernel(lambda_init, q_ref, k_ref, v_ref, lam_ref, g_ref, o_ref, m_ref, acc_ref, sa_ref, sb_ref, qs_ref):
    qi = pl.program_id(1)
    tq = FLASH_TILE
    diag = _causal_mask(tq, FLASH_TILE)
    _flash_init(m_ref, acc_ref)
    lane = _lane_iota((tq, LANES))
    zero = jnp.zeros((), BF16)
    for t in range(2):
        qt = q_ref[0, :, LANES * t:LANES * (t + 1)]
        for part in range(4):
            keep = (lane >= DIFF_HALF * part) & (lane < DIFF_HALF * (part + 1))
            qs_ref[t, part * tq:(part + 1) * tq, :] = jnp.where(keep, qt, zero)

    def scores(j, diagonal, buf):
        for t in range(2):
            s_all = _dot_nt(qs_ref[t], k_ref[0, _kv_rows(j, FLASH_TILE), LANES * t:LANES * (t + 1)])
            for part in range(4):
                s = s_all[part * tq:(part + 1) * tq]
                buf[4 * t + part] = jnp.where(diag, s, NEG_INF) if diagonal else s

    def update(buf, j):
        for idx in range(2 * HEADS):
            h = idx // 2
            _flash_update(buf[idx], v_ref[0, _kv_rows(j, FLASH_TILE), LANES * h:LANES * (h + 1)], m_ref, acc_ref, idx)

    _pipelined_sweep(qi, _diag_then_past(qi), scores, update, sa_ref, sb_ref)

    lp = lam_ref[...]
    lam = (jnp.exp(jnp.sum(lp[0:1] * lp[1:2], axis=-1, keepdims=True))
           - jnp.exp(jnp.sum(lp[2:3] * lp[3:4], axis=-1, keepdims=True)) + lambda_init)
    outs = []
    for h in range(HEADS):
        o = _normalized(acc_ref[2 * h]) - lam * _normalized(acc_ref[2 * h + 1])
        ms = jnp.sum(jnp.where(lane < HEAD_DIM, o * o, 0.0), axis=-1, keepdims=True) * (1.0 / HEAD_DIM)
        outs.append(o * lax.rsqrt(ms + RMS_EPS) * g_ref[...] * (1.0 - lambda_init))
    _store_pairs(o_ref, outs)


def _sb_kernel(q_ref, k_ref, v_ref, o_ref, cum_ref, acc_ref, za_ref, zb_ref, qs_ref):
    qi = pl.program_id(1)
    tq, tk = SB_TILE, SB_TILE
    lane = _lane_iota((tq, LANES))
    zero = jnp.zeros((), BF16)
    cum_ref[...] = jnp.zeros(cum_ref.shape, F32)
    acc_ref[...] = jnp.zeros(acc_ref.shape, F32)
    for t in range(2):
        qt = q_ref[0, :, LANES * t:LANES * (t + 1)]
        qs_ref[t, 0:tq, :] = jnp.where(lane < HEAD_DIM, qt, zero)
        qs_ref[t, tq:2 * tq, :] = jnp.where(lane >= HEAD_DIM, qt, zero)
    row = lax.broadcasted_iota(jnp.int32, (tk, tk), 0)
    col = lax.broadcasted_iota(jnp.int32, (tk, tk), 1)
    u_tri = jnp.where(row > col, 1.0, 0.0).astype(BF16)
    u2_tri = jnp.concatenate([u_tri, u_tri], axis=0)
    row2 = lax.broadcasted_iota(jnp.int32, (2 * tq, tk), 0)
    col2 = lax.broadcasted_iota(jnp.int32, (2 * tq, tk), 1)
    past = col2 < jnp.where(row2 >= tq, row2 - tq, row2)

    def scores(j, diagonal, buf):
        for t in range(2):
            z = _dot_nt(qs_ref[t], k_ref[0, _kv_rows(j, tk), LANES * t:LANES * (t + 1)])
            buf[t] = jnp.where(past, z, NEG_INF) if diagonal else z

    def update(buf, j):
        for t in range(2):
            z = buf[t]
            sp = jnp.maximum(z, 0.0) + jnp.log(1.0 + jnp.exp(-jnp.abs(z)))
            hi, lo = _split_bf16(sp)
            later = _dot(jnp.concatenate([hi, lo], axis=1), u2_tri)
            cum = cum_ref[t]
            a = jnp.concatenate(
                [jnp.exp(z[:, LANES * n:LANES * (n + 1)] - sp[:, LANES * n:LANES * (n + 1)]
                         - later[:, LANES * n:LANES * (n + 1)] - cum) for n in range(tk // LANES)], axis=1)
            acc_ref[t] += _dot(a.astype(BF16), v_ref[0, _kv_rows(j, tk), LANES * t:LANES * (t + 1)])
            cum_ref[t] = cum + jnp.sum(sp, axis=-1, keepdims=True)

    _pipelined_sweep(qi, lambda step: qi - step, scores, update, za_ref, zb_ref)
    for t in range(2):
        acc = acc_ref[t]
        o_ref[0, :, LANES * t:LANES * (t + 1)] = jnp.where(lane < HEAD_DIM, acc[0:tq], acc[tq:2 * tq]).astype(o_ref.dtype)


def _attn_call(kernel, name, tile, proj3, q_off, k_off, v_off, q_width, k_width, v_width, scratch,
               extra_in=(), extra_specs=()):
    bsz, seq, _ = proj3.shape
    return pl.pallas_call(
        kernel,
        grid=(bsz, seq // tile),
        in_specs=[pl.BlockSpec((1, tile, q_width), lambda b, i: (b, i, q_off // q_width)),
                  pl.BlockSpec((1, seq, k_width), lambda b, i: (b, 0, k_off // k_width)),
                  pl.BlockSpec((1, seq, v_width), lambda b, i: (b, 0, v_off // v_width))] + list(extra_specs),
        out_specs=pl.BlockSpec((1, tile, GROUP_WIDTH), lambda b, i: (b, i, 0)),
        out_shape=jax.ShapeDtypeStruct((bsz, seq, GROUP_WIDTH), BF16),
        scratch_shapes=scratch,
        compiler_params=pltpu.CompilerParams(dimension_semantics=("parallel", "parallel"),
                                             vmem_limit_bytes=VMEM_LIMIT_BYTES),
        name=name,
    )(proj3, proj3, proj3, *extra_in)


def _state(n, rows):
    return pltpu.VMEM((n, rows, LANES), F32)


def _staging(n, rows, tk):
    return [pltpu.VMEM((n, rows, tk), F32), pltpu.VMEM((n, rows, tk), F32)]


def _memkv_kernel(mem_ref, wk_ref, wv_ref, k_ref, v_ref):
    mb = mem_ref[...].astype(BF16)
    k_ref[...] = _dot(mb, wk_ref[...]).astype(BF16)
    v_ref[...] = _dot(mb, wv_ref[...]).astype(BF16)


def _memkv_call(mem2, wk, wv):
    rows = mem2.shape[0]
    full = pl.BlockSpec((D_MODEL, D_MODEL), lambda i: (0, 0))
    blk = pl.BlockSpec((N_MEM, D_MODEL), lambda i: (i, 0))
    return pl.pallas_call(
        _memkv_kernel,
        grid=(rows // N_MEM,),
        in_specs=[blk, full, full],
        out_specs=[blk, blk],
        out_shape=[jax.ShapeDtypeStruct((rows, D_MODEL), BF16)] * 2,
        compiler_params=pltpu.CompilerParams(dimension_semantics=("parallel",),
                                             vmem_limit_bytes=VMEM_LIMIT_BYTES),
        name="memkv",
    )(mem2, wk, wv)


def _post_kernel(x_ref, oa_ref, ob_ref, oc_ref, od_ref, wout_ref, g1_ref, b1_ref,
                 wq_ref, km_ref, vm_ref, wo_ref, g2_ref, b2_ref, wrh_ref, wrl_ref, br_ref,
                 x2_ref, gates_ref, counts_ref, oh_ref, cnt_ref):
    mix = (_dot(oa_ref[...], wout_ref[0:256, :]) + _dot(ob_ref[...], wout_ref[256:512, :])
           + _dot(oc_ref[...], wout_ref[512:768, :]) + _dot(od_ref[...], wout_ref[768:1024, :]))
    x1 = _layer_norm(DEEPNORM_ALPHA * x_ref[...] + mix, g1_ref[...], b1_ref[...])

    q = (_dot(x1.astype(BF16), wq_ref[...]) * (XATTN_HEAD_DIM ** -0.5)).astype(BF16)
    for h in range(XATTN_HEADS):
        c = slice(XATTN_HEAD_DIM * h, XATTN_HEAD_DIM * (h + 1))
        s = _dot_nt(q[:, c], km_ref[0, :, c])
        p = jnp.exp(s - jnp.max(s, axis=-1, keepdims=True))
        o = _dot(p.astype(BF16), vm_ref[0, :, c]) / jnp.sum(p, axis=-1, keepdims=True)
        oh_ref[:, c] = o.astype(BF16)
    xa = _dot(oh_ref[...], wo_ref[...])
    x2 = _layer_norm(DEEPNORM_ALPHA * x1 + xa, g2_ref[...], b2_ref[...])
    x2_ref[...] = x2

    hi, lo = _split_bf16(x2)
    logits = _dot(hi, wrh_ref[...]) + _dot(hi, wrl_ref[...]) + _dot(lo, wrh_ref[...]) + br_ref[...]
    lane = _lane_iota(logits.shape)
    gl = jnp.where(lane < N_GROUPS, logits, NEG_INF)
    gmx = jnp.max(gl, axis=-1, keepdims=True)
    gidx = jnp.min(jnp.where(gl == gmx, lane, ROUTER_LANES), axis=-1, keepdims=True)
    gw = 1.0 / jnp.sum(jnp.exp(gl - gmx), axis=-1, keepdims=True)
    e0 = N_GROUPS + EXPERTS_PER_GROUP * gidx
    el = jnp.where((lane >= e0) & (lane < e0 + EXPERTS_PER_GROUP), logits, NEG_INF)
    m1 = jnp.max(el, axis=-1, keepdims=True)
    i1 = jnp.min(jnp.where(el == m1, lane, ROUTER_LANES), axis=-1, keepdims=True)
    el = jnp.where(lane == i1, NEG_INF, el)
    m2 = jnp.max(el, axis=-1, keepdims=True)
    i2 = jnp.min(jnp.where(el == m2, lane, ROUTER_LANES), axis=-1, keepdims=True)
    e = jnp.exp(m2 - m1)
    w1 = gw / (1.0 + e)
    w2 = gw * e / (1.0 + e)

    @pl.when(pl.program_id(0) == 0)
    def _():
        cnt_ref[...] = jnp.zeros_like(cnt_ref)

    tm = logits.shape[0]
    onehot = jnp.where(lane == gidx, 1.0, 0.0)
    earlier = (lax.broadcasted_iota(jnp.int32, (tm, tm), 1) < lax.broadcasted_iota(jnp.int32, (tm, tm), 0))
    before = _dot(jnp.where(earlier, 1.0, 0.0).astype(BF16), onehot.astype(BF16)) + cnt_ref[...]
    rank = jnp.sum(onehot * before, axis=-1, keepdims=True)
    cnt_ref[...] += jnp.sum(onehot, axis=0, keepdims=True)
    counts_ref[...] = cnt_ref[...]
    gates_ref[...] = (jnp.where(lane == i1, w1, 0.0) + jnp.where(lane == i2, w2, 0.0)
                      + jnp.where(lane == 0, gidx.astype(F32), 0.0) + jnp.where(lane == 1, rank, 0.0))


def _post_call(x, oa, ob, oc, od, wout, g1, b1, wq, kmem, vmem, wo, g2, b2, wrh, wrl, br, seq):
    t_tokens = x.shape[0]
    tm = PROJ_TM
    per_seq = seq // tm
    row = lambda w: pl.BlockSpec((tm, w), lambda i: (i, 0))
    full = lambda shape: pl.BlockSpec(shape, lambda i: (0,) * len(shape))
    mem_spec = pl.BlockSpec((1, N_MEM, D_MODEL), lambda i: (i // per_seq, 0, 0))
    sq = (D_MODEL, D_MODEL)
    return pl.pallas_call(
        _post_kernel,
        grid=(t_tokens // tm,),
        in_specs=[row(D_MODEL), row(256), row(256), row(256), row(256),
                  full(sq), full((1, D_MODEL)), full((1, D_MODEL)),
                  full(sq), mem_spec, mem_spec, full(sq), full((1, D_MODEL)), full((1, D_MODEL)),
                  full((D_MODEL, ROUTER_LANES)), full((D_MODEL, ROUTER_LANES)), full((1, ROUTER_LANES))],
        out_specs=[row(D_MODEL), row(ROUTER_LANES), full((1, ROUTER_LANES))],
        out_shape=[jax.ShapeDtypeStruct((t_tokens, D_MODEL), F32),
                   jax.ShapeDtypeStruct((t_tokens, ROUTER_LANES), F32),
                   jax.ShapeDtypeStruct((1, ROUTER_LANES), F32)],
        scratch_shapes=[pltpu.VMEM((tm, D_MODEL), BF16), pltpu.VMEM((1, ROUTER_LANES), F32)],
        compiler_params=pltpu.CompilerParams(dimension_semantics=("arbitrary",),
                                             vmem_limit_bytes=VMEM_LIMIT_BYTES),
        name="post",
    )(x, oa, ob, oc, od, wout, g1, b1, wq, kmem, vmem, wo, g2, b2, wrh, wrl, br)


def _row_dma_wait(src_ref, dst_ref, sem, rows):
    pltpu.make_async_copy(src_ref.at[pl.ds(0, rows)], dst_ref.at[pl.ds(0, rows)], sem).wait()


def _gather_kernel(pos_ref, x_ref, r_ref, xs_init_ref, rs_init_ref, xs_ref, rs_ref, sem_x, sem_r):
    del xs_init_ref, rs_init_ref

    def body(r, carry):
        p = pos_ref[0, 0, r]
        pltpu.make_async_copy(x_ref.at[pl.ds(r, 1)], xs_ref.at[pl.ds(p, 1)], sem_x).start()
        pltpu.make_async_copy(r_ref.at[pl.ds(r, 1)], rs_ref.at[pl.ds(p, 1)], sem_r).start()
        return carry

    lax.fori_loop(0, MOE_ROWS_PER_STEP, body, 0, unroll=8)
    _row_dma_wait(x_ref, xs_ref, sem_x, MOE_ROWS_PER_STEP)
    _row_dma_wait(r_ref, rs_ref, sem_r, MOE_ROWS_PER_STEP)


def _scatter_kernel(pos_ref, ys_ref, y_ref, sem):
    def body(r, carry):
        pltpu.make_async_copy(ys_ref.at[pl.ds(pos_ref[0, 0, r], 1)], y_ref.at[pl.ds(r, 1)], sem).start()
        return carry

    lax.fori_loop(0, MOE_ROWS_PER_STEP, body, 0, unroll=8)
    _row_dma_wait(ys_ref, y_ref, sem, MOE_ROWS_PER_STEP)


def _pos_spec():
    return pl.BlockSpec((1, 1, MOE_ROWS_PER_STEP), lambda i: (i, 0, 0), memory_space=pltpu.SMEM)


def _token_rows_spec(width):
    return pl.BlockSpec((MOE_ROWS_PER_STEP, width), lambda i: (i, 0))


def _gather_call(pos3, x2, route, cap_rows):
    n_steps = pos3.shape[0]
    any_spec = pl.BlockSpec(memory_space=pl.ANY)
    xs0 = jnp.zeros((cap_rows, x2.shape[1]), x2.dtype)
    rs0 = jnp.zeros((cap_rows, route.shape[1]), route.dtype)
    return pl.pallas_call(
        _gather_kernel,
        grid=(n_steps,),
        in_specs=[_pos_spec(), _token_rows_spec(x2.shape[1]), _token_rows_spec(route.shape[1]),
                  any_spec, any_spec],
        out_specs=[any_spec, any_spec],
        out_shape=[jax.ShapeDtypeStruct(xs0.shape, xs0.dtype), jax.ShapeDtypeStruct(rs0.shape, rs0.dtype)],
        input_output_aliases={3: 0, 4: 1},
        scratch_shapes=[pltpu.SemaphoreType.DMA(()), pltpu.SemaphoreType.DMA(())],
        compiler_params=pltpu.CompilerParams(dimension_semantics=("arbitrary",), has_side_effects=True),
        name="moe_gather",
    )(pos3, x2, route, xs0, rs0)


def _scatter_call(pos3, ys, t_tokens):
    n_steps = pos3.shape[0]
    any_spec = pl.BlockSpec(memory_space=pl.ANY)
    return pl.pallas_call(
        _scatter_kernel,
        grid=(n_steps,),
        in_specs=[_pos_spec(), any_spec],
        out_specs=_token_rows_spec(ys.shape[1]),
        out_shape=jax.ShapeDtypeStruct((t_tokens, ys.shape[1]), ys.dtype),
        scratch_shapes=[pltpu.SemaphoreType.DMA(())],
        compiler_params=pltpu.CompilerParams(dimension_semantics=("arbitrary",)),
        name="moe_scatter",
    )(pos3, ys)


def _moe_kernel(tile_grp_ref, n_used_ref, xs_ref, route_ref, wg_ref, wu_ref, wd_ref, g_ref, b_ref, o_ref):
    i = pl.program_id(0)

    @pl.when(i < n_used_ref[0])
    def _():
        grp = tile_grp_ref[i]
        x2 = xs_ref[...]
        xb = x2.astype(BF16)
        route = route_ref[...]
        lane = _lane_iota(route.shape)
        acc = jnp.zeros(x2.shape, F32)
        for j in range(EXPERTS_PER_GROUP):
            e = N_GROUPS + EXPERTS_PER_GROUP * grp + j
            gate_col = jnp.sum(jnp.where(lane == e, route, 0.0), axis=-1, keepdims=True)
            gg = _dot(xb, wg_ref[j])
            uu = _dot(xb, wu_ref[j])
            hid = gg * (1.0 / (1.0 + jnp.exp(-gg))) * uu * gate_col
            acc = acc + _dot(hid.astype(BF16), wd_ref[j])
        o_ref[...] = _layer_norm(DEEPNORM_ALPHA * x2 + acc, g_ref[...], b_ref[...])

    @pl.when(i >= n_used_ref[0])
    def _():
        o_ref[...] = jnp.zeros_like(o_ref)


def _moe_call(tile_grp, n_used, xs, routes, wg, wu, wd, g, b):
    cap_rows = xs.shape[0]
    tm = MOE_TM
    row = lambda w: pl.BlockSpec((tm, w), lambda i, tg, nu: (i, 0))
    vec = pl.BlockSpec((1, D_MODEL), lambda i, tg, nu: (0, 0))
    wspec = lambda shape: pl.BlockSpec(shape, lambda i, tg, nu: (tg[i], 0, 0))
    return pl.pallas_call(
        _moe_kernel,
        grid_spec=pltpu.PrefetchScalarGridSpec(
            num_scalar_prefetch=2,
            grid=(cap_rows // tm,),
            in_specs=[row(D_MODEL), row(ROUTER_LANES),
                      wspec((EXPERTS_PER_GROUP, D_MODEL, EXPERT_FF)),
                      wspec((EXPERTS_PER_GROUP, D_MODEL, EXPERT_FF)),
                      wspec((EXPERTS_PER_GROUP, EXPERT_FF, D_MODEL)),
                      vec, vec],
            out_specs=row(D_MODEL)),
        out_shape=jax.ShapeDtypeStruct((cap_rows, D_MODEL), F32),
        compiler_params=pltpu.CompilerParams(dimension_semantics=("arbitrary",),
                                             vmem_limit_bytes=VMEM_LIMIT_BYTES),
        name="moe",
    )(tile_grp, n_used, xs, routes, wg, wu, wd, g, b)


def _moe_layer(x2, route, counts, wg, wu, wd, g, b):
    t_tokens = x2.shape[0]
    tm = MOE_TM
    n_tiles = t_tokens // tm + N_GROUPS
    cap_rows = n_tiles * tm
    gid = route[:, 0].astype(jnp.int32)
    rank = route[:, 1].astype(jnp.int32)
    cnt = counts[0, :N_GROUPS].astype(jnp.int32)
    tiles_per_grp = (cnt + tm - 1) // tm
    tile_end = jnp.cumsum(tiles_per_grp)
    pos = ((tile_end - tiles_per_grp) * tm)[gid] + rank
    tile_ids = jnp.arange(n_tiles, dtype=jnp.int32)
    tile_grp = jnp.minimum(jnp.sum((tile_ids[:, None] >= tile_end[None, :]).astype(jnp.int32), axis=1),
                           N_GROUPS - 1)
    n_used = tile_end[-1:].astype(jnp.int32)
    pos3 = pos.reshape(t_tokens // MOE_ROWS_PER_STEP, 1, MOE_ROWS_PER_STEP)

    xs, rs = _gather_call(pos3, x2, route, cap_rows)
    ys = _moe_call(tile_grp, n_used, xs, rs, wg, wu, wd, g, b)
    return _scatter_call(pos3, ys, t_tokens)


def _rope_lane_tables(positions, rot, period, offset):
    half = rot // 2
    inv = ROPE_THETA ** (-jnp.arange(0, rot, 2, dtype=F32) / rot)
    ang = positions.astype(F32).reshape(-1)[:, None] * inv
    cos, sin = jnp.cos(ang), jnp.sin(ang)
    p = jnp.arange(LANES) % period - offset
    first = (p >= 0) & (p < half)
    second = (p >= half) & (p < rot)
    idx = jnp.clip(jnp.where(second, p - half, p), 0, half - 1)
    cg, sg = cos[:, idx], sin[:, idx]
    c = jnp.where(first | second, cg, 1.0)
    sa = jnp.where(first, -sg, 0.0)
    sb = jnp.where(second, sg, 0.0)
    return c, sa, sb


def _prep_proj_weights(w_in, w_uq, w_ukv, q_norm, kv_norm):
    q_lat, kv_lat, k_rope = w_in[:, 0:192], w_in[:, 192:320], w_in[:, 320:352]
    sb, mb, df = w_in[:, 352:1120], w_in[:, 1120:1888], w_in[:, 1888:2656]
    zeros = lambda n: jnp.zeros((D_MODEL, n), w_in.dtype)

    def head_tiles(w):
        w = w.reshape(w.shape[0], HEADS, HEAD_DIM)
        return jnp.pad(w, ((0, 0), (0, 0), (0, LANES - HEAD_DIM))).reshape(w.shape[0], HEADS * LANES)

    w1 = jnp.concatenate([head_tiles(mb[:, 0:256]), head_tiles(mb[:, 256:512]), head_tiles(mb[:, 512:768]),
                          head_tiles(df[:, 512:768]),
                          q_lat, zeros(64), kv_lat, zeros(64), k_rope, zeros(32),
                          sb, df[:, 0:512]], axis=1)
    wuq = w_uq.reshape(MLA_Q_LORA, HEADS, MLA_NOPE + MLA_ROPE)
    wuq = jnp.pad(wuq, ((0, 256 - MLA_Q_LORA), (0, 0), (0, LANES - MLA_NOPE - MLA_ROPE))).reshape(256, HEADS * LANES)
    wukv = w_ukv.reshape(MLA_KV_LORA, HEADS, 2, HEAD_DIM)
    wk = head_tiles(wukv[:, :, 0].reshape(MLA_KV_LORA, HEADS * HEAD_DIM))
    wv = head_tiles(wukv[:, :, 1].reshape(MLA_KV_LORA, HEADS * HEAD_DIM))
    gq = jnp.pad(q_norm, (0, 256 - MLA_Q_LORA)).reshape(1, 256)
    return (w1.astype(BF16), wuq.astype(BF16), jnp.concatenate([wk, wv], axis=1).astype(BF16),
            gq, kv_norm.reshape(1, MLA_KV_LORA))


def kernel(x, mem, positions, w_in, mla_q_norm, w_uq, mla_kv_norm, w_ukv, diff_lambda, diff_subln,
           w_out, ln_mix_g, ln_mix_b, xattn_wq, xattn_wk, xattn_wv, xattn_wo, ln_mem_g, ln_mem_b,
           router_group_w, router_group_b, router_expert_w, router_expert_b,
           expert_w_gate, expert_w_up, expert_w_down, ln_ffn_g, ln_ffn_b):
    bsz, seq, _ = x.shape
    t_tokens = bsz * seq
    n_blk = seq // MOBA_BLOCK
    assert seq % PROJ_TM == 0 and seq % FLASH_TILE == 0 and seq % SB_TILE == 0
    assert t_tokens % MOE_TM == 0 and t_tokens % MOE_ROWS_PER_STEP == 0
    assert mem.shape[1] == N_MEM and n_blk <= LANES - HEAD_DIM

    tables = (_rope_lane_tables(positions, MOBA_ROT, LANES, 0)
              + _rope_lane_tables(positions, DIFF_ROT, DIFF_HALF, 0)
              + _rope_lane_tables(positions, MLA_ROPE, LANES, MLA_NOPE))
    mem2 = mem.reshape(bsz * N_MEM, D_MODEL)
    row = lambda v: v.reshape(1, -1)
    xf = x.reshape(t_tokens, D_MODEL)

    for l in range(DEPTH):
        lambda_init = 0.8 - 0.6 * math.exp(-0.3 * l)
        w1, wuq, wukv, gq, gkv = _prep_proj_weights(w_in[l], w_uq[l], w_ukv[l], mla_q_norm[l], mla_kv_norm[l])
        proj, kmean = _proj_call(xf, w1, wuq, wukv, gq, gkv, tables, n_blk)
        proj3 = proj.reshape(bsz, seq, PROJ_OUT_COLS)
        kmean3 = kmean.reshape(bsz, n_blk, HEADS * LANES)

        ft, st = FLASH_TILE, SB_TILE
        o_mla = _attn_call(_mla_kernel, "mla_attn", ft, proj3, O_MLQ, O_MLK, O_MLV, 512, 512, 512,
                           [_state(HEADS, ft), _state(HEADS, ft)] + _staging(HEADS, ft, ft))
        o_sb = _attn_call(_sb_kernel, "sb_attn", st, proj3, O_SBQ, O_SBK, O_SBV, 256, 256, 256,
                          [_state(2, 2 * st), _state(2, 2 * st)]
                          + _staging(2, 2 * st, st) + [pltpu.VMEM((2, 2 * st, LANES), BF16)])
        o_mb = _attn_call(_moba_kernel, "moba_attn", ft, proj3, O_MBQ, O_MBK, O_MBV, 512, 512, 512,
                          [_state(HEADS, ft), _state(HEADS, ft)] + _staging(HEADS, ft, ft)
                          + [pltpu.VMEM((HEADS, ft, LANES), BF16), pltpu.VMEM((LANES, HEADS * LANES), F32)],
                          extra_in=(kmean3,),
                          extra_specs=(pl.BlockSpec((1, n_blk, HEADS * LANES), lambda b, i: (b, 0, 0)),))
        subln = jnp.tile(diff_subln[l], LANES // HEAD_DIM).reshape(1, LANES)
        o_df = _attn_call(functools.partial(_diff_kernel, lambda_init), "diff_attn", ft, proj3,
                          O_DFQ, O_DFK, O_DFV, 256, 256, 512,
                          [_state(2 * HEADS, ft), _state(2 * HEADS, ft)] + _staging(2 * HEADS, ft, ft)
                          + [pltpu.VMEM((2, 4 * ft, LANES), BF16)],
                          extra_in=(diff_lambda[l], subln),
                          extra_specs=(pl.BlockSpec((4, DIFF_HALF), lambda b, i: (0, 0)),
                                       pl.BlockSpec((1, LANES), lambda b, i: (0, 0))))

        kmem, vmem = _memkv_call(mem2, xattn_wk[l].astype(BF16), xattn_wv[l].astype(BF16))
        wr = jnp.concatenate([router_group_w[l], router_expert_w[l],
                              jnp.zeros((D_MODEL, ROUTER_LANES - N_GROUPS - N_EXPERTS), F32)], axis=1)
        wr_hi = wr.astype(BF16)
        wr_lo = (wr - wr_hi.astype(F32)).astype(BF16)
        br = jnp.concatenate([router_group_b[l], router_expert_b[l],
                              jnp.zeros((ROUTER_LANES - N_GROUPS - N_EXPERTS,), F32)]).reshape(1, ROUTER_LANES)
        flat = lambda o: o.reshape(t_tokens, GROUP_WIDTH)
        x2, route, counts = _post_call(
            xf, flat(o_mla), flat(o_sb), flat(o_mb), flat(o_df), w_out[l].astype(BF16),
            row(ln_mix_g[l]), row(ln_mix_b[l]), xattn_wq[l].astype(BF16),
            kmem.reshape(bsz, N_MEM, D_MODEL), vmem.reshape(bsz, N_MEM, D_MODEL), xattn_wo[l].astype(BF16),
            row(ln_mem_g[l]), row(ln_mem_b[l]), wr_hi, wr_lo, br, seq)

        xf = _moe_layer(x2, route, counts, expert_w_gate[l].astype(BF16), expert_w_up[l].astype(BF16),
                        expert_w_down[l].astype(BF16), row(ln_ffn_g[l]), row(ln_ffn_b[l]))
    return xf.reshape(bsz, seq, D_MODEL)
```

```python
import functools
import math

import jax
import jax.numpy as jnp
from jax import lax
from jax.experimental import pallas as pl
from jax.experimental.pallas import tpu as pltpu

F32 = jnp.float32
BF16 = jnp.bfloat16
NEG_INF = float("-inf")

D_MODEL = 1024
DEPTH = 4
N_MEM = 256
HEAD_DIM = 64
GROUP_WIDTH = 256
HEADS = 4
ROPE_THETA = 500000.0
MLA_Q_LORA = 192
MLA_KV_LORA = 128
MLA_NOPE = 64
MLA_ROPE = 32
MOBA_BLOCK = 256
MOBA_TOPK = 3
MOBA_ROT = 16
DIFF_HALF = 32
DIFF_ROT = 8
XATTN_HEADS = 4
XATTN_HEAD_DIM = 256
N_GROUPS = 8
EXPERTS_PER_GROUP = 4
N_EXPERTS = 32
EXPERT_FF = 256
DEEPNORM_ALPHA = (2 * DEPTH) ** 0.25
LN_EPS = 1e-5
RMS_EPS = 1e-6

LANES = 128
VMEM_LIMIT_BYTES = 56 * 1024 * 1024
PROJ_TM = 512
SB_TILE = 256
FLASH_TILE = 512
MOE_TM = 512
MOE_ROWS_PER_STEP = 512

C_MBQ, C_MBK, C_MBV, C_DFV, C_LAT, C_SB, C_DFQ, C_DFK = 0, 512, 1024, 1536, 2048, 2560, 3328, 3584
PROJ_IN_COLS = 3840
O_MBQ, O_MBK, O_MBV, O_DFV, O_MLQ, O_MLK, O_MLV = 0, 512, 1024, 1536, 2048, 2560, 3072
O_SBQ, O_SBK, O_SBV, O_DFQ, O_DFK = 3584, 3840, 4096, 4352, 4608
PROJ_OUT_COLS = 4864
ROUTER_LANES = 128
MOE_ROW_WIDTH = D_MODEL + ROUTER_LANES
LOG2E = 1.4426950408889634
MOBA_MASK_BIAS = 2.0 ** 100


def _dot(a, b):
    return jnp.dot(a, b, preferred_element_type=F32)


def _dot_nt(a, b):
    return lax.dot_general(a, b, (((1,), (1,)), ((), ())), preferred_element_type=F32)


def _split_bf16(x):
    hi = x.astype(BF16)
    lo = (x - hi.astype(F32)).astype(BF16)
    return hi, lo


def _layer_norm(x, g, b):
    mu = jnp.mean(x, axis=-1, keepdims=True)
    xc = x - mu
    var = jnp.mean(xc * xc, axis=-1, keepdims=True)
    return xc * lax.rsqrt(var + LN_EPS) * g + b


def _lane_iota(shape):
    return lax.broadcasted_iota(jnp.int32, shape, 1)


def _rope128(t, c, sa, sb, half):
    nxt = pltpu.roll(t, LANES - half, axis=1)
    prv = pltpu.roll(t, half, axis=1)
    return t * c + nxt * sa + prv * sb


def _proj_kernel(n_blk, x_ref, w1_ref, wuq_ref, wukv_ref, gq_ref, gkv_ref,
                 mbc_ref, mbsa_ref, mbsb_ref, dfc_ref, dfsa_ref, dfsb_ref,
                 mlc_ref, mlsa_ref, mlsb_ref, out_ref, kmean_ref):
    xb = x_ref[...].astype(BF16)
    tm = xb.shape[0]
    lane = _lane_iota((tm, LANES))
    upper = lane >= HEAD_DIM

    def mm(c0, width):
        return _dot(xb, w1_ref[:, c0:c0 + width])

    def put(c0, val):
        out_ref[:, c0:c0 + val.shape[1]] = val.astype(BF16)

    def with_ones(v):
        return jnp.where(upper, 1.0, v)

    mbq, mbk, mbv = mm(C_MBQ, 512), mm(C_MBK, 512), mm(C_MBV, 512)
    mbc, mbsa, mbsb = mbc_ref[...], mbsa_ref[...], mbsb_ref[...]
    per_step = tm // MOBA_BLOCK
    base_blk = lax.rem(pl.program_id(0) * per_step, n_blk)
    row_blk = jnp.right_shift(lax.broadcasted_iota(jnp.int32, (tm, LANES), 0), int(math.log2(MOBA_BLOCK)))
    onehot = lane == HEAD_DIM + base_blk + row_blk
    for h in range(HEADS):
        c = slice(LANES * h, LANES * (h + 1))
        put(O_MBQ + LANES * h, _rope128(mbq[:, c], mbc, mbsa, mbsb, MOBA_ROT // 2) * (HEAD_DIM ** -0.5 * LOG2E))
        k = _rope128(mbk[:, c], mbc, mbsa, mbsb, MOBA_ROT // 2)
        for r in range(per_step):
            kmean_ref[0, r:r + 1, c] = jnp.mean(k[r * MOBA_BLOCK:(r + 1) * MOBA_BLOCK], axis=0, keepdims=True)
        put(O_MBK + LANES * h, jnp.where(onehot, 1.0, k))
        put(O_MBV + LANES * h, with_ones(mbv[:, c]))

    dfq, dfk, dfv = mm(C_DFQ, 256), mm(C_DFK, 256), mm(C_DFV, 512)
    dfc, dfsa, dfsb = dfc_ref[...], dfsa_ref[...], dfsb_ref[...]
    for t in range(2):
        c = slice(LANES * t, LANES * (t + 1))
        put(O_DFQ + LANES * t, _rope128(dfq[:, c], dfc, dfsa, dfsb, DIFF_ROT // 2) * (DIFF_HALF ** -0.5 * LOG2E))
        put(O_DFK + LANES * t, _rope128(dfk[:, c], dfc, dfsa, dfsb, DIFF_ROT // 2))
    for h in range(HEADS):
        put(O_DFV + LANES * h, with_ones(dfv[:, LANES * h:LANES * (h + 1)]))

    sb = mm(C_SB, 768)
    put(O_SBQ, sb[:, 0:256] * 0.125)
    put(O_SBK, sb[:, 256:768])

    lat = mm(C_LAT, 512)
    mlc, mlsa, mlsb = mlc_ref[...], mlsa_ref[...], mlsb_ref[...]
    ql = lat[:, 0:256]
    qn = ql * lax.rsqrt(jnp.sum(ql * ql, axis=-1, keepdims=True) * (1.0 / MLA_Q_LORA) + RMS_EPS) * gq_ref[...]
    qm = _dot(qn.astype(BF16), wuq_ref[...])
    kvl = lat[:, 256:384]
    kvn = kvl * lax.rsqrt(jnp.mean(kvl * kvl, axis=-1, keepdims=True) + RMS_EPS) * gkv_ref[...]
    kv = _dot(kvn.astype(BF16), wukv_ref[...])
    kpe = _rope128(lat[:, 384:512], mlc, mlsa, mlsb, MLA_ROPE // 2)
    mscale = (MLA_NOPE + MLA_ROPE) ** -0.5 * LOG2E
    for h in range(HEADS):
        c = slice(LANES * h, LANES * (h + 1))
        put(O_MLQ + LANES * h, _rope128(qm[:, c], mlc, mlsa, mlsb, MLA_ROPE // 2) * mscale)
        put(O_MLK + LANES * h, kv[:, c] + kpe)
        put(O_MLV + LANES * h, with_ones(kv[:, 512 + LANES * h:512 + LANES * (h + 1)]))


def _proj_call(x, w1, wuq, wukv, gq, gkv, tables, n_blk):
    t_tokens = x.shape[0]
    n_steps = t_tokens // PROJ_TM
    full = lambda shape: pl.BlockSpec(shape, lambda i: (0,) * len(shape))
    tab_spec = pl.BlockSpec((PROJ_TM, LANES), lambda i: (i, 0))
    return pl.pallas_call(
        functools.partial(_proj_kernel, n_blk),
        grid=(n_steps,),
        in_specs=[pl.BlockSpec((PROJ_TM, D_MODEL), lambda i: (i, 0)),
                  full((D_MODEL, PROJ_IN_COLS)), full((256, 512)), full((128, 1024)),
                  full((1, 256)), full((1, 128))] + [tab_spec] * 9,
        out_specs=[pl.BlockSpec((PROJ_TM, PROJ_OUT_COLS), lambda i: (i, 0)),
                   pl.BlockSpec((1, PROJ_TM // MOBA_BLOCK, 512), lambda i: (i, 0, 0))],
        out_shape=[jax.ShapeDtypeStruct((t_tokens, PROJ_OUT_COLS), BF16),
                   jax.ShapeDtypeStruct((n_steps, PROJ_TM // MOBA_BLOCK, 512), F32)],
        compiler_params=pltpu.CompilerParams(dimension_semantics=("parallel",),
                                             vmem_limit_bytes=VMEM_LIMIT_BYTES),
        name="proj",
    )(x, w1, wuq, wukv, gq, gkv, *tables)


def _causal_mask(tq, tk):
    return lax.broadcasted_iota(jnp.int32, (tq, tk), 1) <= lax.broadcasted_iota(jnp.int32, (tq, tk), 0)


def _kv_rows(j, tile):
    return pl.ds(pl.multiple_of(j * tile, tile), tile)


def _flash_update(s, v_tile, m_ref, acc_ref, idx):
    m_prev = m_ref[idx]
    m_new = jnp.maximum(m_prev, jnp.max(s, axis=-1, keepdims=True))
    p = jnp.concatenate([jnp.exp2(s[:, LANES * t:LANES * (t + 1)] - m_new) for t in range(s.shape[1] // LANES)],
                        axis=1)
    acc_ref[idx] = jnp.exp2(m_prev - m_new) * acc_ref[idx] + _dot(p.astype(BF16), v_tile)
    m_ref[idx] = m_new


def _flash_init(m_ref, acc_ref):
    m_ref[...] = jnp.full(m_ref.shape, NEG_INF, F32)
    acc_ref[...] = jnp.zeros(acc_ref.shape, F32)


def _normalized(acc):
    return acc / pltpu.roll(acc, HEAD_DIM, axis=1)


def _store_pairs(o_ref, outs):
    lane = _lane_iota(outs[0].shape)
    for t in range(2):
        pair = jnp.where(lane < HEAD_DIM, outs[2 * t], pltpu.roll(outs[2 * t + 1], HEAD_DIM, axis=1))
        o_ref[0, :, LANES * t:LANES * (t + 1)] = pair.astype(o_ref.dtype)


def _pipelined_sweep(qi, block_of, scores, update, buf_a, buf_b):
    scores(block_of(0), True, buf_a)

    def body(p, carry):
        t = 2 * p
        scores(block_of(t + 1), False, buf_b)
        update(buf_a, block_of(t))
        scores(block_of(t + 2), False, buf_a)
        update(buf_b, block_of(t + 1))
        return carry

    lax.fori_loop(0, jnp.right_shift(qi, 1), body, 0)
    odd = lax.rem(qi, 2)

    @pl.when(odd == 1)
    def _():
        scores(block_of(qi), False, buf_b)
        update(buf_a, block_of(qi - 1))
        update(buf_b, block_of(qi))

    @pl.when(odd == 0)
    def _():
        update(buf_a, block_of(qi))


def _diag_then_past(qi):
    return lambda step: jnp.where(step == 0, qi, step - 1)


def _mla_kernel(q_ref, k_ref, v_ref, o_ref, m_ref, acc_ref, sa_ref, sb_ref):
    qi = pl.program_id(1)
    diag = _causal_mask(FLASH_TILE, FLASH_TILE)
    _flash_init(m_ref, acc_ref)

    def scores(j, diagonal, buf):
        for h in range(HEADS):
            c = slice(LANES * h, LANES * (h + 1))
            s = _dot_nt(q_ref[0, :, c], k_ref[0, _kv_rows(j, FLASH_TILE), c])
            buf[h] = jnp.where(diag, s, NEG_INF) if diagonal else s

    def update(buf, j):
        for h in range(HEADS):
            _flash_update(buf[h], v_ref[0, _kv_rows(j, FLASH_TILE), LANES * h:LANES * (h + 1)], m_ref, acc_ref, h)

    _pipelined_sweep(qi, _diag_then_past(qi), scores, update, sa_ref, sb_ref)
    _store_pairs(o_ref, [_normalized(acc_ref[h]) for h in range(HEADS)])


def _moba_kernel(q_ref, k_ref, v_ref, kmean_ref, o_ref, m_ref, acc_ref, sa_ref, sb_ref, qx_ref, km_ref):
    qi = pl.program_id(1)
    tq = FLASH_TILE
    n_blk = kmean_ref.shape[1]
    diag = _causal_mask(tq, FLASH_TILE)
    _flash_init(m_ref, acc_ref)
    km_ref[...] = jnp.zeros_like(km_ref)
    km_ref[HEAD_DIM:HEAD_DIM + n_blk, :] = kmean_ref[0]
    blk = _lane_iota((tq, LANES)) - HEAD_DIM
    slot = (blk >= 0) & (blk < n_blk)
    own = (tq // MOBA_BLOCK) * qi + jnp.right_shift(lax.broadcasted_iota(jnp.int32, (tq, LANES), 0),
                                                    int(math.log2(MOBA_BLOCK)))
    for h in range(HEADS):
        c = slice(LANES * h, LANES * (h + 1))
        qh = q_ref[0, :, c]
        km_hi, km_lo = _split_bf16(km_ref[:, c])
        gate = _dot_nt(qh, km_hi) + _dot_nt(qh, km_lo)
        gate = jnp.where((blk >= 0) & (blk < own), gate, NEG_INF)
        sel = blk == own
        for _ in range(MOBA_TOPK):
            mx = jnp.max(gate, axis=-1, keepdims=True)
            first_idx = jnp.min(jnp.where(gate == mx, blk, LANES), axis=-1, keepdims=True)
            pick = (blk == first_idx) & (mx > NEG_INF)
            sel = sel | pick
            gate = jnp.where(pick, NEG_INF, gate)
        bias = jnp.where(sel, 0.0, -MOBA_MASK_BIAS).astype(BF16)
        qx_ref[h] = jnp.where(slot, bias, qh)

    def scores(j, diagonal, buf):
        for h in range(HEADS):
            s = _dot_nt(qx_ref[h], k_ref[0, _kv_rows(j, FLASH_TILE), LANES * h:LANES * (h + 1)])
            buf[h] = jnp.where(diag, s, NEG_INF) if diagonal else s

    def update(buf, j):
        for h in range(HEADS):
            _flash_update(buf[h], v_ref[0, _kv_rows(j, FLASH_TILE), LANES * h:LANES * (h + 1)], m_ref, acc_ref, h)

    _pipelined_sweep(qi, _diag_then_past(qi), scores, update, sa_ref, sb_ref)
    _store_pairs(o_ref, [_normalized(acc_ref[h]) for h in range(HEADS)])


def _diff_kernel(lambda_init, q_ref, k_ref, v_ref, lam_ref, g_ref, o_ref, m_ref, acc_ref, sa_ref, sb_ref, qs_ref):
    qi = pl.program_id(1)
    tq = FLASH_TILE
    diag = _causal_mask(tq, FLASH_TILE)
    _flash_init(m_ref, acc_ref)
    lane = _lane_iota((tq, LANES))
    zero = jnp.zeros((), BF16)
    for t in range(2):
        qt = q_ref[0, :, LANES * t:LANES * (t + 1)]
        for part in range(4):
            keep = (lane >= DIFF_HALF * part) & (lane < DIFF_HALF * (part + 1))
            qs_ref[t, part * tq:(part + 1) * tq, :] = jnp.where(keep, qt, zero)

    def scores(j, diagonal, buf):
        for t in range(2):
            s_all = _dot_nt(qs_ref[t], k_ref[0, _kv_rows(j, FLASH_TILE), LANES * t:LANES * (t + 1)])
            for part in range(4):
                s = s_all[part * tq:(part + 1) * tq]
                buf[4 * t + part] = jnp.where(diag, s, NEG_INF) if diagonal else s

    def update(buf, j):
        for idx in range(2 * HEADS):
            h = idx // 2
            _flash_update(buf[idx], v_ref[0, _kv_rows(j, FLASH_TILE), LANES * h:LANES * (h + 1)], m_ref, acc_ref, idx)

    _pipelined_sweep(qi, _diag_then_past(qi), scores, update, sa_ref, sb_ref)

    lp = lam_ref[...]
    lam = (jnp.exp(jnp.sum(lp[0:1] * lp[1:2], axis=-1, keepdims=True))
           - jnp.exp(jnp.sum(lp[2:3] * lp[3:4], axis=-1, keepdims=True)) + lambda_init)
    outs = []
    for h in range(HEADS):
        o = _normalized(acc_ref[2 * h]) - lam * _normalized(acc_ref[2 * h + 1])
        ms = jnp.sum(jnp.where(lane < HEAD_DIM, o * o, 0.0), axis=-1, keepdims=True) * (1.0 / HEAD_DIM)
        outs.append(o * lax.rsqrt(ms + RMS_EPS) * g_ref[...] * (1.0 - lambda_init))
    _store_pairs(o_ref, outs)


def _sb_kernel(q_ref, k_ref, v_ref, o_ref, cum_ref, acc_ref, za_ref, zb_ref, qs_ref):
    qi = pl.program_id(1)
    tq, tk = SB_TILE, SB_TILE
    lane = _lane_iota((tq, LANES))
    zero = jnp.zeros((), BF16)
    cum_ref[...] = jnp.zeros(cum_ref.shape, F32)
    acc_ref[...] = jnp.zeros(acc_ref.shape, F32)
    for t in range(2):
        qt = q_ref[0, :, LANES * t:LANES * (t + 1)]
        qs_ref[t, 0:tq, :] = jnp.where(lane < HEAD_DIM, qt, zero)
        qs_ref[t, tq:2 * tq, :] = jnp.where(lane >= HEAD_DIM, qt, zero)
    row = lax.broadcasted_iota(jnp.int32, (tk, tk), 0)
    col = lax.broadcasted_iota(jnp.int32, (tk, tk), 1)
    u_tri = jnp.where(row > col, 1.0, 0.0).astype(BF16)
    row2 = lax.broadcasted_iota(jnp.int32, (2 * tq, tk), 0)
    col2 = lax.broadcasted_iota(jnp.int32, (2 * tq, tk), 1)
    past = col2 < jnp.where(row2 >= tq, row2 - tq, row2)

    def scores(j, diagonal, buf):
        for t in range(2):
            z = _dot_nt(qs_ref[t], k_ref[0, _kv_rows(j, tk), LANES * t:LANES * (t + 1)])
            buf[t] = jnp.where(past, z, NEG_INF) if diagonal else z

    def update(buf, j):
        for t in range(2):
            z = buf[t]
            sp = jnp.maximum(z, 0.0) + jnp.log(1.0 + jnp.exp(-jnp.abs(z)))
            later = _dot(sp.astype(BF16), u_tri)
            cum = cum_ref[t]
            a = jnp.concatenate(
                [jnp.exp(z[:, LANES * n:LANES * (n + 1)] - sp[:, LANES * n:LANES * (n + 1)]
                         - later[:, LANES * n:LANES * (n + 1)] - cum) for n in range(tk // LANES)], axis=1)
            acc_ref[t] += _dot(a.astype(BF16), v_ref[0, _kv_rows(j, tk), LANES * t:LANES * (t + 1)])
            cum_ref[t] = cum + jnp.sum(sp, axis=-1, keepdims=True)

    _pipelined_sweep(qi, lambda step: qi - step, scores, update, za_ref, zb_ref)
    for t in range(2):
        acc = acc_ref[t]
        o_ref[0, :, LANES * t:LANES * (t + 1)] = jnp.where(lane < HEAD_DIM, acc[0:tq], acc[tq:2 * tq]).astype(o_ref.dtype)


def _attn_call(kernel, name, tile, proj3, q_off, k_off, v_off, q_width, k_width, v_width, scratch,
               extra_in=(), extra_specs=()):
    bsz, seq, _ = proj3.shape
    return pl.pallas_call(
        kernel,
        grid=(bsz, seq // tile),
        in_specs=[pl.BlockSpec((1, tile, q_width), lambda b, i: (b, i, q_off // q_width)),
                  pl.BlockSpec((1, seq, k_width), lambda b, i: (b, 0, k_off // k_width)),
                  pl.BlockSpec((1, seq, v_width), lambda b, i: (b, 0, v_off // v_width))] + list(extra_specs),
        out_specs=pl.BlockSpec((1, tile, GROUP_WIDTH), lambda b, i: (b, i, 0)),
        out_shape=jax.ShapeDtypeStruct((bsz, seq, GROUP_WIDTH), BF16),
        scratch_shapes=scratch,
        compiler_params=pltpu.CompilerParams(dimension_semantics=("parallel", "parallel"),
                                             vmem_limit_bytes=VMEM_LIMIT_BYTES),
        name=name,
    )(proj3, proj3, proj3, *extra_in)


def _state(n, rows):
    return pltpu.VMEM((n, rows, LANES), F32)


def _staging(n, rows, tk):
    return [pltpu.VMEM((n, rows, tk), F32), pltpu.VMEM((n, rows, tk), F32)]


def _memkv_kernel(mem_ref, wk_ref, wv_ref, k_ref, v_ref):
    mb = mem_ref[...].astype(BF16)
    k_ref[...] = _dot(mb, wk_ref[...]).astype(BF16)
    v_ref[...] = _dot(mb, wv_ref[...]).astype(BF16)


def _memkv_call(mem2, wk, wv):
    rows = mem2.shape[0]
    full = pl.BlockSpec((D_MODEL, D_MODEL), lambda i: (0, 0))
    blk = pl.BlockSpec((N_MEM, D_MODEL), lambda i: (i, 0))
    return pl.pallas_call(
        _memkv_kernel,
        grid=(rows // N_MEM,),
        in_specs=[blk, full, full],
        out_specs=[blk, blk],
        out_shape=[jax.ShapeDtypeStruct((rows, D_MODEL), BF16)] * 2,
        compiler_params=pltpu.CompilerParams(dimension_semantics=("parallel",),
                                             vmem_limit_bytes=VMEM_LIMIT_BYTES),
        name="memkv",
    )(mem2, wk, wv)


def _post_kernel(x_ref, oa_ref, ob_ref, oc_ref, od_ref, wout_ref, g1_ref, b1_ref,
                 wq_ref, km_ref, vm_ref, wo_ref, g2_ref, b2_ref, wrh_ref, wrl_ref, br_ref,
                 x2r_ref, counts_ref, oh_ref, cnt_ref):
    mix = (_dot(oa_ref[...], wout_ref[0:256, :]) + _dot(ob_ref[...], wout_ref[256:512, :])
           + _dot(oc_ref[...], wout_ref[512:768, :]) + _dot(od_ref[...], wout_ref[768:1024, :]))
    x1 = _layer_norm(DEEPNORM_ALPHA * x_ref[...] + mix, g1_ref[...], b1_ref[...])

    q = (_dot(x1.astype(BF16), wq_ref[...]) * (XATTN_HEAD_DIM ** -0.5)).astype(BF16)
    for h in range(XATTN_HEADS):
        c = slice(XATTN_HEAD_DIM * h, XATTN_HEAD_DIM * (h + 1))
        s = _dot_nt(q[:, c], km_ref[0, :, c])
        p = jnp.exp(s - jnp.max(s, axis=-1, keepdims=True))
        o = _dot(p.astype(BF16), vm_ref[0, :, c]) / jnp.sum(p, axis=-1, keepdims=True)
        oh_ref[:, c] = o.astype(BF16)
    xa = _dot(oh_ref[...], wo_ref[...])
    x2 = _layer_norm(DEEPNORM_ALPHA * x1 + xa, g2_ref[...], b2_ref[...])
    x2r_ref[:, 0:D_MODEL] = x2

    hi, lo = _split_bf16(x2)
    logits = _dot(hi, wrh_ref[...]) + _dot(hi, wrl_ref[...]) + _dot(lo, wrh_ref[...]) + br_ref[...]
    lane = _lane_iota(logits.shape)
    gl = jnp.where(lane < N_GROUPS, logits, NEG_INF)
    gmx = jnp.max(gl, axis=-1, keepdims=True)
    gidx = jnp.min(jnp.where(gl == gmx, lane, ROUTER_LANES), axis=-1, keepdims=True)
    gw = 1.0 / jnp.sum(jnp.exp(gl - gmx), axis=-1, keepdims=True)
    e0 = N_GROUPS + EXPERTS_PER_GROUP * gidx
    el = jnp.where((lane >= e0) & (lane < e0 + EXPERTS_PER_GROUP), logits, NEG_INF)
    m1 = jnp.max(el, axis=-1, keepdims=True)
    i1 = jnp.min(jnp.where(el == m1, lane, ROUTER_LANES), axis=-1, keepdims=True)
    el = jnp.where(lane == i1, NEG_INF, el)
    m2 = jnp.max(el, axis=-1, keepdims=True)
    i2 = jnp.min(jnp.where(el == m2, lane, ROUTER_LANES), axis=-1, keepdims=True)
    e = jnp.exp(m2 - m1)
    w1 = gw / (1.0 + e)
    w2 = gw * e / (1.0 + e)

    @pl.when(pl.program_id(0) == 0)
    def _():
        cnt_ref[...] = jnp.zeros_like(cnt_ref)

    tm = logits.shape[0]
    onehot = jnp.where(lane == gidx, 1.0, 0.0)
    earlier = (lax.broadcasted_iota(jnp.int32, (tm, tm), 1) < lax.broadcasted_iota(jnp.int32, (tm, tm), 0))
    before = _dot(jnp.where(earlier, 1.0, 0.0).astype(BF16), onehot.astype(BF16)) + cnt_ref[...]
    rank = jnp.sum(onehot * before, axis=-1, keepdims=True)
    cnt_ref[...] += jnp.sum(onehot, axis=0, keepdims=True)
    counts_ref[...] = cnt_ref[...]
    x2r_ref[:, D_MODEL:] = (jnp.where(lane == i1, w1, 0.0) + jnp.where(lane == i2, w2, 0.0)
                            + jnp.where(lane == 0, gidx.astype(F32), 0.0) + jnp.where(lane == 1, rank, 0.0))


def _post_call(x, oa, ob, oc, od, wout, g1, b1, wq, kmem, vmem, wo, g2, b2, wrh, wrl, br, seq):
    t_tokens = x.shape[0]
    tm = PROJ_TM
    per_seq = seq // tm
    row = lambda w: pl.BlockSpec((tm, w), lambda i: (i, 0))
    full = lambda shape: pl.BlockSpec(shape, lambda i: (0,) * len(shape))
    mem_spec = pl.BlockSpec((1, N_MEM, D_MODEL), lambda i: (i // per_seq, 0, 0))
    sq = (D_MODEL, D_MODEL)
    return pl.pallas_call(
        _post_kernel,
        grid=(t_tokens // tm,),
        in_specs=[row(D_MODEL), row(256), row(256), row(256), row(256),
                  full(sq), full((1, D_MODEL)), full((1, D_MODEL)),
                  full(sq), mem_spec, mem_spec, full(sq), full((1, D_MODEL)), full((1, D_MODEL)),
                  full((D_MODEL, ROUTER_LANES)), full((D_MODEL, ROUTER_LANES)), full((1, ROUTER_LANES))],
        out_specs=[row(MOE_ROW_WIDTH), full((1, ROUTER_LANES))],
        out_shape=[jax.ShapeDtypeStruct((t_tokens, MOE_ROW_WIDTH), F32),
                   jax.ShapeDtypeStruct((1, ROUTER_LANES), F32)],
        scratch_shapes=[pltpu.VMEM((tm, D_MODEL), BF16), pltpu.VMEM((1, ROUTER_LANES), F32)],
        compiler_params=pltpu.CompilerParams(dimension_semantics=("arbitrary",),
                                             vmem_limit_bytes=VMEM_LIMIT_BYTES),
        name="post",
    )(x, oa, ob, oc, od, wout, g1, b1, wq, kmem, vmem, wo, g2, b2, wrh, wrl, br)


def _row_dma_wait(src_ref, dst_ref, sem, rows):
    pltpu.make_async_copy(src_ref.at[pl.ds(0, rows)], dst_ref.at[pl.ds(0, rows)], sem).wait()


def _issue_row_dmas(copy_of_row, sems):
    def body(i, carry):
        for prio in range(2):
            copy_of_row(2 * i + prio, sems[prio]).start(priority=prio)
        return carry

    lax.fori_loop(0, MOE_ROWS_PER_STEP // 2, body, 0, unroll=4)


def _gather_kernel(pos_ref, x_ref, xs_init_ref, xs_ref, sem0, sem1):
    del xs_init_ref
    _issue_row_dmas(lambda r, sem: pltpu.make_async_copy(
        x_ref.at[pl.ds(r, 1)], xs_ref.at[pl.ds(pos_ref[0, 0, r], 1)], sem), (sem0, sem1))
    for sem in (sem0, sem1):
        _row_dma_wait(x_ref, xs_ref, sem, MOE_ROWS_PER_STEP // 2)


def _scatter_kernel(pos_ref, ys_ref, y_ref, sem0, sem1):
    _issue_row_dmas(lambda r, sem: pltpu.make_async_copy(
        ys_ref.at[pl.ds(pos_ref[0, 0, r], 1)], y_ref.at[pl.ds(r, 1)], sem), (sem0, sem1))
    for sem in (sem0, sem1):
        _row_dma_wait(ys_ref, y_ref, sem, MOE_ROWS_PER_STEP // 2)


def _pos_spec():
    return pl.BlockSpec((1, 1, MOE_ROWS_PER_STEP), lambda i: (i, 0, 0), memory_space=pltpu.SMEM)


def _token_rows_spec(width):
    return pl.BlockSpec((MOE_ROWS_PER_STEP, width), lambda i: (i, 0))


def _gather_call(pos3, x2r, cap_rows):
    n_steps = pos3.shape[0]
    any_spec = pl.BlockSpec(memory_space=pl.ANY)
    xs0 = jnp.zeros((cap_rows, x2r.shape[1]), x2r.dtype)
    return pl.pallas_call(
        _gather_kernel,
        grid=(n_steps,),
        in_specs=[_pos_spec(), _token_rows_spec(x2r.shape[1]), any_spec],
        out_specs=any_spec,
        out_shape=jax.ShapeDtypeStruct(xs0.shape, xs0.dtype),
        input_output_aliases={2: 0},
        scratch_shapes=[pltpu.SemaphoreType.DMA(()), pltpu.SemaphoreType.DMA(())],
        compiler_params=pltpu.CompilerParams(dimension_semantics=("arbitrary",), has_side_effects=True),
        name="moe_gather",
    )(pos3, x2r, xs0)


def _scatter_call(pos3, ys, t_tokens):
    n_steps = pos3.shape[0]
    any_spec = pl.BlockSpec(memory_space=pl.ANY)
    return pl.pallas_call(
        _scatter_kernel,
        grid=(n_steps,),
        in_specs=[_pos_spec(), any_spec],
        out_specs=_token_rows_spec(ys.shape[1]),
        out_shape=jax.ShapeDtypeStruct((t_tokens, ys.shape[1]), ys.dtype),
        scratch_shapes=[pltpu.SemaphoreType.DMA(()), pltpu.SemaphoreType.DMA(())],
        compiler_params=pltpu.CompilerParams(dimension_semantics=("arbitrary",)),
        name="moe_scatter",
    )(pos3, ys)


def _moe_kernel(tile_grp_ref, n_used_ref, xs_ref, wg_ref, wu_ref, wd_ref, g_ref, b_ref, o_ref):
    i = pl.program_id(0)

    @pl.when(i < n_used_ref[0])
    def _():
        grp = tile_grp_ref[i]
        x2 = xs_ref[:, 0:D_MODEL]
        xb = x2.astype(BF16)
        route = xs_ref[:, D_MODEL:]
        lane = _lane_iota(route.shape)
        acc = jnp.zeros(x2.shape, F32)
        for j in range(EXPERTS_PER_GROUP):
            e = N_GROUPS + EXPERTS_PER_GROUP * grp + j
            gate_col = jnp.sum(jnp.where(lane == e, route, 0.0), axis=-1, keepdims=True)
            gg = _dot(xb, wg_ref[j])
            uu = _dot(xb, wu_ref[j])
            hid = gg * (1.0 / (1.0 + jnp.exp(-gg))) * uu * gate_col
            acc = acc + _dot(hid.astype(BF16), wd_ref[j])
        o_ref[...] = _layer_norm(DEEPNORM_ALPHA * x2 + acc, g_ref[...], b_ref[...])

    @pl.when(i >= n_used_ref[0])
    def _():
        o_ref[...] = jnp.zeros_like(o_ref)


def _moe_call(tile_grp, n_used, xs, wg, wu, wd, g, b):
    cap_rows = xs.shape[0]
    tm = MOE_TM
    row = lambda w: pl.BlockSpec((tm, w), lambda i, tg, nu: (i, 0))
    vec = pl.BlockSpec((1, D_MODEL), lambda i, tg, nu: (0, 0))
    wspec = lambda shape: pl.BlockSpec(shape, lambda i, tg, nu: (tg[i], 0, 0))
    return pl.pallas_call(
        _moe_kernel,
        grid_spec=pltpu.PrefetchScalarGridSpec(
            num_scalar_prefetch=2,
            grid=(cap_rows // tm,),
            in_specs=[row(MOE_ROW_WIDTH),
                      wspec((EXPERTS_PER_GROUP, D_MODEL, EXPERT_FF)),
                      wspec((EXPERTS_PER_GROUP, D_MODEL, EXPERT_FF)),
                      wspec((EXPERTS_PER_GROUP, EXPERT_FF, D_MODEL)),
                      vec, vec],
            out_specs=row(D_MODEL)),
        out_shape=jax.ShapeDtypeStruct((cap_rows, D_MODEL), F32),
        compiler_params=pltpu.CompilerParams(dimension_semantics=("arbitrary",),
                                             vmem_limit_bytes=VMEM_LIMIT_BYTES),
        name="moe",
    )(tile_grp, n_used, xs, wg, wu, wd, g, b)


def _moe_layer(x2r, counts, wg, wu, wd, g, b):
    t_tokens = x2r.shape[0]
    tm = MOE_TM
    n_tiles = t_tokens // tm + N_GROUPS
    cap_rows = n_tiles * tm
    gid = x2r[:, D_MODEL].astype(jnp.int32)
    rank = x2r[:, D_MODEL + 1].astype(jnp.int32)
    cnt = counts[0, :N_GROUPS].astype(jnp.int32)
    tiles_per_grp = (cnt + tm - 1) // tm
    tile_end = jnp.cumsum(tiles_per_grp)
    pos = ((tile_end - tiles_per_grp) * tm)[gid] + rank
    tile_ids = jnp.arange(n_tiles, dtype=jnp.int32)
    tile_grp = jnp.minimum(jnp.sum((tile_ids[:, None] >= tile_end[None, :]).astype(jnp.int32), axis=1),
                           N_GROUPS - 1)
    n_used = tile_end[-1:].astype(jnp.int32)
    pos3 = pos.reshape(t_tokens // MOE_ROWS_PER_STEP, 1, MOE_ROWS_PER_STEP)

    xs = _gather_call(pos3, x2r, cap_rows)
    ys = _moe_call(tile_grp, n_used, xs, wg, wu, wd, g, b)
    return _scatter_call(pos3, ys, t_tokens)


def _rope_lane_tables(positions, rot, period, offset):
    half = rot // 2
    inv = ROPE_THETA ** (-jnp.arange(0, rot, 2, dtype=F32) / rot)
    ang = positions.astype(F32).reshape(-1)[:, None] * inv
    cos, sin = jnp.cos(ang), jnp.sin(ang)
    p = jnp.arange(LANES) % period - offset
    first = (p >= 0) & (p < half)
    second = (p >= half) & (p < rot)
    idx = jnp.clip(jnp.where(second, p - half, p), 0, half - 1)
    cg, sg = cos[:, idx], sin[:, idx]
    c = jnp.where(first | second, cg, 1.0)
    sa = jnp.where(first, -sg, 0.0)
    sb = jnp.where(second, sg, 0.0)
    return c, sa, sb


def _prep_proj_weights(w_in, w_uq, w_ukv, q_norm, kv_norm):
    q_lat, kv_lat, k_rope = w_in[:, 0:192], w_in[:, 192:320], w_in[:, 320:352]
    sb, mb, df = w_in[:, 352:1120], w_in[:, 1120:1888], w_in[:, 1888:2656]
    zeros = lambda n: jnp.zeros((D_MODEL, n), w_in.dtype)

    def head_tiles(w):
        w = w.reshape(w.shape[0], HEADS, HEAD_DIM)
        return jnp.pad(w, ((0, 0), (0, 0), (0, LANES - HEAD_DIM))).reshape(w.shape[0], HEADS * LANES)

    w1 = jnp.concatenate([head_tiles(mb[:, 0:256]), head_tiles(mb[:, 256:512]), head_tiles(mb[:, 512:768]),
                          head_tiles(df[:, 512:768]),
                          q_lat, zeros(64), kv_lat, zeros(64), k_rope, zeros(32),
                          sb, df[:, 0:512]], axis=1)
    wuq = w_uq.reshape(MLA_Q_LORA, HEADS, MLA_NOPE + MLA_ROPE)
    wuq = jnp.pad(wuq, ((0, 256 - MLA_Q_LORA), (0, 0), (0, LANES - MLA_NOPE - MLA_ROPE))).reshape(256, HEADS * LANES)
    wukv = w_ukv.reshape(MLA_KV_LORA, HEADS, 2, HEAD_DIM)
    wk = head_tiles(wukv[:, :, 0].reshape(MLA_KV_LORA, HEADS * HEAD_DIM))
    wv = head_tiles(wukv[:, :, 1].reshape(MLA_KV_LORA, HEADS * HEAD_DIM))
    gq = jnp.pad(q_norm, (0, 256 - MLA_Q_LORA)).reshape(1, 256)
    return (w1.astype(BF16), wuq.astype(BF16), jnp.concatenate([wk, wv], axis=1).astype(BF16),
            gq, kv_norm.reshape(1, MLA_KV_LORA))


def kernel(x, mem, positions, w_in, mla_q_norm, w_uq, mla_kv_norm, w_ukv, diff_lambda, diff_subln,
           w_out, ln_mix_g, ln_mix_b, xattn_wq, xattn_wk, xattn_wv, xattn_wo, ln_mem_g, ln_mem_b,
           router_group_w, router_group_b, router_expert_w, router_expert_b,
           expert_w_gate, expert_w_up, expert_w_down, ln_ffn_g, ln_ffn_b):
    bsz, seq, _ = x.shape
    t_tokens = bsz * seq
    n_blk = seq // MOBA_BLOCK
    assert seq % PROJ_TM == 0 and seq % FLASH_TILE == 0 and seq % SB_TILE == 0
    assert t_tokens % MOE_TM == 0 and t_tokens % MOE_ROWS_PER_STEP == 0
    assert mem.shape[1] == N_MEM and n_blk <= LANES - HEAD_DIM

    tables = (_rope_lane_tables(positions, MOBA_ROT, LANES, 0)
              + _rope_lane_tables(positions, DIFF_ROT, DIFF_HALF, 0)
              + _rope_lane_tables(positions, MLA_ROPE, LANES, MLA_NOPE))
    mem2 = mem.reshape(bsz * N_MEM, D_MODEL)
    row = lambda v: v.reshape(1, -1)
    xf = x.reshape(t_tokens, D_MODEL)

    for l in range(DEPTH):
        lambda_init = 0.8 - 0.6 * math.exp(-0.3 * l)
        w1, wuq, wukv, gq, gkv = _prep_proj_weights(w_in[l], w_uq[l], w_ukv[l], mla_q_norm[l], mla_kv_norm[l])
        proj, kmean = _proj_call(xf, w1, wuq, wukv, gq, gkv, tables, n_blk)
        proj3 = proj.reshape(bsz, seq, PROJ_OUT_COLS)
        kmean3 = kmean.reshape(bsz, n_blk, HEADS * LANES)

        ft, st = FLASH_TILE, SB_TILE
        o_mla = _attn_call(_mla_kernel, "mla_attn", ft, proj3, O_MLQ, O_MLK, O_MLV, 512, 512, 512,
                           [_state(HEADS, ft), _state(HEADS, ft)] + _staging(HEADS, ft, ft))
        o_sb = _attn_call(_sb_kernel, "sb_attn", st, proj3, O_SBQ, O_SBK, O_SBV, 256, 256, 256,
                          [_state(2, 2 * st), _state(2, 2 * st)]
                          + _staging(2, 2 * st, st) + [pltpu.VMEM((2, 2 * st, LANES), BF16)])
        o_mb = _attn_call(_moba_kernel, "moba_attn", ft, proj3, O_MBQ, O_MBK, O_MBV, 512, 512, 512,
                          [_state(HEADS, ft), _state(HEADS, ft)] + _staging(HEADS, ft, ft)
                          + [pltpu.VMEM((HEADS, ft, LANES), BF16), pltpu.VMEM((LANES, HEADS * LANES), F32)],
                          extra_in=(kmean3,),
                          extra_specs=(pl.BlockSpec((1, n_blk, HEADS * LANES), lambda b, i: (b, 0, 0)),))
        subln = jnp.tile(diff_subln[l], LANES // HEAD_DIM).reshape(1, LANES)
        o_df = _attn_call(functools.partial(_diff_kernel, lambda_init), "diff_attn", ft, proj3,
                          O_DFQ, O_DFK, O_DFV, 256, 256, 512,
                          [_state(2 * HEADS, ft), _state(2 * HEADS, ft)] + _staging(2 * HEADS, ft, ft)
                          + [pltpu.VMEM((2, 4 * ft, LANES), BF16)],
                          extra_in=(diff_lambda[l], subln),
                          extra_specs=(pl.BlockSpec((4, DIFF_HALF), lambda b, i: (0, 0)),
                                       pl.BlockSpec((1, LANES), lambda b, i: (0, 0))))

        kmem, vmem = _memkv_call(mem2, xattn_wk[l].astype(BF16), xattn_wv[l].astype(BF16))
        wr = jnp.concatenate([router_group_w[l], router_expert_w[l],
                              jnp.zeros((D_MODEL, ROUTER_LANES - N_GROUPS - N_EXPERTS), F32)], axis=1)
        wr_hi = wr.astype(BF16)
        wr_lo = (wr - wr_hi.astype(F32)).astype(BF16)
        br = jnp.concatenate([router_group_b[l], router_expert_b[l],
                              jnp.zeros((ROUTER_LANES - N_GROUPS - N_EXPERTS,), F32)]).reshape(1, ROUTER_LANES)
        flat = lambda o: o.reshape(t_tokens, GROUP_WIDTH)
        x2r, counts = _post_call(
            xf, flat(o_mla), flat(o_sb), flat(o_mb), flat(o_df), w_out[l].astype(BF16),
            row(ln_mix_g[l]), row(ln_mix_b[l]), xattn_wq[l].astype(BF16),
            kmem.reshape(bsz, N_MEM, D_MODEL), vmem.reshape(bsz, N_MEM, D_MODEL), xattn_wo[l].astype(BF16),
            row(ln_mem_g[l]), row(ln_mem_b[l]), wr_hi, wr_lo, br, seq)

        xf = _moe_layer(x2r, counts, expert_w_gate[l].astype(BF16), expert_w_up[l].astype(BF16),
                        expert_w_down[l].astype(BF16), row(ln_ffn_g[l]), row(ln_ffn_b[l]))
    return xf.reshape(bsz, seq, D_MODEL)
```

```python
import functools
import math

import jax
import jax.numpy as jnp
from jax import lax
from jax.experimental import pallas as pl
from jax.experimental.pallas import tpu as pltpu

F32 = jnp.float32
BF16 = jnp.bfloat16
NEG_INF = float("-inf")

D_MODEL = 1024
DEPTH = 4
N_MEM = 256
HEAD_DIM = 64
GROUP_WIDTH = 256
HEADS = 4
ROPE_THETA = 500000.0
MLA_Q_LORA = 192
MLA_KV_LORA = 128
MLA_NOPE = 64
MLA_ROPE = 32
MOBA_BLOCK = 256
MOBA_TOPK = 3
MOBA_ROT = 16
DIFF_HALF = 32
DIFF_ROT = 8
XATTN_HEADS = 4
XATTN_HEAD_DIM = 256
N_GROUPS = 8
EXPERTS_PER_GROUP = 4
N_EXPERTS = 32
EXPERT_FF = 256
DEEPNORM_ALPHA = (2 * DEPTH) ** 0.25
LN_EPS = 1e-5
RMS_EPS = 1e-6

LANES = 128
VMEM_LIMIT_BYTES = 56 * 1024 * 1024
PROJ_TM = 512
SB_TILE = 256
FLASH_TILE = 512
MOE_TM = 512
MOE_ROWS_PER_STEP = 512

C_MB, C_DF, C_LAT, C_SB = 0, 768, 1536, 2048
PROJ_IN_COLS = 2816
O_MBQ, O_MBK, O_MBV, O_DFV, O_MLQ, O_MLK, O_MLV = 0, 512, 1024, 1536, 2048, 2560, 3072
O_SBQ, O_SBK, O_SBV, O_DFQ, O_DFK = 3584, 3840, 4096, 4352, 4608
PROJ_OUT_COLS = 4864
ROUTER_LANES = 128
MOE_ROW_WIDTH = D_MODEL + ROUTER_LANES
LOG2E = 1.4426950408889634
MOBA_MASK_BIAS = 2.0 ** 100


def _dot(a, b):
    return jnp.dot(a, b, preferred_element_type=F32)


def _dot_nt(a, b):
    return lax.dot_general(a, b, (((1,), (1,)), ((), ())), preferred_element_type=F32)


def _split_bf16(x):
    hi = x.astype(BF16)
    lo = (x - hi.astype(F32)).astype(BF16)
    return hi, lo


def _layer_norm(x, g, b):
    mu = jnp.mean(x, axis=-1, keepdims=True)
    xc = x - mu
    var = jnp.mean(xc * xc, axis=-1, keepdims=True)
    return xc * lax.rsqrt(var + LN_EPS) * g + b


def _lane_iota(shape):
    return lax.broadcasted_iota(jnp.int32, shape, 1)


def _rope128(t, c, sa, sb, half):
    nxt = pltpu.roll(t, LANES - half, axis=1)
    prv = pltpu.roll(t, half, axis=1)
    return t * c + nxt * sa + prv * sb


def _proj_kernel(n_blk, x_ref, w1_ref, wuq_ref, wukv_ref, gq_ref, gkv_ref,
                 mbc_ref, mbsa_ref, mbsb_ref, dfc_ref, dfsa_ref, dfsb_ref,
                 mlc_ref, mlsa_ref, mlsb_ref, out_ref, kmean_ref):
    xb = x_ref[...].astype(BF16)
    tm = xb.shape[0]
    lane = _lane_iota((tm, LANES))
    upper = lane >= HEAD_DIM

    def mm(c0, width):
        return _dot(xb, w1_ref[:, c0:c0 + width])

    def put(c0, val):
        out_ref[:, c0:c0 + val.shape[1]] = val.astype(BF16)

    def with_ones(v):
        return jnp.where(upper, 1.0, v)

    def head_tile(pair, odd):
        return pltpu.roll(pair, HEAD_DIM, axis=1) if odd else pair

    mb = mm(C_MB, 768)
    mbc, mbsa, mbsb = mbc_ref[...], mbsa_ref[...], mbsb_ref[...]
    per_step = tm // MOBA_BLOCK
    base_blk = lax.rem(pl.program_id(0) * per_step, n_blk)
    row_blk = jnp.right_shift(lax.broadcasted_iota(jnp.int32, (tm, LANES), 0), int(math.log2(MOBA_BLOCK)))
    onehot = lane == HEAD_DIM + base_blk + row_blk
    for t in range(2):
        c = slice(LANES * t, LANES * (t + 1))
        qp = _rope128(mb[:, c], mbc, mbsa, mbsb, MOBA_ROT // 2) * (HEAD_DIM ** -0.5 * LOG2E)
        kp = _rope128(mb[:, 256 + LANES * t:256 + LANES * (t + 1)], mbc, mbsa, mbsb, MOBA_ROT // 2)
        vp = mb[:, 512 + LANES * t:512 + LANES * (t + 1)]
        for odd in range(2):
            h = 2 * t + odd
            put(O_MBQ + LANES * h, jnp.where(upper, 0.0, head_tile(qp, odd)))
            k = jnp.where(upper, 0.0, head_tile(kp, odd))
            for r in range(per_step):
                kmean_ref[0, r:r + 1, LANES * h:LANES * (h + 1)] = jnp.mean(
                    k[r * MOBA_BLOCK:(r + 1) * MOBA_BLOCK], axis=0, keepdims=True)
            put(O_MBK + LANES * h, jnp.where(onehot, 1.0, k))
            put(O_MBV + LANES * h, with_ones(head_tile(vp, odd)))

    df = mm(C_DF, 768)
    dfc, dfsa, dfsb = dfc_ref[...], dfsa_ref[...], dfsb_ref[...]
    for t in range(2):
        c = slice(LANES * t, LANES * (t + 1))
        put(O_DFQ + LANES * t, _rope128(df[:, c], dfc, dfsa, dfsb, DIFF_ROT // 2) * (DIFF_HALF ** -0.5 * LOG2E))
        put(O_DFK + LANES * t, _rope128(df[:, 256 + LANES * t:256 + LANES * (t + 1)], dfc, dfsa, dfsb,
                                        DIFF_ROT // 2))
        for odd in range(2):
            put(O_DFV + LANES * (2 * t + odd),
                with_ones(head_tile(df[:, 512 + LANES * t:512 + LANES * (t + 1)], odd)))

    sb = mm(C_SB, 768)
    put(O_SBQ, sb[:, 0:256] * 0.125)
    put(O_SBK, sb[:, 256:768])

    lat = mm(C_LAT, 512)
    mlc, mlsa, mlsb = mlc_ref[...], mlsa_ref[...], mlsb_ref[...]
    ql = lat[:, 0:256]
    qn = ql * lax.rsqrt(jnp.sum(ql * ql, axis=-1, keepdims=True) * (1.0 / MLA_Q_LORA) + RMS_EPS) * gq_ref[...]
    qm = _dot(qn.astype(BF16), wuq_ref[...])
    kvl = lat[:, 256:384]
    kvn = kvl * lax.rsqrt(jnp.mean(kvl * kvl, axis=-1, keepdims=True) + RMS_EPS) * gkv_ref[...]
    kv = _dot(kvn.astype(BF16), wukv_ref[...])
    kpe = _rope128(lat[:, 384:512], mlc, mlsa, mlsb, MLA_ROPE // 2)
    mscale = (MLA_NOPE + MLA_ROPE) ** -0.5 * LOG2E
    for h in range(HEADS):
        c = slice(LANES * h, LANES * (h + 1))
        put(O_MLQ + LANES * h, _rope128(qm[:, c], mlc, mlsa, mlsb, MLA_ROPE // 2) * mscale)
        put(O_MLK + LANES * h, kv[:, c] + kpe)
        put(O_MLV + LANES * h, with_ones(kv[:, 512 + LANES * h:512 + LANES * (h + 1)]))


def _proj_call(x, w1, wuq, wukv, gq, gkv, tables, n_blk):
    t_tokens = x.shape[0]
    n_steps = t_tokens // PROJ_TM
    full = lambda shape: pl.BlockSpec(shape, lambda i: (0,) * len(shape))
    tab_spec = pl.BlockSpec((PROJ_TM, LANES), lambda i: (i, 0))
    return pl.pallas_call(
        functools.partial(_proj_kernel, n_blk),
        grid=(n_steps,),
        in_specs=[pl.BlockSpec((PROJ_TM, D_MODEL), lambda i: (i, 0)),
                  full((D_MODEL, PROJ_IN_COLS)), full((256, 512)), full((128, 1024)),
                  full((1, 256)), full((1, 128))] + [tab_spec] * 9,
        out_specs=[pl.BlockSpec((PROJ_TM, PROJ_OUT_COLS), lambda i: (i, 0)),
                   pl.BlockSpec((1, PROJ_TM // MOBA_BLOCK, 512), lambda i: (i, 0, 0))],
        out_shape=[jax.ShapeDtypeStruct((t_tokens, PROJ_OUT_COLS), BF16),
                   jax.ShapeDtypeStruct((n_steps, PROJ_TM // MOBA_BLOCK, 512), F32)],
        compiler_params=pltpu.CompilerParams(dimension_semantics=("parallel",),
                                             vmem_limit_bytes=VMEM_LIMIT_BYTES),
        name="proj",
    )(x, w1, wuq, wukv, gq, gkv, *tables)


def _causal_mask(tq, tk):
    return lax.broadcasted_iota(jnp.int32, (tq, tk), 1) <= lax.broadcasted_iota(jnp.int32, (tq, tk), 0)


def _kv_rows(j, tile):
    return pl.ds(pl.multiple_of(j * tile, tile), tile)


def _flash_update(s, v_tile, m_ref, acc_ref, idx):
    m_prev = m_ref[idx]
    m_new = jnp.maximum(m_prev, jnp.max(s, axis=-1, keepdims=True))
    p = jnp.concatenate([jnp.exp2(s[:, LANES * t:LANES * (t + 1)] - m_new) for t in range(s.shape[1] // LANES)],
                        axis=1)
    acc_ref[idx] = jnp.exp2(m_prev - m_new) * acc_ref[idx] + _dot(p.astype(BF16), v_tile)
    m_ref[idx] = m_new


def _flash_init(m_ref, acc_ref):
    m_ref[...] = jnp.full(m_ref.shape, NEG_INF, F32)
    acc_ref[...] = jnp.zeros(acc_ref.shape, F32)


def _normalized(acc):
    return acc / pltpu.roll(acc, HEAD_DIM, axis=1)


def _store_pairs(o_ref, outs):
    lane = _lane_iota(outs[0].shape)
    for t in range(2):
        pair = jnp.where(lane < HEAD_DIM, outs[2 * t], pltpu.roll(outs[2 * t + 1], HEAD_DIM, axis=1))
        o_ref[0, :, LANES * t:LANES * (t + 1)] = pair.astype(o_ref.dtype)


def _pipelined_sweep(qi, block_of, scores, update, buf_a, buf_b):
    scores(block_of(0), True, buf_a)

    def body(p, carry):
        t = 2 * p
        scores(block_of(t + 1), False, buf_b)
        update(buf_a, block_of(t))
        scores(block_of(t + 2), False, buf_a)
        update(buf_b, block_of(t + 1))
        return carry

    lax.fori_loop(0, jnp.right_shift(qi, 1), body, 0)
    odd = lax.rem(qi, 2)

    @pl.when(odd == 1)
    def _():
        scores(block_of(qi), False, buf_b)
        update(buf_a, block_of(qi - 1))
        update(buf_b, block_of(qi))

    @pl.when(odd == 0)
    def _():
        update(buf_a, block_of(qi))


def _diag_then_past(qi):
    return lambda step: jnp.where(step == 0, qi, step - 1)


def _mla_kernel(q_ref, k_ref, v_ref, o_ref, m_ref, acc_ref, sa_ref, sb_ref):
    qi = pl.program_id(1)
    diag = _causal_mask(FLASH_TILE, FLASH_TILE)
    _flash_init(m_ref, acc_ref)

    def scores(j, diagonal, buf):
        for h in range(HEADS):
            c = slice(LANES * h, LANES * (h + 1))
            s = _dot_nt(q_ref[0, :, c], k_ref[0, _kv_rows(j, FLASH_TILE), c])
            buf[h] = jnp.where(diag, s, NEG_INF) if diagonal else s

    def update(buf, j):
        for h in range(HEADS):
            _flash_update(buf[h], v_ref[0, _kv_rows(j, FLASH_TILE), LANES * h:LANES * (h + 1)], m_ref, acc_ref, h)

    _pipelined_sweep(qi, _diag_then_past(qi), scores, update, sa_ref, sb_ref)
    _store_pairs(o_ref, [_normalized(acc_ref[h]) for h in range(HEADS)])


def _moba_kernel(q_ref, k_ref, v_ref, kmean_ref, o_ref, m_ref, acc_ref, sa_ref, sb_ref, qx_ref):
    qi = pl.program_id(1)
    tq = FLASH_TILE
    n_blk = kmean_ref.shape[1]
    diag = _causal_mask(tq, FLASH_TILE)
    _flash_init(m_ref, acc_ref)
    blk = lax.broadcasted_iota(jnp.int32, (n_blk, tq), 0)
    own = (tq // MOBA_BLOCK) * qi + jnp.right_shift(lax.broadcasted_iota(jnp.int32, (n_blk, tq), 1),
                                                    int(math.log2(MOBA_BLOCK)))
    lane = _lane_iota((tq, LANES))
    slot = (lane >= HEAD_DIM) & (lane < HEAD_DIM + n_blk)
    place = jnp.where(lax.broadcasted_iota(jnp.int32, (n_blk, LANES), 1)
                      == HEAD_DIM + lax.broadcasted_iota(jnp.int32, (n_blk, LANES), 0), 1.0, 0.0).astype(BF16)
    for h in range(HEADS):
        c = slice(LANES * h, LANES * (h + 1))
        qh = q_ref[0, :, c]
        km_hi, km_lo = _split_bf16(kmean_ref[0, :, c])
        gate = _dot_nt(km_hi, qh) + _dot_nt(km_lo, qh)
        gate = jnp.where(blk < own, gate, NEG_INF)
        sel = blk == own
        for _ in range(MOBA_TOPK):
            mx = jnp.max(gate, axis=0, keepdims=True)
            first_idx = jnp.min(jnp.where(gate == mx, blk, n_blk), axis=0, keepdims=True)
            pick = (blk == first_idx) & (mx > NEG_INF)
            sel = sel | pick
            gate = jnp.where(pick, NEG_INF, gate)
        placed = lax.dot_general(jnp.where(sel, 1.0, 0.0).astype(BF16), place, (((0,), (0,)), ((), ())),
                                 preferred_element_type=F32)
        bias = ((placed - 1.0) * MOBA_MASK_BIAS).astype(BF16)
        qx_ref[h] = jnp.where(slot, bias, qh)

    def scores(j, diagonal, buf):
        for h in range(HEADS):
            s = _dot_nt(qx_ref[h], k_ref[0, _kv_rows(j, FLASH_TILE), LANES * h:LANES * (h + 1)])
            buf[h] = jnp.where(diag, s, NEG_INF) if diagonal else s

    def update(buf, j):
        for h in range(HEADS):
            _flash_update(buf[h], v_ref[0, _kv_rows(j, FLASH_TILE), LANES * h:LANES * (h + 1)], m_ref, acc_ref, h)

    _pipelined_sweep(qi, _diag_then_past(qi), scores, update, sa_ref, sb_ref)
    _store_pairs(o_ref, [_normalized(acc_ref[h]) for h in range(HEADS)])


def _diff_kernel(lambda_init, q_ref, k_ref, v_ref, lam_ref, g_ref, o_ref, m_ref, acc_ref, sa_ref, sb_ref, qs_ref):
    qi = pl.program_id(1)
    tq = FLASH_TILE
    diag = _causal_mask(tq, FLASH_TILE)
    _flash_init(m_ref, acc_ref)
    lane = _lane_iota((tq, LANES))
    zero = jnp.zeros((), BF16)
    for t in range(2):
        qt = q_ref[0, :, LANES * t:LANES * (t + 1)]
        for part in range(4):
            keep = (lane >= DIFF_HALF * part) & (lane < DIFF_HALF * (part + 1))
            qs_ref[t, part * tq:(part + 1) * tq, :] = jnp.where(keep, qt, zero)

    def scores(j, diagonal, buf):
        for t in range(2):
            s_all = _dot_nt(qs_ref[t], k_ref[0, _kv_rows(j, FLASH_TILE), LANES * t:LANES * (t + 1)])
            for part in range(4):
                s = s_all[part * tq:(part + 1) * tq]
                buf[4 * t + part] = jnp.where(diag, s, NEG_INF) if diagonal else s

    def update(buf, j):
        for idx in range(2 * HEADS):
            h = idx // 2
            _flash_update(buf[idx], v_ref[0, _kv_rows(j, FLASH_TILE), LANES * h:LANES * (h + 1)], m_ref, acc_ref, idx)

    _pipelined_sweep(qi, _diag_then_past(qi), scores, update, sa_ref, sb_ref)

    lp = lam_ref[...]
    lam = (jnp.exp(jnp.sum(lp[0:1] * lp[1:2], axis=-1, keepdims=True))
           - jnp.exp(jnp.sum(lp[2:3] * lp[3:4], axis=-1, keepdims=True)) + lambda_init)
    outs = []
    for h in range(HEADS):
        o = _normalized(acc_ref[2 * h]) - lam * _normalized(acc_ref[2 * h + 1])
        ms = jnp.sum(jnp.where(lane < HEAD_DIM, o * o, 0.0), axis=-1, keepdims=True) * (1.0 / HEAD_DIM)
        outs.append(o * lax.rsqrt(ms + RMS_EPS) * g_ref[...] * (1.0 - lambda_init))
    _store_pairs(o_ref, outs)


def _sb_kernel(q_ref, k_ref, v_ref, o_ref, cum_ref, acc_ref, za_ref, zb_ref, qs_ref):
    qi = pl.program_id(1)
    tq, tk = SB_TILE, SB_TILE
    lane = _lane_iota((tq, LANES))
    zero = jnp.zeros((), BF16)
    cum_ref[...] = jnp.zeros(cum_ref.shape, F32)
    acc_ref[...] = jnp.zeros(acc_ref.shape, F32)
    for t in range(2):
        qt = q_ref[0, :, LANES * t:LANES * (t + 1)]
        qs_ref[t, 0:tq, :] = jnp.where(lane < HEAD_DIM, qt, zero)
        qs_ref[t, tq:2 * tq, :] = jnp.where(lane >= HEAD_DIM, qt, zero)
    row = lax.broadcasted_iota(jnp.int32, (tk, tk), 0)
    col = lax.broadcasted_iota(jnp.int32, (tk, tk), 1)
    u_tri = jnp.where(row > col, 1.0, 0.0).astype(BF16)
    row2 = lax.broadcasted_iota(jnp.int32, (2 * tq, tk), 0)
    col2 = lax.broadcasted_iota(jnp.int32, (2 * tq, tk), 1)
    past = col2 < jnp.where(row2 >= tq, row2 - tq, row2)

    def scores(j, diagonal, buf):
        for t in range(2):
            z = _dot_nt(qs_ref[t], k_ref[0, _kv_rows(j, tk), LANES * t:LANES * (t + 1)])
            buf[t] = jnp.where(past, z, NEG_INF) if diagonal else z

    def update(buf, j):
        for t in range(2):
            z = buf[t]
            sp = jnp.maximum(z, 0.0) + jnp.log(1.0 + jnp.exp(-jnp.abs(z)))
            later = _dot(sp.astype(BF16), u_tri)
            cum = cum_ref[t]
            a = jnp.concatenate(
                [jnp.exp(z[:, LANES * n:LANES * (n + 1)] - sp[:, LANES * n:LANES * (n + 1)]
                         - later[:, LANES * n:LANES * (n + 1)] - cum) for n in range(tk // LANES)], axis=1)
            acc_ref[t] += _dot(a.astype(BF16), v_ref[0, _kv_rows(j, tk), LANES * t:LANES * (t + 1)])
            cum_ref[t] = cum + jnp.sum(sp, axis=-1, keepdims=True)

    _pipelined_sweep(qi, lambda step: qi - step, scores, update, za_ref, zb_ref)
    for t in range(2):
        acc = acc_ref[t]
        o_ref[0, :, LANES * t:LANES * (t + 1)] = jnp.where(lane < HEAD_DIM, acc[0:tq], acc[tq:2 * tq]).astype(o_ref.dtype)


def _attn_call(kernel, name, tile, proj3, q_off, k_off, v_off, q_width, k_width, v_width, scratch,
               extra_in=(), extra_specs=()):
    bsz, seq, _ = proj3.shape
    return pl.pallas_call(
        kernel,
        grid=(bsz, seq // tile),
        in_specs=[pl.BlockSpec((1, tile, q_width), lambda b, i: (b, i, q_off // q_width)),
                  pl.BlockSpec((1, seq, k_width), lambda b, i: (b, 0, k_off // k_width)),
                  pl.BlockSpec((1, seq, v_width), lambda b, i: (b, 0, v_off // v_width))] + list(extra_specs),
        out_specs=pl.BlockSpec((1, tile, GROUP_WIDTH), lambda b, i: (b, i, 0)),
        out_shape=jax.ShapeDtypeStruct((bsz, seq, GROUP_WIDTH), BF16),
        scratch_shapes=scratch,
        compiler_params=pltpu.CompilerParams(dimension_semantics=("parallel", "parallel"),
                                             vmem_limit_bytes=VMEM_LIMIT_BYTES),
        name=name,
    )(proj3, proj3, proj3, *extra_in)


def _state(n, rows):
    return pltpu.VMEM((n, rows, LANES), F32)


def _staging(n, rows, tk):
    return [pltpu.VMEM((n, rows, tk), F32), pltpu.VMEM((n, rows, tk), F32)]


def _memkv_kernel(mem_ref, wk_ref, wv_ref, k_ref, v_ref):
    mb = mem_ref[...].astype(BF16)
    k_ref[...] = _dot(mb, wk_ref[...]).astype(BF16)
    v_ref[...] = _dot(mb, wv_ref[...]).astype(BF16)


def _memkv_call(mem2, wk, wv):
    rows = mem2.shape[0]
    full = pl.BlockSpec((D_MODEL, D_MODEL), lambda i: (0, 0))
    blk = pl.BlockSpec((N_MEM, D_MODEL), lambda i: (i, 0))
    return pl.pallas_call(
        _memkv_kernel,
        grid=(rows // N_MEM,),
        in_specs=[blk, full, full],
        out_specs=[blk, blk],
        out_shape=[jax.ShapeDtypeStruct((rows, D_MODEL), BF16)] * 2,
        compiler_params=pltpu.CompilerParams(dimension_semantics=("parallel",),
                                             vmem_limit_bytes=VMEM_LIMIT_BYTES),
        name="memkv",
    )(mem2, wk, wv)


def _post_kernel(x_ref, oa_ref, ob_ref, oc_ref, od_ref, wout_ref, g1_ref, b1_ref,
                 wq_ref, km_ref, vm_ref, wo_ref, g2_ref, b2_ref, wrh_ref, wrl_ref, br_ref,
                 x2r_ref, counts_ref, oh_ref, cnt_ref):
    mix = (_dot(oa_ref[...], wout_ref[0:256, :]) + _dot(ob_ref[...], wout_ref[256:512, :])
           + _dot(oc_ref[...], wout_ref[512:768, :]) + _dot(od_ref[...], wout_ref[768:1024, :]))
    x1 = _layer_norm(DEEPNORM_ALPHA * x_ref[...] + mix, g1_ref[...], b1_ref[...])

    q = (_dot(x1.astype(BF16), wq_ref[...]) * (XATTN_HEAD_DIM ** -0.5)).astype(BF16)
    for h in range(XATTN_HEADS):
        c = slice(XATTN_HEAD_DIM * h, XATTN_HEAD_DIM * (h + 1))
        s = _dot_nt(q[:, c], km_ref[0, :, c])
        p = jnp.exp(s - jnp.max(s, axis=-1, keepdims=True))
        o = _dot(p.astype(BF16), vm_ref[0, :, c]) / jnp.sum(p, axis=-1, keepdims=True)
        oh_ref[:, c] = o.astype(BF16)
    xa = _dot(oh_ref[...], wo_ref[...])
    x2 = _layer_norm(DEEPNORM_ALPHA * x1 + xa, g2_ref[...], b2_ref[...])
    x2r_ref[:, 0:D_MODEL] = x2

    hi, lo = _split_bf16(x2)
    logits = _dot(hi, wrh_ref[...]) + _dot(hi, wrl_ref[...]) + _dot(lo, wrh_ref[...]) + br_ref[...]
    lane = _lane_iota(logits.shape)
    gl = jnp.where(lane < N_GROUPS, logits, NEG_INF)
    gmx = jnp.max(gl, axis=-1, keepdims=True)
    gidx = jnp.min(jnp.where(gl == gmx, lane, ROUTER_LANES), axis=-1, keepdims=True)
    gw = 1.0 / jnp.sum(jnp.exp(gl - gmx), axis=-1, keepdims=True)
    e0 = N_GROUPS + EXPERTS_PER_GROUP * gidx
    el = jnp.where((lane >= e0) & (lane < e0 + EXPERTS_PER_GROUP), logits, NEG_INF)
    m1 = jnp.max(el, axis=-1, keepdims=True)
    i1 = jnp.min(jnp.where(el == m1, lane, ROUTER_LANES), axis=-1, keepdims=True)
    el = jnp.where(lane == i1, NEG_INF, el)
    m2 = jnp.max(el, axis=-1, keepdims=True)
    i2 = jnp.min(jnp.where(el == m2, lane, ROUTER_LANES), axis=-1, keepdims=True)
    e = jnp.exp(m2 - m1)
    w1 = gw / (1.0 + e)
    w2 = gw * e / (1.0 + e)

    @pl.when(pl.program_id(0) == 0)
    def _():
        cnt_ref[...] = jnp.zeros_like(cnt_ref)

    tm = logits.shape[0]
    onehot = jnp.where(lane == gidx, 1.0, 0.0)
    earlier = (lax.broadcasted_iota(jnp.int32, (tm, tm), 1) < lax.broadcasted_iota(jnp.int32, (tm, tm), 0))
    before = _dot(jnp.where(earlier, 1.0, 0.0).astype(BF16), onehot.astype(BF16)) + cnt_ref[...]
    rank = jnp.sum(onehot * before, axis=-1, keepdims=True)
    cnt_ref[...] += jnp.sum(onehot, axis=0, keepdims=True)
    counts_ref[...] = cnt_ref[...]
    x2r_ref[:, D_MODEL:] = (jnp.where(lane == i1, w1, 0.0) + jnp.where(lane == i2, w2, 0.0)
                            + jnp.where(lane == 0, gidx.astype(F32), 0.0) + jnp.where(lane == 1, rank, 0.0))


def _post_call(x, oa, ob, oc, od, wout, g1, b1, wq, kmem, vmem, wo, g2, b2, wrh, wrl, br, seq):
    t_tokens = x.shape[0]
    tm = PROJ_TM
    per_seq = seq // tm
    row = lambda w: pl.BlockSpec((tm, w), lambda i: (i, 0))
    full = lambda shape: pl.BlockSpec(shape, lambda i: (0,) * len(shape))
    mem_spec = pl.BlockSpec((1, N_MEM, D_MODEL), lambda i: (i // per_seq, 0, 0))
    sq = (D_MODEL, D_MODEL)
    return pl.pallas_call(
        _post_kernel,
        grid=(t_tokens // tm,),
        in_specs=[row(D_MODEL), row(256), row(256), row(256), row(256),
                  full(sq), full((1, D_MODEL)), full((1, D_MODEL)),
                  full(sq), mem_spec, mem_spec, full(sq), full((1, D_MODEL)), full((1, D_MODEL)),
                  full((D_MODEL, ROUTER_LANES)), full((D_MODEL, ROUTER_LANES)), full((1, ROUTER_LANES))],
        out_specs=[row(MOE_ROW_WIDTH), full((1, ROUTER_LANES))],
        out_shape=[jax.ShapeDtypeStruct((t_tokens, MOE_ROW_WIDTH), F32),
                   jax.ShapeDtypeStruct((1, ROUTER_LANES), F32)],
        scratch_shapes=[pltpu.VMEM((tm, D_MODEL), BF16), pltpu.VMEM((1, ROUTER_LANES), F32)],
        compiler_params=pltpu.CompilerParams(dimension_semantics=("arbitrary",),
                                             vmem_limit_bytes=VMEM_LIMIT_BYTES),
        name="post",
    )(x, oa, ob, oc, od, wout, g1, b1, wq, kmem, vmem, wo, g2, b2, wrh, wrl, br)


def _row_dma_wait(src_ref, dst_ref, sem, rows):
    pltpu.make_async_copy(src_ref.at[pl.ds(0, rows)], dst_ref.at[pl.ds(0, rows)], sem).wait()


def _issue_row_dmas(copy_of_row, sems):
    def body(i, carry):
        for prio in range(2):
            copy_of_row(2 * i + prio, sems[prio]).start(priority=prio)
        return carry

    lax.fori_loop(0, MOE_ROWS_PER_STEP // 2, body, 0, unroll=4)


def _gather_kernel(pos_ref, x_ref, xs_init_ref, xs_ref, sem0, sem1):
    del xs_init_ref
    _issue_row_dmas(lambda r, sem: pltpu.make_async_copy(
        x_ref.at[pl.ds(r, 1)], xs_ref.at[pl.ds(pos_ref[0, 0, r], 1)], sem), (sem0, sem1))
    for sem in (sem0, sem1):
        _row_dma_wait(x_ref, xs_ref, sem, MOE_ROWS_PER_STEP // 2)


def _scatter_kernel(pos_ref, ys_ref, y_ref, sem0, sem1):
    _issue_row_dmas(lambda r, sem: pltpu.make_async_copy(
        ys_ref.at[pl.ds(pos_ref[0, 0, r], 1)], y_ref.at[pl.ds(r, 1)], sem), (sem0, sem1))
    for sem in (sem0, sem1):
        _row_dma_wait(ys_ref, y_ref, sem, MOE_ROWS_PER_STEP // 2)


def _pos_spec():
    return pl.BlockSpec((1, 1, MOE_ROWS_PER_STEP), lambda i: (i, 0, 0), memory_space=pltpu.SMEM)


def _token_rows_spec(width):
    return pl.BlockSpec((MOE_ROWS_PER_STEP, width), lambda i: (i, 0))


def _gather_call(pos3, x2r, cap_rows):
    n_steps = pos3.shape[0]
    any_spec = pl.BlockSpec(memory_space=pl.ANY)
    xs0 = jnp.zeros((cap_rows, x2r.shape[1]), x2r.dtype)
    return pl.pallas_call(
        _gather_kernel,
        grid=(n_steps,),
        in_specs=[_pos_spec(), _token_rows_spec(x2r.shape[1]), any_spec],
        out_specs=any_spec,
        out_shape=jax.ShapeDtypeStruct(xs0.shape, xs0.dtype),
        input_output_aliases={2: 0},
        scratch_shapes=[pltpu.SemaphoreType.DMA(()), pltpu.SemaphoreType.DMA(())],
        compiler_params=pltpu.CompilerParams(dimension_semantics=("arbitrary",), has_side_effects=True),
        name="moe_gather",
    )(pos3, x2r, xs0)


def _scatter_call(pos3, ys, t_tokens):
    n_steps = pos3.shape[0]
    any_spec = pl.BlockSpec(memory_space=pl.ANY)
    return pl.pallas_call(
        _scatter_kernel,
        grid=(n_steps,),
        in_specs=[_pos_spec(), any_spec],
        out_specs=_token_rows_spec(ys.shape[1]),
        out_shape=jax.ShapeDtypeStruct((t_tokens, ys.shape[1]), ys.dtype),
        scratch_shapes=[pltpu.SemaphoreType.DMA(()), pltpu.SemaphoreType.DMA(())],
        compiler_params=pltpu.CompilerParams(dimension_semantics=("arbitrary",)),
        name="moe_scatter",
    )(pos3, ys)


def _moe_kernel(tile_grp_ref, n_used_ref, xs_ref, wg_ref, wu_ref, wd_ref, g_ref, b_ref, o_ref):
    i = pl.program_id(0)

    @pl.when(i < n_used_ref[0])
    def _():
        grp = tile_grp_ref[i]
        x2 = xs_ref[:, 0:D_MODEL]
        xb = x2.astype(BF16)
        route = xs_ref[:, D_MODEL:]
        lane = _lane_iota(route.shape)
        acc = jnp.zeros(x2.shape, F32)
        for j in range(EXPERTS_PER_GROUP):
            e = N_GROUPS + EXPERTS_PER_GROUP * grp + j
            gate_col = jnp.sum(jnp.where(lane == e, route, 0.0), axis=-1, keepdims=True)
            gg = _dot(xb, wg_ref[j])
            uu = _dot(xb, wu_ref[j])
            hid = gg * (1.0 / (1.0 + jnp.exp(-gg))) * uu * gate_col
            acc = acc + _dot(hid.astype(BF16), wd_ref[j])
        o_ref[...] = _layer_norm(DEEPNORM_ALPHA * x2 + acc, g_ref[...], b_ref[...])

    @pl.when(i >= n_used_ref[0])
    def _():
        o_ref[...] = jnp.zeros_like(o_ref)


def _moe_call(tile_grp, n_used, xs, wg, wu, wd, g, b):
    cap_rows = xs.shape[0]
    tm = MOE_TM
    row = lambda w: pl.BlockSpec((tm, w), lambda i, tg, nu: (i, 0))
    vec = pl.BlockSpec((1, D_MODEL), lambda i, tg, nu: (0, 0))
    wspec = lambda shape: pl.BlockSpec(shape, lambda i, tg, nu: (tg[i], 0, 0))
    return pl.pallas_call(
        _moe_kernel,
        grid_spec=pltpu.PrefetchScalarGridSpec(
            num_scalar_prefetch=2,
            grid=(cap_rows // tm,),
            in_specs=[row(MOE_ROW_WIDTH),
                      wspec((EXPERTS_PER_GROUP, D_MODEL, EXPERT_FF)),
                      wspec((EXPERTS_PER_GROUP, D_MODEL, EXPERT_FF)),
                      wspec((EXPERTS_PER_GROUP, EXPERT_FF, D_MODEL)),
                      vec, vec],
            out_specs=row(D_MODEL)),
        out_shape=jax.ShapeDtypeStruct((cap_rows, D_MODEL), F32),
        compiler_params=pltpu.CompilerParams(dimension_semantics=("arbitrary",),
                                             vmem_limit_bytes=VMEM_LIMIT_BYTES),
        name="moe",
    )(tile_grp, n_used, xs, wg, wu, wd, g, b)


def _moe_layer(x2r, counts, wg, wu, wd, g, b):
    t_tokens = x2r.shape[0]
    tm = MOE_TM
    n_tiles = t_tokens // tm + N_GROUPS
    cap_rows = n_tiles * tm
    gid = x2r[:, D_MODEL].astype(jnp.int32)
    rank = x2r[:, D_MODEL + 1].astype(jnp.int32)
    cnt = counts[0, :N_GROUPS].astype(jnp.int32)
    tiles_per_grp = (cnt + tm - 1) // tm
    tile_end = jnp.cumsum(tiles_per_grp)
    pos = ((tile_end - tiles_per_grp) * tm)[gid] + rank
    tile_ids = jnp.arange(n_tiles, dtype=jnp.int32)
    tile_grp = jnp.minimum(jnp.sum((tile_ids[:, None] >= tile_end[None, :]).astype(jnp.int32), axis=1),
                           N_GROUPS - 1)
    n_used = tile_end[-1:].astype(jnp.int32)
    pos3 = pos.reshape(t_tokens // MOE_ROWS_PER_STEP, 1, MOE_ROWS_PER_STEP)

    xs = _gather_call(pos3, x2r, cap_rows)
    ys = _moe_call(tile_grp, n_used, xs, wg, wu, wd, g, b)
    return _scatter_call(pos3, ys, t_tokens)


def _rope_lane_tables(positions, rot, period, offset):
    half = rot // 2
    inv = ROPE_THETA ** (-jnp.arange(0, rot, 2, dtype=F32) / rot)
    ang = positions.astype(F32).reshape(-1)[:, None] * inv
    cos, sin = jnp.cos(ang), jnp.sin(ang)
    p = jnp.arange(LANES) % period - offset
    first = (p >= 0) & (p < half)
    second = (p >= half) & (p < rot)
    idx = jnp.clip(jnp.where(second, p - half, p), 0, half - 1)
    cg, sg = cos[:, idx], sin[:, idx]
    c = jnp.where(first | second, cg, 1.0)
    sa = jnp.where(first, -sg, 0.0)
    sb = jnp.where(second, sg, 0.0)
    return c, sa, sb


def _prep_proj_weights(w_in, w_uq, w_ukv, q_norm, kv_norm):
    q_lat, kv_lat, k_rope = w_in[:, 0:192], w_in[:, 192:320], w_in[:, 320:352]
    sb, mb, df = w_in[:, 352:1120], w_in[:, 1120:1888], w_in[:, 1888:2656]
    zeros = lambda n: jnp.zeros((D_MODEL, n), w_in.dtype)

    def head_tiles(w):
        w = w.reshape(w.shape[0], HEADS, HEAD_DIM)
        return jnp.pad(w, ((0, 0), (0, 0), (0, LANES - HEAD_DIM))).reshape(w.shape[0], HEADS * LANES)

    w1 = jnp.concatenate([mb, df, q_lat, zeros(64), kv_lat, zeros(64), k_rope, zeros(32), sb], axis=1)
    wuq = w_uq.reshape(MLA_Q_LORA, HEADS, MLA_NOPE + MLA_ROPE)
    wuq = jnp.pad(wuq, ((0, 256 - MLA_Q_LORA), (0, 0), (0, LANES - MLA_NOPE - MLA_ROPE))).reshape(256, HEADS * LANES)
    wukv = w_ukv.reshape(MLA_KV_LORA, HEADS, 2, HEAD_DIM)
    wk = head_tiles(wukv[:, :, 0].reshape(MLA_KV_LORA, HEADS * HEAD_DIM))
    wv = head_tiles(wukv[:, :, 1].reshape(MLA_KV_LORA, HEADS * HEAD_DIM))
    gq = jnp.pad(q_norm, (0, 256 - MLA_Q_LORA)).reshape(1, 256)
    return (w1.astype(BF16), wuq.astype(BF16), jnp.concatenate([wk, wv], axis=1).astype(BF16),
            gq, kv_norm.reshape(1, MLA_KV_LORA))


def kernel(x, mem, positions, w_in, mla_q_norm, w_uq, mla_kv_norm, w_ukv, diff_lambda, diff_subln,
           w_out, ln_mix_g, ln_mix_b, xattn_wq, xattn_wk, xattn_wv, xattn_wo, ln_mem_g, ln_mem_b,
           router_group_w, router_group_b, router_expert_w, router_expert_b,
           expert_w_gate, expert_w_up, expert_w_down, ln_ffn_g, ln_ffn_b):
    bsz, seq, _ = x.shape
    t_tokens = bsz * seq
    n_blk = seq // MOBA_BLOCK
    assert seq % PROJ_TM == 0 and seq % FLASH_TILE == 0 and seq % SB_TILE == 0
    assert t_tokens % MOE_TM == 0 and t_tokens % MOE_ROWS_PER_STEP == 0
    assert mem.shape[1] == N_MEM and n_blk <= LANES - HEAD_DIM

    tables = (_rope_lane_tables(positions, MOBA_ROT, HEAD_DIM, 0)
              + _rope_lane_tables(positions, DIFF_ROT, DIFF_HALF, 0)
              + _rope_lane_tables(positions, MLA_ROPE, LANES, MLA_NOPE))
    mem2 = mem.reshape(bsz * N_MEM, D_MODEL)
    row = lambda v: v.reshape(1, -1)
    xf = x.reshape(t_tokens, D_MODEL)

    for l in range(DEPTH):
        lambda_init = 0.8 - 0.6 * math.exp(-0.3 * l)
        w1, wuq, wukv, gq, gkv = _prep_proj_weights(w_in[l], w_uq[l], w_ukv[l], mla_q_norm[l], mla_kv_norm[l])
        proj, kmean = _proj_call(xf, w1, wuq, wukv, gq, gkv, tables, n_blk)
        proj3 = proj.reshape(bsz, seq, PROJ_OUT_COLS)
        kmean3 = kmean.reshape(bsz, n_blk, HEADS * LANES)

        ft, st = FLASH_TILE, SB_TILE
        o_mla = _attn_call(_mla_kernel, "mla_attn", ft, proj3, O_MLQ, O_MLK, O_MLV, 512, 512, 512,
                           [_state(HEADS, ft), _state(HEADS, ft)] + _staging(HEADS, ft, ft))
        o_sb = _attn_call(_sb_kernel, "sb_attn", st, proj3, O_SBQ, O_SBK, O_SBV, 256, 256, 256,
                          [_state(2, 2 * st), _state(2, 2 * st)]
                          + _staging(2, 2 * st, st) + [pltpu.VMEM((2, 2 * st, LANES), BF16)])
        o_mb = _attn_call(_moba_kernel, "moba_attn", ft, proj3, O_MBQ, O_MBK, O_MBV, 512, 512, 512,
                          [_state(HEADS, ft), _state(HEADS, ft)] + _staging(HEADS, ft, ft)
                          + [pltpu.VMEM((HEADS, ft, LANES), BF16)],
                          extra_in=(kmean3,),
                          extra_specs=(pl.BlockSpec((1, n_blk, HEADS * LANES), lambda b, i: (b, 0, 0)),))
        subln = jnp.tile(diff_subln[l], LANES // HEAD_DIM).reshape(1, LANES)
        o_df = _attn_call(functools.partial(_diff_kernel, lambda_init), "diff_attn", ft, proj3,
                          O_DFQ, O_DFK, O_DFV, 256, 256, 512,
                          [_state(2 * HEADS, ft), _state(2 * HEADS, ft)] + _staging(2 * HEADS, ft, ft)
                          + [pltpu.VMEM((2, 4 * ft, LANES), BF16)],
                          extra_in=(diff_lambda[l], subln),
                          extra_specs=(pl.BlockSpec((4, DIFF_HALF), lambda b, i: (0, 0)),
                                       pl.BlockSpec((1, LANES), lambda b, i: (0, 0))))

        kmem, vmem = _memkv_call(mem2, xattn_wk[l].astype(BF16), xattn_wv[l].astype(BF16))
        wr = jnp.concatenate([router_group_w[l], router_expert_w[l],
                              jnp.zeros((D_MODEL, ROUTER_LANES - N_GROUPS - N_EXPERTS), F32)], axis=1)
        wr_hi = wr.astype(BF16)
        wr_lo = (wr - wr_hi.astype(F32)).astype(BF16)
        br = jnp.concatenate([router_group_b[l], router_expert_b[l],
                              jnp.zeros((ROUTER_LANES - N_GROUPS - N_EXPERTS,), F32)]).reshape(1, ROUTER_LANES)
        flat = lambda o: o.reshape(t_tokens, GROUP_WIDTH)
        x2r, counts = _post_call(
            xf, flat(o_mla), flat(o_sb), flat(o_mb), flat(o_df), w_out[l].astype(BF16),
            row(ln_mix_g[l]), row(ln_mix_b[l]), xattn_wq[l].astype(BF16),
            kmem.reshape(bsz, N_MEM, D_MODEL), vmem.reshape(bsz, N_MEM, D_MODEL), xattn_wo[l].astype(BF16),
            row(ln_mem_g[l]), row(ln_mem_b[l]), wr_hi, wr_lo, br, seq)

        xf = _moe_layer(x2r, counts, expert_w_gate[l].astype(BF16), expert_w_up[l].astype(BF16),
                        expert_w_down[l].astype(BF16), row(ln_ffn_g[l]), row(ln_ffn_b[l]))
    return xf.reshape(bsz, seq, D_MODEL)
```

```python
import functools
import math

import jax
import jax.numpy as jnp
from jax import lax
from jax.experimental import pallas as pl
from jax.experimental.pallas import tpu as pltpu

F32 = jnp.float32
BF16 = jnp.bfloat16
NEG_INF = float("-inf")

D_MODEL = 1024
DEPTH = 4
N_MEM = 256
HEAD_DIM = 64
GROUP_WIDTH = 256
HEADS = 4
ROPE_THETA = 500000.0
MLA_Q_LORA = 192
MLA_KV_LORA = 128
MLA_NOPE = 64
MLA_ROPE = 32
MOBA_BLOCK = 256
MOBA_TOPK = 3
MOBA_ROT = 16
DIFF_HALF = 32
DIFF_ROT = 8
XATTN_HEADS = 4
XATTN_HEAD_DIM = 256
N_GROUPS = 8
EXPERTS_PER_GROUP = 4
N_EXPERTS = 32
EXPERT_FF = 256
DEEPNORM_ALPHA = (2 * DEPTH) ** 0.25
LN_EPS = 1e-5
RMS_EPS = 1e-6

LANES = 128
VMEM_LIMIT_BYTES = 56 * 1024 * 1024
PROJ_TM = 512
POST_TM = 1024
POST_CHUNK = 512
SB_TILE = 256
FLASH_TILE = 512
MOE_TM = 512
MOE_ROWS_PER_STEP = 512

C_MB, C_DF, C_LAT, C_SB = 0, 768, 1536, 2048
PROJ_IN_COLS = 2816
O_MBQ, O_MBK, O_MBV, O_DFV, O_MLQ, O_MLK, O_MLV = 0, 512, 1024, 1536, 2048, 2560, 3072
O_SBQ, O_SBK, O_SBV, O_DFQ, O_DFK = 3584, 3840, 4096, 4352, 4608
PROJ_OUT_COLS = 4864
ROUTER_LANES = 128
MOE_ROW_WIDTH = D_MODEL + ROUTER_LANES
LOG2E = 1.4426950408889634
MOBA_MASK_BIAS = 2.0 ** 100


def _dot(a, b):
    return jnp.dot(a, b, preferred_element_type=F32)


def _dot_nt(a, b):
    return lax.dot_general(a, b, (((1,), (1,)), ((), ())), preferred_element_type=F32)


def _split_bf16(x):
    hi = x.astype(BF16)
    lo = (x - hi.astype(F32)).astype(BF16)
    return hi, lo


def _layer_norm(x, g, b):
    mu = jnp.mean(x, axis=-1, keepdims=True)
    xc = x - mu
    var = jnp.mean(xc * xc, axis=-1, keepdims=True)
    return xc * lax.rsqrt(var + LN_EPS) * g + b


def _lane_iota(shape):
    return lax.broadcasted_iota(jnp.int32, shape, 1)


def _rope128(t, c, sa, sb, half):
    nxt = pltpu.roll(t, LANES - half, axis=1)
    prv = pltpu.roll(t, half, axis=1)
    return t * c + nxt * sa + prv * sb


def _proj_kernel(n_blk, x_ref, w1_ref, wuq_ref, wukv_ref, gq_ref, gkv_ref,
                 mbc_ref, mbsa_ref, mbsb_ref, dfc_ref, dfsa_ref, dfsb_ref,
                 mlc_ref, mlsa_ref, mlsb_ref, out_ref, kmean_ref):
    xb = x_ref[...].astype(BF16)
    tm = xb.shape[0]
    lane = _lane_iota((tm, LANES))
    upper = lane >= HEAD_DIM

    def mm(c0, width):
        return _dot(xb, w1_ref[:, c0:c0 + width])

    def put(c0, val):
        out_ref[:, c0:c0 + val.shape[1]] = val.astype(BF16)

    def with_ones(v):
        return jnp.where(upper, 1.0, v)

    def head_tile(pair, odd):
        return pltpu.roll(pair, HEAD_DIM, axis=1) if odd else pair

    mb = mm(C_MB, 768)
    mbc, mbsa, mbsb = mbc_ref[...], mbsa_ref[...], mbsb_ref[...]
    per_step = tm // MOBA_BLOCK
    base_blk = lax.rem(pl.program_id(0) * per_step, n_blk)
    row_blk = jnp.right_shift(lax.broadcasted_iota(jnp.int32, (tm, LANES), 0), int(math.log2(MOBA_BLOCK)))
    onehot = lane == HEAD_DIM + base_blk + row_blk
    for t in range(2):
        c = slice(LANES * t, LANES * (t + 1))
        qp = _rope128(mb[:, c], mbc, mbsa, mbsb, MOBA_ROT // 2) * (HEAD_DIM ** -0.5 * LOG2E)
        kp = _rope128(mb[:, 256 + LANES * t:256 + LANES * (t + 1)], mbc, mbsa, mbsb, MOBA_ROT // 2)
        vp = mb[:, 512 + LANES * t:512 + LANES * (t + 1)]
        for odd in range(2):
            h = 2 * t + odd
            put(O_MBQ + LANES * h, jnp.where(upper, 0.0, head_tile(qp, odd)))
            k = jnp.where(upper, 0.0, head_tile(kp, odd))
            for r in range(per_step):
                kmean_ref[0, r:r + 1, LANES * h:LANES * (h + 1)] = jnp.mean(
                    k[r * MOBA_BLOCK:(r + 1) * MOBA_BLOCK], axis=0, keepdims=True)
            put(O_MBK + LANES * h, jnp.where(onehot, 1.0, k))
            put(O_MBV + LANES * h, with_ones(head_tile(vp, odd)))

    df = mm(C_DF, 768)
    dfc, dfsa, dfsb = dfc_ref[...], dfsa_ref[...], dfsb_ref[...]
    for t in range(2):
        c = slice(LANES * t, LANES * (t + 1))
        put(O_DFQ + LANES * t, _rope128(df[:, c], dfc, dfsa, dfsb, DIFF_ROT // 2) * (DIFF_HALF ** -0.5 * LOG2E))
        put(O_DFK + LANES * t, _rope128(df[:, 256 + LANES * t:256 + LANES * (t + 1)], dfc, dfsa, dfsb,
                                        DIFF_ROT // 2))
        for odd in range(2):
            put(O_DFV + LANES * (2 * t + odd),
                with_ones(head_tile(df[:, 512 + LANES * t:512 + LANES * (t + 1)], odd)))

    sb = mm(C_SB, 768)
    put(O_SBQ, sb[:, 0:256] * 0.125)
    put(O_SBK, sb[:, 256:768])

    lat = mm(C_LAT, 512)
    mlc, mlsa, mlsb = mlc_ref[...], mlsa_ref[...], mlsb_ref[...]
    ql = lat[:, 0:256]
    qn = ql * lax.rsqrt(jnp.sum(ql * ql, axis=-1, keepdims=True) * (1.0 / MLA_Q_LORA) + RMS_EPS) * gq_ref[...]
    qm = _dot(qn.astype(BF16), wuq_ref[...])
    kvl = lat[:, 256:384]
    kvn = kvl * lax.rsqrt(jnp.mean(kvl * kvl, axis=-1, keepdims=True) + RMS_EPS) * gkv_ref[...]
    kv = _dot(kvn.astype(BF16), wukv_ref[...])
    kpe = _rope128(lat[:, 384:512], mlc, mlsa, mlsb, MLA_ROPE // 2)
    mscale = (MLA_NOPE + MLA_ROPE) ** -0.5 * LOG2E
    for h in range(HEADS):
        c = slice(LANES * h, LANES * (h + 1))
        put(O_MLQ + LANES * h, _rope128(qm[:, c], mlc, mlsa, mlsb, MLA_ROPE // 2) * mscale)
        put(O_MLK + LANES * h, kv[:, c] + kpe)
        put(O_MLV + LANES * h, with_ones(kv[:, 512 + LANES * h:512 + LANES * (h + 1)]))


def _proj_call(x, w1, wuq, wukv, gq, gkv, tables, n_blk):
    t_tokens = x.shape[0]
    n_steps = t_tokens // PROJ_TM
    full = lambda shape: pl.BlockSpec(shape, lambda i: (0,) * len(shape))
    tab_spec = pl.BlockSpec((PROJ_TM, LANES), lambda i: (i, 0))
    return pl.pallas_call(
        functools.partial(_proj_kernel, n_blk),
        grid=(n_steps,),
        in_specs=[pl.BlockSpec((PROJ_TM, D_MODEL), lambda i: (i, 0)),
                  full((D_MODEL, PROJ_IN_COLS)), full((256, 512)), full((128, 1024)),
                  full((1, 256)), full((1, 128))] + [tab_spec] * 9,
        out_specs=[pl.BlockSpec((PROJ_TM, PROJ_OUT_COLS), lambda i: (i, 0)),
                   pl.BlockSpec((1, PROJ_TM // MOBA_BLOCK, 512), lambda i: (i, 0, 0))],
        out_shape=[jax.ShapeDtypeStruct((t_tokens, PROJ_OUT_COLS), BF16),
                   jax.ShapeDtypeStruct((n_steps, PROJ_TM // MOBA_BLOCK, 512), F32)],
        compiler_params=pltpu.CompilerParams(dimension_semantics=("parallel",),
                                             vmem_limit_bytes=VMEM_LIMIT_BYTES),
        name="proj",
    )(x, w1, wuq, wukv, gq, gkv, *tables)


def _causal_mask(tq, tk):
    return lax.broadcasted_iota(jnp.int32, (tq, tk), 1) <= lax.broadcasted_iota(jnp.int32, (tq, tk), 0)


def _kv_rows(j, tile):
    return pl.ds(pl.multiple_of(j * tile, tile), tile)


def _flash_update(s, v_tile, m_ref, acc_ref, idx):
    m_prev = m_ref[idx]
    m_new = jnp.maximum(m_prev, jnp.max(s, axis=-1, keepdims=True))
    p = jnp.concatenate([jnp.exp2(s[:, LANES * t:LANES * (t + 1)] - m_new) for t in range(s.shape[1] // LANES)],
                        axis=1)
    acc_ref[idx] = jnp.exp2(m_prev - m_new) * acc_ref[idx] + _dot(p.astype(BF16), v_tile)
    m_ref[idx] = m_new


def _flash_init(m_ref, acc_ref):
    m_ref[...] = jnp.full(m_ref.shape, NEG_INF, F32)
    acc_ref[...] = jnp.zeros(acc_ref.shape, F32)


def _normalized(acc):
    return acc / pltpu.roll(acc, HEAD_DIM, axis=1)


def _store_pairs(o_ref, outs):
    lane = _lane_iota(outs[0].shape)
    for t in range(2):
        pair = jnp.where(lane < HEAD_DIM, outs[2 * t], pltpu.roll(outs[2 * t + 1], HEAD_DIM, axis=1))
        o_ref[0, :, LANES * t:LANES * (t + 1)] = pair.astype(o_ref.dtype)


def _pipelined_sweep(qi, block_of, scores, update, buf_a, buf_b):
    scores(block_of(0), True, buf_a)

    def body(p, carry):
        t = 2 * p
        scores(block_of(t + 1), False, buf_b)
        update(buf_a, block_of(t))
        scores(block_of(t + 2), False, buf_a)
        update(buf_b, block_of(t + 1))
        return carry

    lax.fori_loop(0, jnp.right_shift(qi, 1), body, 0)
    odd = lax.rem(qi, 2)

    @pl.when(odd == 1)
    def _():
        scores(block_of(qi), False, buf_b)
        update(buf_a, block_of(qi - 1))
        update(buf_b, block_of(qi))

    @pl.when(odd == 0)
    def _():
        update(buf_a, block_of(qi))


def _diag_then_past(qi):
    return lambda step: jnp.where(step == 0, qi, step - 1)


def _mla_kernel(q_ref, k_ref, v_ref, o_ref, m_ref, acc_ref, sa_ref, sb_ref):
    qi = pl.program_id(1)
    diag = _causal_mask(FLASH_TILE, FLASH_TILE)
    _flash_init(m_ref, acc_ref)

    def scores(j, diagonal, buf):
        for h in range(HEADS):
            c = slice(LANES * h, LANES * (h + 1))
            s = _dot_nt(q_ref[0, :, c], k_ref[0, _kv_rows(j, FLASH_TILE), c])
            buf[h] = jnp.where(diag, s, NEG_INF) if diagonal else s

    def update(buf, j):
        for h in range(HEADS):
            _flash_update(buf[h], v_ref[0, _kv_rows(j, FLASH_TILE), LANES * h:LANES * (h + 1)], m_ref, acc_ref, h)

    _pipelined_sweep(qi, _diag_then_past(qi), scores, update, sa_ref, sb_ref)
    _store_pairs(o_ref, [_normalized(acc_ref[h]) for h in range(HEADS)])


def _moba_kernel(q_ref, k_ref, v_ref, kmean_ref, o_ref, m_ref, acc_ref, sa_ref, sb_ref, qx_ref):
    qi = pl.program_id(1)
    tq = FLASH_TILE
    n_blk = kmean_ref.shape[1]
    diag = _causal_mask(tq, FLASH_TILE)
    _flash_init(m_ref, acc_ref)
    blk = lax.broadcasted_iota(jnp.int32, (n_blk, tq), 0)
    own = (tq // MOBA_BLOCK) * qi + jnp.right_shift(lax.broadcasted_iota(jnp.int32, (n_blk, tq), 1),
                                                    int(math.log2(MOBA_BLOCK)))
    lane = _lane_iota((tq, LANES))
    slot = (lane >= HEAD_DIM) & (lane < HEAD_DIM + n_blk)
    place = jnp.where(lax.broadcasted_iota(jnp.int32, (n_blk, LANES), 1)
                      == HEAD_DIM + lax.broadcasted_iota(jnp.int32, (n_blk, LANES), 0), 1.0, 0.0).astype(BF16)
    for h in range(HEADS):
        c = slice(LANES * h, LANES * (h + 1))
        qh = q_ref[0, :, c]
        km_hi, km_lo = _split_bf16(kmean_ref[0, :, c])
        gate = _dot_nt(km_hi, qh) + _dot_nt(km_lo, qh)
        gate = jnp.where(blk < own, gate, NEG_INF)
        sel = blk == own
        for _ in range(MOBA_TOPK):
            mx = jnp.max(gate, axis=0, keepdims=True)
            first_idx = jnp.min(jnp.where(gate == mx, blk, n_blk), axis=0, keepdims=True)
            pick = (blk == first_idx) & (mx > NEG_INF)
            sel = sel | pick
            gate = jnp.where(pick, NEG_INF, gate)
        placed = lax.dot_general(jnp.where(sel, 1.0, 0.0).astype(BF16), place, (((0,), (0,)), ((), ())),
                                 preferred_element_type=F32)
        bias = ((placed - 1.0) * MOBA_MASK_BIAS).astype(BF16)
        qx_ref[h] = jnp.where(slot, bias, qh)

    def scores(j, diagonal, buf):
        for h in range(HEADS):
            s = _dot_nt(qx_ref[h], k_ref[0, _kv_rows(j, FLASH_TILE), LANES * h:LANES * (h + 1)])
            buf[h] = jnp.where(diag, s, NEG_INF) if diagonal else s

    def update(buf, j):
        for h in range(HEADS):
            _flash_update(buf[h], v_ref[0, _kv_rows(j, FLASH_TILE), LANES * h:LANES * (h + 1)], m_ref, acc_ref, h)

    _pipelined_sweep(qi, _diag_then_past(qi), scores, update, sa_ref, sb_ref)
    _store_pairs(o_ref, [_normalized(acc_ref[h]) for h in range(HEADS)])


def _diff_kernel(lambda_init, q_ref, k_ref, v_ref, lam_ref, g_ref, o_ref, m_ref, acc_ref, sa_ref, sb_ref, qs_ref):
    qi = pl.program_id(1)
    tq = FLASH_TILE
    diag = _causal_mask(tq, FLASH_TILE)
    _flash_init(m_ref, acc_ref)
    lane = _lane_iota((tq, LANES))
    zero = jnp.zeros((), BF16)
    for t in range(2):
        qt = q_ref[0, :, LANES * t:LANES * (t + 1)]
        for part in range(4):
            keep = (lane >= DIFF_HALF * part) & (lane < DIFF_HALF * (part + 1))
            qs_ref[t, part * tq:(part + 1) * tq, :] = jnp.where(keep, qt, zero)

    def scores(j, diagonal, buf):
        for t in range(2):
            s_all = _dot_nt(qs_ref[t], k_ref[0, _kv_rows(j, FLASH_TILE), LANES * t:LANES * (t + 1)])
            for part in range(4):
                s = s_all[part * tq:(part + 1) * tq]
                buf[4 * t + part] = jnp.where(diag, s, NEG_INF) if diagonal else s

    def update(buf, j):
        for idx in range(2 * HEADS):
            h = idx // 2
            _flash_update(buf[idx], v_ref[0, _kv_rows(j, FLASH_TILE), LANES * h:LANES * (h + 1)], m_ref, acc_ref, idx)

    _pipelined_sweep(qi, _diag_then_past(qi), scores, update, sa_ref, sb_ref)

    lp = lam_ref[...]
    lam = (jnp.exp(jnp.sum(lp[0:1] * lp[1:2], axis=-1, keepdims=True))
           - jnp.exp(jnp.sum(lp[2:3] * lp[3:4], axis=-1, keepdims=True)) + lambda_init)
    outs = []
    for h in range(HEADS):
        o = _normalized(acc_ref[2 * h]) - lam * _normalized(acc_ref[2 * h + 1])
        ms = jnp.sum(jnp.where(lane < HEAD_DIM, o * o, 0.0), axis=-1, keepdims=True) * (1.0 / HEAD_DIM)
        outs.append(o * lax.rsqrt(ms + RMS_EPS) * g_ref[...] * (1.0 - lambda_init))
    _store_pairs(o_ref, outs)


def _sb_kernel(q_ref, k_ref, v_ref, o_ref, cum_ref, acc_ref, za_ref, zb_ref, qs_ref):
    qi = pl.program_id(1)
    tq, tk = SB_TILE, SB_TILE
    lane = _lane_iota((tq, LANES))
    zero = jnp.zeros((), BF16)
    cum_ref[...] = jnp.zeros(cum_ref.shape, F32)
    acc_ref[...] = jnp.zeros(acc_ref.shape, F32)
    for t in range(2):
        qt = q_ref[0, :, LANES * t:LANES * (t + 1)]
        qs_ref[t, 0:tq, :] = jnp.where(lane < HEAD_DIM, qt, zero)
        qs_ref[t, tq:2 * tq, :] = jnp.where(lane >= HEAD_DIM, qt, zero)
    row = lax.broadcasted_iota(jnp.int32, (tk, tk), 0)
    col = lax.broadcasted_iota(jnp.int32, (tk, tk), 1)
    u_tri = jnp.where(row > col, 1.0, 0.0).astype(BF16)
    row2 = lax.broadcasted_iota(jnp.int32, (2 * tq, tk), 0)
    col2 = lax.broadcasted_iota(jnp.int32, (2 * tq, tk), 1)
    past = col2 < jnp.where(row2 >= tq, row2 - tq, row2)

    def scores(j, diagonal, buf):
        for t in range(2):
            z = _dot_nt(qs_ref[t], k_ref[0, _kv_rows(j, tk), LANES * t:LANES * (t + 1)])
            buf[t] = jnp.where(past, z, NEG_INF) if diagonal else z

    def update(buf, j):
        for t in range(2):
            z = buf[t]
            sp = jnp.maximum(z, 0.0) + jnp.log(1.0 + jnp.exp(-jnp.abs(z)))
            later = _dot(sp.astype(BF16), u_tri)
            cum = cum_ref[t]
            a = jnp.concatenate(
                [jnp.exp(z[:, LANES * n:LANES * (n + 1)] - sp[:, LANES * n:LANES * (n + 1)]
                         - later[:, LANES * n:LANES * (n + 1)] - cum) for n in range(tk // LANES)], axis=1)
            acc_ref[t] += _dot(a.astype(BF16), v_ref[0, _kv_rows(j, tk), LANES * t:LANES * (t + 1)])
            cum_ref[t] = cum + jnp.sum(sp, axis=-1, keepdims=True)

    _pipelined_sweep(qi, lambda step: qi - step, scores, update, za_ref, zb_ref)
    for t in range(2):
        acc = acc_ref[t]
        o_ref[0, :, LANES * t:LANES * (t + 1)] = jnp.where(lane < HEAD_DIM, acc[0:tq], acc[tq:2 * tq]).astype(o_ref.dtype)


def _attn_call(kernel, name, tile, proj3, q_off, k_off, v_off, q_width, k_width, v_width, scratch,
               extra_in=(), extra_specs=()):
    bsz, seq, _ = proj3.shape
    return pl.pallas_call(
        kernel,
        grid=(bsz, seq // tile),
        in_specs=[pl.BlockSpec((1, tile, q_width), lambda b, i: (b, i, q_off // q_width)),
                  pl.BlockSpec((1, seq, k_width), lambda b, i: (b, 0, k_off // k_width)),
                  pl.BlockSpec((1, seq, v_width), lambda b, i: (b, 0, v_off // v_width))] + list(extra_specs),
        out_specs=pl.BlockSpec((1, tile, GROUP_WIDTH), lambda b, i: (b, i, 0)),
        out_shape=jax.ShapeDtypeStruct((bsz, seq, GROUP_WIDTH), BF16),
        scratch_shapes=scratch,
        compiler_params=pltpu.CompilerParams(dimension_semantics=("parallel", "parallel"),
                                             vmem_limit_bytes=VMEM_LIMIT_BYTES),
        name=name,
    )(proj3, proj3, proj3, *extra_in)


def _state(n, rows):
    return pltpu.VMEM((n, rows, LANES), F32)


def _staging(n, rows, tk):
    return [pltpu.VMEM((n, rows, tk), F32), pltpu.VMEM((n, rows, tk), F32)]


def _memkv_kernel(mem_ref, wk_ref, wv_ref, k_ref, v_ref):
    mb = mem_ref[...].astype(BF16)
    k_ref[...] = _dot(mb, wk_ref[...]).astype(BF16)
    v_ref[...] = _dot(mb, wv_ref[...]).astype(BF16)


def _memkv_call(mem2, wk, wv):
    rows = mem2.shape[0]
    full = pl.BlockSpec((D_MODEL, D_MODEL), lambda i: (0, 0))
    blk = pl.BlockSpec((N_MEM, D_MODEL), lambda i: (i, 0))
    return pl.pallas_call(
        _memkv_kernel,
        grid=(rows // N_MEM,),
        in_specs=[blk, full, full],
        out_specs=[blk, blk],
        out_shape=[jax.ShapeDtypeStruct((rows, D_MODEL), BF16)] * 2,
        compiler_params=pltpu.CompilerParams(dimension_semantics=("parallel",),
                                             vmem_limit_bytes=VMEM_LIMIT_BYTES),
        name="memkv",
    )(mem2, wk, wv)


def _post_kernel(x_ref, oa_ref, ob_ref, oc_ref, od_ref, wout_ref, g1_ref, b1_ref,
                 wq_ref, km_ref, vm_ref, wo_ref, g2_ref, b2_ref, wrh_ref, wrl_ref, br_ref,
                 x2r_ref, counts_ref, oh_ref, cnt_ref):
    @pl.when(pl.program_id(0) == 0)
    def _():
        cnt_ref[...] = jnp.zeros_like(cnt_ref)

    for r0 in range(0, POST_TM, POST_CHUNK):
        _post_rows(slice(r0, r0 + POST_CHUNK), x_ref, oa_ref, ob_ref, oc_ref, od_ref, wout_ref, g1_ref, b1_ref,
                   wq_ref, km_ref, vm_ref, wo_ref, g2_ref, b2_ref, wrh_ref, wrl_ref, br_ref, x2r_ref, oh_ref, cnt_ref)
    counts_ref[...] = cnt_ref[...]


def _post_rows(rows, x_ref, oa_ref, ob_ref, oc_ref, od_ref, wout_ref, g1_ref, b1_ref,
               wq_ref, km_ref, vm_ref, wo_ref, g2_ref, b2_ref, wrh_ref, wrl_ref, br_ref, x2r_ref, oh_ref, cnt_ref):
    mix = (_dot(oa_ref[rows, :], wout_ref[0:256, :]) + _dot(ob_ref[rows, :], wout_ref[256:512, :])
           + _dot(oc_ref[rows, :], wout_ref[512:768, :]) + _dot(od_ref[rows, :], wout_ref[768:1024, :]))
    x1 = _layer_norm(DEEPNORM_ALPHA * x_ref[rows, :] + mix, g1_ref[...], b1_ref[...])

    q = (_dot(x1.astype(BF16), wq_ref[...]) * (XATTN_HEAD_DIM ** -0.5)).astype(BF16)
    for h in range(XATTN_HEADS):
        c = slice(XATTN_HEAD_DIM * h, XATTN_HEAD_DIM * (h + 1))
        s = _dot_nt(q[:, c], km_ref[0, :, c])
        p = jnp.exp(s - jnp.max(s, axis=-1, keepdims=True))
        o = _dot(p.astype(BF16), vm_ref[0, :, c]) / jnp.sum(p, axis=-1, keepdims=True)
        oh_ref[rows, c] = o.astype(BF16)
    xa = _dot(oh_ref[rows, :], wo_ref[...])
    x2 = _layer_norm(DEEPNORM_ALPHA * x1 + xa, g2_ref[...], b2_ref[...])
    x2r_ref[rows, 0:D_MODEL] = x2

    hi, lo = _split_bf16(x2)
    logits = _dot(hi, wrh_ref[...]) + _dot(hi, wrl_ref[...]) + _dot(lo, wrh_ref[...]) + br_ref[...]
    lane = _lane_iota(logits.shape)
    gl = jnp.where(lane < N_GROUPS, logits, NEG_INF)
    gmx = jnp.max(gl, axis=-1, keepdims=True)
    gidx = jnp.min(jnp.where(gl == gmx, lane, ROUTER_LANES), axis=-1, keepdims=True)
    gw = 1.0 / jnp.sum(jnp.exp(gl - gmx), axis=-1, keepdims=True)
    e0 = N_GROUPS + EXPERTS_PER_GROUP * gidx
    el = jnp.where((lane >= e0) & (lane < e0 + EXPERTS_PER_GROUP), logits, NEG_INF)
    m1 = jnp.max(el, axis=-1, keepdims=True)
    i1 = jnp.min(jnp.where(el == m1, lane, ROUTER_LANES), axis=-1, keepdims=True)
    el = jnp.where(lane == i1, NEG_INF, el)
    m2 = jnp.max(el, axis=-1, keepdims=True)
    i2 = jnp.min(jnp.where(el == m2, lane, ROUTER_LANES), axis=-1, keepdims=True)
    e = jnp.exp(m2 - m1)
    w1 = gw / (1.0 + e)
    w2 = gw * e / (1.0 + e)

    n = logits.shape[0]
    onehot = jnp.where(lane == gidx, 1.0, 0.0)
    earlier = (lax.broadcasted_iota(jnp.int32, (n, n), 1) < lax.broadcasted_iota(jnp.int32, (n, n), 0))
    before = _dot(jnp.where(earlier, 1.0, 0.0).astype(BF16), onehot.astype(BF16)) + cnt_ref[...]
    rank = jnp.sum(onehot * before, axis=-1, keepdims=True)
    cnt_ref[...] += jnp.sum(onehot, axis=0, keepdims=True)
    x2r_ref[rows, D_MODEL:] = (jnp.where(lane == i1, w1, 0.0) + jnp.where(lane == i2, w2, 0.0)
                               + jnp.where(lane == 0, gidx.astype(F32), 0.0) + jnp.where(lane == 1, rank, 0.0))


def _post_call(x, oa, ob, oc, od, wout, g1, b1, wq, kmem, vmem, wo, g2, b2, wrh, wrl, br, seq):
    t_tokens = x.shape[0]
    tm = POST_TM
    per_seq = seq // tm
    row = lambda w: pl.BlockSpec((tm, w), lambda i: (i, 0))
    full = lambda shape: pl.BlockSpec(shape, lambda i: (0,) * len(shape))
    mem_spec = pl.BlockSpec((1, N_MEM, D_MODEL), lambda i: (i // per_seq, 0, 0))
    sq = (D_MODEL, D_MODEL)
    return pl.pallas_call(
        _post_kernel,
        grid=(t_tokens // tm,),
        in_specs=[row(D_MODEL), row(256), row(256), row(256), row(256),
                  full(sq), full((1, D_MODEL)), full((1, D_MODEL)),
                  full(sq), mem_spec, mem_spec, full(sq), full((1, D_MODEL)), full((1, D_MODEL)),
                  full((D_MODEL, ROUTER_LANES)), full((D_MODEL, ROUTER_LANES)), full((1, ROUTER_LANES))],
        out_specs=[row(MOE_ROW_WIDTH), full((1, ROUTER_LANES))],
        out_shape=[jax.ShapeDtypeStruct((t_tokens, MOE_ROW_WIDTH), F32),
                   jax.ShapeDtypeStruct((1, ROUTER_LANES), F32)],
        scratch_shapes=[pltpu.VMEM((tm, D_MODEL), BF16), pltpu.VMEM((1, ROUTER_LANES), F32)],
        compiler_params=pltpu.CompilerParams(dimension_semantics=("arbitrary",),
                                             vmem_limit_bytes=VMEM_LIMIT_BYTES),
        name="post",
    )(x, oa, ob, oc, od, wout, g1, b1, wq, kmem, vmem, wo, g2, b2, wrh, wrl, br)


def _row_dma_wait(src_ref, dst_ref, sem, rows):
    pltpu.make_async_copy(src_ref.at[pl.ds(0, rows)], dst_ref.at[pl.ds(0, rows)], sem).wait()


def _issue_row_dmas(copy_of_row, sems):
    for r in range(MOE_ROWS_PER_STEP):
        copy_of_row(r, sems[r % 2]).start(priority=r % 2)


def _gather_kernel(pos_ref, x_ref, xs_init_ref, xs_ref, sem0, sem1):
    del xs_init_ref
    _issue_row_dmas(lambda r, sem: pltpu.make_async_copy(
        x_ref.at[pl.ds(r, 1)], xs_ref.at[pl.ds(pos_ref[0, 0, r], 1)], sem), (sem0, sem1))
    for sem in (sem0, sem1):
        _row_dma_wait(x_ref, xs_ref, sem, MOE_ROWS_PER_STEP // 2)


def _scatter_kernel(pos_ref, ys_ref, y_ref, sem0, sem1):
    _issue_row_dmas(lambda r, sem: pltpu.make_async_copy(
        ys_ref.at[pl.ds(pos_ref[0, 0, r], 1)], y_ref.at[pl.ds(r, 1)], sem), (sem0, sem1))
    for sem in (sem0, sem1):
        _row_dma_wait(ys_ref, y_ref, sem, MOE_ROWS_PER_STEP // 2)


def _pos_spec():
    return pl.BlockSpec((1, 1, MOE_ROWS_PER_STEP), lambda i: (i, 0, 0), memory_space=pltpu.SMEM)


def _token_rows_spec(width):
    return pl.BlockSpec((MOE_ROWS_PER_STEP, width), lambda i: (i, 0))


def _gather_call(pos3, x2r, sorted_buf):
    n_steps = pos3.shape[0]
    any_spec = pl.BlockSpec(memory_space=pl.ANY)
    return pl.pallas_call(
        _gather_kernel,
        grid=(n_steps,),
        in_specs=[_pos_spec(), _token_rows_spec(x2r.shape[1]), any_spec],
        out_specs=any_spec,
        out_shape=jax.ShapeDtypeStruct(sorted_buf.shape, sorted_buf.dtype),
        input_output_aliases={2: 0},
        scratch_shapes=[pltpu.SemaphoreType.DMA(()), pltpu.SemaphoreType.DMA(())],
        compiler_params=pltpu.CompilerParams(dimension_semantics=("arbitrary",), has_side_effects=True),
        name="moe_gather",
    )(pos3, x2r, sorted_buf)


def _scatter_call(pos3, ys, t_tokens):
    n_steps = pos3.shape[0]
    any_spec = pl.BlockSpec(memory_space=pl.ANY)
    return pl.pallas_call(
        _scatter_kernel,
        grid=(n_steps,),
        in_specs=[_pos_spec(), any_spec],
        out_specs=_token_rows_spec(ys.shape[1]),
        out_shape=jax.ShapeDtypeStruct((t_tokens, ys.shape[1]), ys.dtype),
        scratch_shapes=[pltpu.SemaphoreType.DMA(()), pltpu.SemaphoreType.DMA(())],
        compiler_params=pltpu.CompilerParams(dimension_semantics=("arbitrary",)),
        name="moe_scatter",
    )(pos3, ys)


def _moe_kernel(tile_grp_ref, n_used_ref, xs_ref, wg_ref, wu_ref, wd_ref, g_ref, b_ref, o_ref):
    i = pl.program_id(0)

    @pl.when(i < n_used_ref[0])
    def _():
        grp = tile_grp_ref[i]
        x2 = xs_ref[:, 0:D_MODEL]
        xb = x2.astype(BF16)
        route = xs_ref[:, D_MODEL:]
        lane = _lane_iota(route.shape)
        acc = jnp.zeros(x2.shape, F32)
        for j in range(EXPERTS_PER_GROUP):
            e = N_GROUPS + EXPERTS_PER_GROUP * grp + j
            gate_col = jnp.sum(jnp.where(lane == e, route, 0.0), axis=-1, keepdims=True)
            gg = _dot(xb, wg_ref[j])
            uu = _dot(xb, wu_ref[j])
            hid = gg * (1.0 / (1.0 + jnp.exp(-gg))) * uu * gate_col
            acc = acc + _dot(hid.astype(BF16), wd_ref[j])
        o_ref[...] = _layer_norm(DEEPNORM_ALPHA * x2 + acc, g_ref[...], b_ref[...])

    @pl.when(i >= n_used_ref[0])
    def _():
        o_ref[...] = jnp.zeros_like(o_ref)


def _moe_call(tile_grp, n_used, xs, wg, wu, wd, g, b):
    cap_rows = xs.shape[0]
    tm = MOE_TM
    row = lambda w: pl.BlockSpec((tm, w), lambda i, tg, nu: (i, 0))
    vec = pl.BlockSpec((1, D_MODEL), lambda i, tg, nu: (0, 0))
    wspec = lambda shape: pl.BlockSpec(shape, lambda i, tg, nu: (tg[i], 0, 0))
    return pl.pallas_call(
        _moe_kernel,
        grid_spec=pltpu.PrefetchScalarGridSpec(
            num_scalar_prefetch=2,
            grid=(cap_rows // tm,),
            in_specs=[row(MOE_ROW_WIDTH),
                      wspec((EXPERTS_PER_GROUP, D_MODEL, EXPERT_FF)),
                      wspec((EXPERTS_PER_GROUP, D_MODEL, EXPERT_FF)),
                      wspec((EXPERTS_PER_GROUP, EXPERT_FF, D_MODEL)),
                      vec, vec],
            out_specs=row(D_MODEL)),
        out_shape=jax.ShapeDtypeStruct((cap_rows, D_MODEL), F32),
        compiler_params=pltpu.CompilerParams(dimension_semantics=("arbitrary",),
                                             vmem_limit_bytes=VMEM_LIMIT_BYTES),
        name="moe",
    )(tile_grp, n_used, xs, wg, wu, wd, g, b)


def _moe_capacity_rows(t_tokens):
    return (t_tokens // MOE_TM + N_GROUPS) * MOE_TM


def _moe_layer(x2r, counts, sorted_buf, wg, wu, wd, g, b):
    t_tokens = x2r.shape[0]
    tm = MOE_TM
    n_tiles = sorted_buf.shape[0] // tm
    gid = x2r[:, D_MODEL].astype(jnp.int32)
    rank = x2r[:, D_MODEL + 1].astype(jnp.int32)
    cnt = counts[0, :N_GROUPS].astype(jnp.int32)
    tiles_per_grp = (cnt + tm - 1) // tm
    tile_end = jnp.cumsum(tiles_per_grp)
    pos = ((tile_end - tiles_per_grp) * tm)[gid] + rank
    tile_ids = jnp.arange(n_tiles, dtype=jnp.int32)
    tile_grp = jnp.minimum(jnp.sum((tile_ids[:, None] >= tile_end[None, :]).astype(jnp.int32), axis=1),
                           N_GROUPS - 1)
    n_used = tile_end[-1:].astype(jnp.int32)
    pos3 = pos.reshape(t_tokens // MOE_ROWS_PER_STEP, 1, MOE_ROWS_PER_STEP)

    xs = _gather_call(pos3, x2r, sorted_buf)
    ys = _moe_call(tile_grp, n_used, xs, wg, wu, wd, g, b)
    return _scatter_call(pos3, ys, t_tokens), xs


def _rope_lane_tables(positions, rot, period, offset):
    half = rot // 2
    inv = ROPE_THETA ** (-jnp.arange(0, rot, 2, dtype=F32) / rot)
    ang = positions.astype(F32).reshape(-1)[:, None] * inv
    cos, sin = jnp.cos(ang), jnp.sin(ang)
    p = jnp.arange(LANES) % period - offset
    first = (p >= 0) & (p < half)
    second = (p >= half) & (p < rot)
    idx = jnp.clip(jnp.where(second, p - half, p), 0, half - 1)
    cg, sg = cos[:, idx], sin[:, idx]
    c = jnp.where(first | second, cg, 1.0)
    sa = jnp.where(first, -sg, 0.0)
    sb = jnp.where(second, sg, 0.0)
    return c, sa, sb


def _prep_proj_weights(w_in, w_uq, w_ukv, q_norm, kv_norm):
    q_lat, kv_lat, k_rope = w_in[:, 0:192], w_in[:, 192:320], w_in[:, 320:352]
    sb, mb, df = w_in[:, 352:1120], w_in[:, 1120:1888], w_in[:, 1888:2656]
    zeros = lambda n: jnp.zeros((D_MODEL, n), w_in.dtype)

    def head_tiles(w):
        w = w.reshape(w.shape[0], HEADS, HEAD_DIM)
        return jnp.pad(w, ((0, 0), (0, 0), (0, LANES - HEAD_DIM))).reshape(w.shape[0], HEADS * LANES)

    w1 = jnp.concatenate([mb, df, q_lat, zeros(64), kv_lat, zeros(64), k_rope, zeros(32), sb], axis=1)
    wuq = w_uq.reshape(MLA_Q_LORA, HEADS, MLA_NOPE + MLA_ROPE)
    wuq = jnp.pad(wuq, ((0, 256 - MLA_Q_LORA), (0, 0), (0, LANES - MLA_NOPE - MLA_ROPE))).reshape(256, HEADS * LANES)
    wukv = w_ukv.reshape(MLA_KV_LORA, HEADS, 2, HEAD_DIM)
    wk = head_tiles(wukv[:, :, 0].reshape(MLA_KV_LORA, HEADS * HEAD_DIM))
    wv = head_tiles(wukv[:, :, 1].reshape(MLA_KV_LORA, HEADS * HEAD_DIM))
    gq = jnp.pad(q_norm, (0, 256 - MLA_Q_LORA)).reshape(1, 256)
    return (w1.astype(BF16), wuq.astype(BF16), jnp.concatenate([wk, wv], axis=1).astype(BF16),
            gq, kv_norm.reshape(1, MLA_KV_LORA))


def kernel(x, mem, positions, w_in, mla_q_norm, w_uq, mla_kv_norm, w_ukv, diff_lambda, diff_subln,
           w_out, ln_mix_g, ln_mix_b, xattn_wq, xattn_wk, xattn_wv, xattn_wo, ln_mem_g, ln_mem_b,
           router_group_w, router_group_b, router_expert_w, router_expert_b,
           expert_w_gate, expert_w_up, expert_w_down, ln_ffn_g, ln_ffn_b):
    bsz, seq, _ = x.shape
    t_tokens = bsz * seq
    n_blk = seq // MOBA_BLOCK
    assert seq % PROJ_TM == 0 and seq % POST_TM == 0 and seq % FLASH_TILE == 0 and seq % SB_TILE == 0
    assert t_tokens % MOE_TM == 0 and t_tokens % MOE_ROWS_PER_STEP == 0
    assert mem.shape[1] == N_MEM and n_blk <= LANES - HEAD_DIM

    tables = (_rope_lane_tables(positions, MOBA_ROT, HEAD_DIM, 0)
              + _rope_lane_tables(positions, DIFF_ROT, DIFF_HALF, 0)
              + _rope_lane_tables(positions, MLA_ROPE, LANES, MLA_NOPE))
    mem2 = mem.reshape(bsz * N_MEM, D_MODEL)
    row = lambda v: v.reshape(1, -1)
    xf = x.reshape(t_tokens, D_MODEL)
    sorted_buf = jnp.zeros((_moe_capacity_rows(t_tokens), MOE_ROW_WIDTH), F32)

    for l in range(DEPTH):
        lambda_init = 0.8 - 0.6 * math.exp(-0.3 * l)
        w1, wuq, wukv, gq, gkv = _prep_proj_weights(w_in[l], w_uq[l], w_ukv[l], mla_q_norm[l], mla_kv_norm[l])
        proj, kmean = _proj_call(xf, w1, wuq, wukv, gq, gkv, tables, n_blk)
        proj3 = proj.reshape(bsz, seq, PROJ_OUT_COLS)
        kmean3 = kmean.reshape(bsz, n_blk, HEADS * LANES)

        ft, st = FLASH_TILE, SB_TILE
        o_mla = _attn_call(_mla_kernel, "mla_attn", ft, proj3, O_MLQ, O_MLK, O_MLV, 512, 512, 512,
                           [_state(HEADS, ft), _state(HEADS, ft)] + _staging(HEADS, ft, ft))
        o_sb = _attn_call(_sb_kernel, "sb_attn", st, proj3, O_SBQ, O_SBK, O_SBV, 256, 256, 256,
                          [_state(2, 2 * st), _state(2, 2 * st)]
                          + _staging(2, 2 * st, st) + [pltpu.VMEM((2, 2 * st, LANES), BF16)])
        o_mb = _attn_call(_moba_kernel, "moba_attn", ft, proj3, O_MBQ, O_MBK, O_MBV, 512, 512, 512,
                          [_state(HEADS, ft), _state(HEADS, ft)] + _staging(HEADS, ft, ft)
                          + [pltpu.VMEM((HEADS, ft, LANES), BF16)],
                          extra_in=(kmean3,),
                          extra_specs=(pl.BlockSpec((1, n_blk, HEADS * LANES), lambda b, i: (b, 0, 0)),))
        subln = jnp.tile(diff_subln[l], LANES // HEAD_DIM).reshape(1, LANES)
        o_df = _attn_call(functools.partial(_diff_kernel, lambda_init), "diff_attn", ft, proj3,
                          O_DFQ, O_DFK, O_DFV, 256, 256, 512,
                          [_state(2 * HEADS, ft), _state(2 * HEADS, ft)] + _staging(2 * HEADS, ft, ft)
                          + [pltpu.VMEM((2, 4 * ft, LANES), BF16)],
                          extra_in=(diff_lambda[l], subln),
                          extra_specs=(pl.BlockSpec((4, DIFF_HALF), lambda b, i: (0, 0)),
                                       pl.BlockSpec((1, LANES), lambda b, i: (0, 0))))

        kmem, vmem = _memkv_call(mem2, xattn_wk[l].astype(BF16), xattn_wv[l].astype(BF16))
        wr = jnp.concatenate([router_group_w[l], router_expert_w[l],
                              jnp.zeros((D_MODEL, ROUTER_LANES - N_GROUPS - N_EXPERTS), F32)], axis=1)
        wr_hi = wr.astype(BF16)
        wr_lo = (wr - wr_hi.astype(F32)).astype(BF16)
        br = jnp.concatenate([router_group_b[l], router_expert_b[l],
                              jnp.zeros((ROUTER_LANES - N_GROUPS - N_EXPERTS,), F32)]).reshape(1, ROUTER_LANES)
        flat = lambda o: o.reshape(t_tokens, GROUP_WIDTH)
        x2r, counts = _post_call(
            xf, flat(o_mla), flat(o_sb), flat(o_mb), flat(o_df), w_out[l].astype(BF16),
            row(ln_mix_g[l]), row(ln_mix_b[l]), xattn_wq[l].astype(BF16),
            kmem.reshape(bsz, N_MEM, D_MODEL), vmem.reshape(bsz, N_MEM, D_MODEL), xattn_wo[l].astype(BF16),
            row(ln_mem_g[l]), row(ln_mem_b[l]), wr_hi, wr_lo, br, seq)

        xf, sorted_buf = _moe_layer(x2r, counts, sorted_buf, expert_w_gate[l].astype(BF16),
                                    expert_w_up[l].astype(BF16), expert_w_down[l].astype(BF16),
                                    row(ln_ffn_g[l]), row(ln_ffn_b[l]))
    return xf.reshape(bsz, seq, D_MODEL)
```

```python
import functools
import math

import jax
import jax.numpy as jnp
from jax import lax
from jax.experimental import pallas as pl
from jax.experimental.pallas import tpu as pltpu

F32 = jnp.float32
BF16 = jnp.bfloat16
NEG_INF = float("-inf")

D_MODEL = 1024
DEPTH = 4
N_MEM = 256
HEAD_DIM = 64
GROUP_WIDTH = 256
HEADS = 4
ROPE_THETA = 500000.0
MLA_Q_LORA = 192
MLA_KV_LORA = 128
MLA_NOPE = 64
MLA_ROPE = 32
MOBA_BLOCK = 256
MOBA_TOPK = 3
MOBA_ROT = 16
DIFF_HALF = 32
DIFF_ROT = 8
XATTN_HEADS = 4
XATTN_HEAD_DIM = 256
N_GROUPS = 8
EXPERTS_PER_GROUP = 4
N_EXPERTS = 32
EXPERT_FF = 256
DEEPNORM_ALPHA = (2 * DEPTH) ** 0.25
LN_EPS = 1e-5
RMS_EPS = 1e-6

LANES = 128
VMEM_LIMIT_BYTES = 56 * 1024 * 1024
PROJ_TM = 512
POST_TM = 1024
POST_CHUNK = 512
SB_TQ = 512
SB_TK = 256
FLASH_TILE = 512
MOE_TM = 512
MOE_ROWS_PER_STEP = 512

C_MB, C_DF, C_LAT, C_SB = 0, 768, 1536, 2048
PROJ_IN_COLS = 2816
O_MBQ, O_MBK, O_MBV, O_DFV, O_MLQ, O_MLK, O_MLV = 0, 512, 1024, 1536, 2048, 2560, 3072
O_SBQ, O_SBK, O_SBV, O_DFQ, O_DFK = 3584, 3840, 4096, 4352, 4608
PROJ_OUT_COLS = 4864
ROUTER_LANES = 128
MOE_ROW_WIDTH = D_MODEL + ROUTER_LANES
LOG2E = 1.4426950408889634
MOBA_MASK_BIAS = 2.0 ** 100


def _dot(a, b):
    return jnp.dot(a, b, preferred_element_type=F32)


def _dot_nt(a, b):
    return lax.dot_general(a, b, (((1,), (1,)), ((), ())), preferred_element_type=F32)


def _split_bf16(x):
    hi = x.astype(BF16)
    lo = (x - hi.astype(F32)).astype(BF16)
    return hi, lo


def _layer_norm(x, g, b):
    mu = jnp.mean(x, axis=-1, keepdims=True)
    xc = x - mu
    var = jnp.mean(xc * xc, axis=-1, keepdims=True)
    return xc * lax.rsqrt(var + LN_EPS) * g + b


def _lane_iota(shape):
    return lax.broadcasted_iota(jnp.int32, shape, 1)


def _rope128(t, c, sa, sb, half):
    nxt = pltpu.roll(t, LANES - half, axis=1)
    prv = pltpu.roll(t, half, axis=1)
    return t * c + nxt * sa + prv * sb


def _proj_kernel(n_blk, x_ref, w1_ref, wuq_ref, wukv_ref, gq_ref, gkv_ref,
                 mbc_ref, mbsa_ref, mbsb_ref, dfc_ref, dfsa_ref, dfsb_ref,
                 mlc_ref, mlsa_ref, mlsb_ref, out_ref, kmean_ref):
    xb = x_ref[...].astype(BF16)
    tm = xb.shape[0]
    lane = _lane_iota((tm, LANES))
    upper = lane >= HEAD_DIM

    def mm(c0, width):
        return _dot(xb, w1_ref[:, c0:c0 + width])

    def put(c0, val):
        out_ref[:, c0:c0 + val.shape[1]] = val.astype(BF16)

    def with_ones(v):
        return jnp.where(upper, 1.0, v)

    def head_tile(pair, odd):
        return pltpu.roll(pair, HEAD_DIM, axis=1) if odd else pair

    mb = mm(C_MB, 768)
    mbc, mbsa, mbsb = mbc_ref[...], mbsa_ref[...], mbsb_ref[...]
    per_step = tm // MOBA_BLOCK
    base_blk = lax.rem(pl.program_id(0) * per_step, n_blk)
    row_blk = jnp.right_shift(lax.broadcasted_iota(jnp.int32, (tm, LANES), 0), int(math.log2(MOBA_BLOCK)))
    onehot = lane == HEAD_DIM + base_blk + row_blk
    for t in range(2):
        c = slice(LANES * t, LANES * (t + 1))
        qp = _rope128(mb[:, c], mbc, mbsa, mbsb, MOBA_ROT // 2) * (HEAD_DIM ** -0.5 * LOG2E)
        kp = _rope128(mb[:, 256 + LANES * t:256 + LANES * (t + 1)], mbc, mbsa, mbsb, MOBA_ROT // 2)
        vp = mb[:, 512 + LANES * t:512 + LANES * (t + 1)]
        for odd in range(2):
            h = 2 * t + odd
            put(O_MBQ + LANES * h, jnp.where(upper, 0.0, head_tile(qp, odd)))
            k = jnp.where(upper, 0.0, head_tile(kp, odd))
            for r in range(per_step):
                kmean_ref[0, r:r + 1, LANES * h:LANES * (h + 1)] = jnp.mean(
                    k[r * MOBA_BLOCK:(r + 1) * MOBA_BLOCK], axis=0, keepdims=True)
            put(O_MBK + LANES * h, jnp.where(onehot, 1.0, k))
            put(O_MBV + LANES * h, with_ones(head_tile(vp, odd)))

    df = mm(C_DF, 768)
    dfc, dfsa, dfsb = dfc_ref[...], dfsa_ref[...], dfsb_ref[...]
    for t in range(2):
        c = slice(LANES * t, LANES * (t + 1))
        put(O_DFQ + LANES * t, _rope128(df[:, c], dfc, dfsa, dfsb, DIFF_ROT // 2) * (DIFF_HALF ** -0.5 * LOG2E))
        put(O_DFK + LANES * t, _rope128(df[:, 256 + LANES * t:256 + LANES * (t + 1)], dfc, dfsa, dfsb,
                                        DIFF_ROT // 2))
        for odd in range(2):
            put(O_DFV + LANES * (2 * t + odd),
                with_ones(head_tile(df[:, 512 + LANES * t:512 + LANES * (t + 1)], odd)))

    sb = mm(C_SB, 768)
    put(O_SBQ, sb[:, 0:256] * 0.125)
    put(O_SBK, sb[:, 256:768])

    lat = mm(C_LAT, 512)
    mlc, mlsa, mlsb = mlc_ref[...], mlsa_ref[...], mlsb_ref[...]
    ql = lat[:, 0:256]
    qn = ql * lax.rsqrt(jnp.sum(ql * ql, axis=-1, keepdims=True) * (1.0 / MLA_Q_LORA) + RMS_EPS) * gq_ref[...]
    qm = _dot(qn.astype(BF16), wuq_ref[...])
    kvl = lat[:, 256:384]
    kvn = kvl * lax.rsqrt(jnp.mean(kvl * kvl, axis=-1, keepdims=True) + RMS_EPS) * gkv_ref[...]
    kv = _dot(kvn.astype(BF16), wukv_ref[...])
    kpe = _rope128(lat[:, 384:512], mlc, mlsa, mlsb, MLA_ROPE // 2)
    mscale = (MLA_NOPE + MLA_ROPE) ** -0.5 * LOG2E
    for h in range(HEADS):
        c = slice(LANES * h, LANES * (h + 1))
        put(O_MLQ + LANES * h, _rope128(qm[:, c], mlc, mlsa, mlsb, MLA_ROPE // 2) * mscale)
        put(O_MLK + LANES * h, kv[:, c] + kpe)
        put(O_MLV + LANES * h, with_ones(kv[:, 512 + LANES * h:512 + LANES * (h + 1)]))


def _proj_call(x, w1, wuq, wukv, gq, gkv, tables, n_blk):
    t_tokens = x.shape[0]
    n_steps = t_tokens // PROJ_TM
    full = lambda shape: pl.BlockSpec(shape, lambda i: (0,) * len(shape))
    tab_spec = pl.BlockSpec((PROJ_TM, LANES), lambda i: (i, 0))
    return pl.pallas_call(
        functools.partial(_proj_kernel, n_blk),
        grid=(n_steps,),
        in_specs=[pl.BlockSpec((PROJ_TM, D_MODEL), lambda i: (i, 0)),
                  full((D_MODEL, PROJ_IN_COLS)), full((256, 512)), full((128, 1024)),
                  full((1, 256)), full((1, 128))] + [tab_spec] * 9,
        out_specs=[pl.BlockSpec((PROJ_TM, PROJ_OUT_COLS), lambda i: (i, 0)),
                   pl.BlockSpec((1, PROJ_TM // MOBA_BLOCK, 512), lambda i: (i, 0, 0))],
        out_shape=[jax.ShapeDtypeStruct((t_tokens, PROJ_OUT_COLS), BF16),
                   jax.ShapeDtypeStruct((n_steps, PROJ_TM // MOBA_BLOCK, 512), F32)],
        compiler_params=pltpu.CompilerParams(dimension_semantics=("parallel",),
                                             vmem_limit_bytes=VMEM_LIMIT_BYTES),
        name="proj",
    )(x, w1, wuq, wukv, gq, gkv, *tables)


def _causal_mask(tq, tk):
    return lax.broadcasted_iota(jnp.int32, (tq, tk), 1) <= lax.broadcasted_iota(jnp.int32, (tq, tk), 0)


def _kv_rows(j, tile):
    return pl.ds(pl.multiple_of(j * tile, tile), tile)


def _flash_update(s, v_tile, m_ref, acc_ref, idx):
    m_prev = m_ref[idx]
    m_new = jnp.maximum(m_prev, jnp.max(s, axis=-1, keepdims=True))
    p = jnp.concatenate([jnp.exp2(s[:, LANES * t:LANES * (t + 1)] - m_new) for t in range(s.shape[1] // LANES)],
                        axis=1)
    acc_ref[idx] = jnp.exp2(m_prev - m_new) * acc_ref[idx] + _dot(p.astype(BF16), v_tile)
    m_ref[idx] = m_new


def _flash_init(m_ref, acc_ref):
    m_ref[...] = jnp.full(m_ref.shape, NEG_INF, F32)
    acc_ref[...] = jnp.zeros(acc_ref.shape, F32)


def _normalized(acc):
    return acc / pltpu.roll(acc, HEAD_DIM, axis=1)


def _store_pairs(o_ref, outs):
    lane = _lane_iota(outs[0].shape)
    for t in range(2):
        pair = jnp.where(lane < HEAD_DIM, outs[2 * t], pltpu.roll(outs[2 * t + 1], HEAD_DIM, axis=1))
        o_ref[0, :, LANES * t:LANES * (t + 1)] = pair.astype(o_ref.dtype)


def _pipelined_sweep(qi, block_of, scores, update, buf_a, buf_b):
    scores(block_of(0), True, buf_a)

    def body(p, carry):
        t = 2 * p
        scores(block_of(t + 1), False, buf_b)
        update(buf_a, block_of(t))
        scores(block_of(t + 2), False, buf_a)
        update(buf_b, block_of(t + 1))
        return carry

    lax.fori_loop(0, jnp.right_shift(qi, 1), body, 0)
    odd = lax.rem(qi, 2)

    @pl.when(odd == 1)
    def _():
        scores(block_of(qi), False, buf_b)
        update(buf_a, block_of(qi - 1))
        update(buf_b, block_of(qi))

    @pl.when(odd == 0)
    def _():
        update(buf_a, block_of(qi))


def _diag_then_past(qi):
    return lambda step: jnp.where(step == 0, qi, step - 1)


def _mla_kernel(q_ref, k_ref, v_ref, o_ref, m_ref, acc_ref, sa_ref, sb_ref):
    qi = pl.program_id(1)
    diag = _causal_mask(FLASH_TILE, FLASH_TILE)
    _flash_init(m_ref, acc_ref)

    def scores(j, diagonal, buf):
        for h in range(HEADS):
            c = slice(LANES * h, LANES * (h + 1))
            s = _dot_nt(q_ref[0, :, c], k_ref[0, _kv_rows(j, FLASH_TILE), c])
            buf[h] = jnp.where(diag, s, NEG_INF) if diagonal else s

    def update(buf, j):
        for h in range(HEADS):
            _flash_update(buf[h], v_ref[0, _kv_rows(j, FLASH_TILE), LANES * h:LANES * (h + 1)], m_ref, acc_ref, h)

    _pipelined_sweep(qi, _diag_then_past(qi), scores, update, sa_ref, sb_ref)
    _store_pairs(o_ref, [_normalized(acc_ref[h]) for h in range(HEADS)])


def _moba_kernel(q_ref, k_ref, v_ref, kmean_ref, o_ref, m_ref, acc_ref, sa_ref, sb_ref, qx_ref):
    qi = pl.program_id(1)
    tq = FLASH_TILE
    n_blk = kmean_ref.shape[1]
    diag = _causal_mask(tq, FLASH_TILE)
    _flash_init(m_ref, acc_ref)
    blk = lax.broadcasted_iota(jnp.int32, (n_blk, tq), 0)
    own = (tq // MOBA_BLOCK) * qi + jnp.right_shift(lax.broadcasted_iota(jnp.int32, (n_blk, tq), 1),
                                                    int(math.log2(MOBA_BLOCK)))
    lane = _lane_iota((tq, LANES))
    slot = (lane >= HEAD_DIM) & (lane < HEAD_DIM + n_blk)
    place = jnp.where(lax.broadcasted_iota(jnp.int32, (n_blk, LANES), 1)
                      == HEAD_DIM + lax.broadcasted_iota(jnp.int32, (n_blk, LANES), 0), 1.0, 0.0).astype(BF16)
    for h in range(HEADS):
        c = slice(LANES * h, LANES * (h + 1))
        qh = q_ref[0, :, c]
        km_hi, km_lo = _split_bf16(kmean_ref[0, :, c])
        gate = _dot_nt(km_hi, qh) + _dot_nt(km_lo, qh)
        gate = jnp.where(blk < own, gate, NEG_INF)
        sel = blk == own
        for _ in range(MOBA_TOPK):
            mx = jnp.max(gate, axis=0, keepdims=True)
            first_idx = jnp.min(jnp.where(gate == mx, blk, n_blk), axis=0, keepdims=True)
            pick = (blk == first_idx) & (mx > NEG_INF)
            sel = sel | pick
            gate = jnp.where(pick, NEG_INF, gate)
        placed = lax.dot_general(jnp.where(sel, 1.0, 0.0).astype(BF16), place, (((0,), (0,)), ((), ())),
                                 preferred_element_type=F32)
        bias = ((placed - 1.0) * MOBA_MASK_BIAS).astype(BF16)
        qx_ref[h] = jnp.where(slot, bias, qh)

    def scores(j, diagonal, buf):
        for h in range(HEADS):
            s = _dot_nt(qx_ref[h], k_ref[0, _kv_rows(j, FLASH_TILE), LANES * h:LANES * (h + 1)])
            buf[h] = jnp.where(diag, s, NEG_INF) if diagonal else s

    def update(buf, j):
        for h in range(HEADS):
            _flash_update(buf[h], v_ref[0, _kv_rows(j, FLASH_TILE), LANES * h:LANES * (h + 1)], m_ref, acc_ref, h)

    _pipelined_sweep(qi, _diag_then_past(qi), scores, update, sa_ref, sb_ref)
    _store_pairs(o_ref, [_normalized(acc_ref[h]) for h in range(HEADS)])


def _diff_kernel(lambda_init, q_ref, k_ref, v_ref, lam_ref, g_ref, o_ref, m_ref, acc_ref, sa_ref, sb_ref, qs_ref):
    qi = pl.program_id(1)
    tq = FLASH_TILE
    diag = _causal_mask(tq, FLASH_TILE)
    _flash_init(m_ref, acc_ref)
    lane = _lane_iota((tq, LANES))
    zero = jnp.zeros((), BF16)
    for t in range(2):
        qt = q_ref[0, :, LANES * t:LANES * (t + 1)]
        for part in range(4):
            keep = (lane >= DIFF_HALF * part) & (lane < DIFF_HALF * (part + 1))
            qs_ref[t, part * tq:(part + 1) * tq, :] = jnp.where(keep, qt, zero)

    def scores(j, diagonal, buf):
        for t in range(2):
            s_all = _dot_nt(qs_ref[t], k_ref[0, _kv_rows(j, FLASH_TILE), LANES * t:LANES * (t + 1)])
            for part in range(4):
                s = s_all[part * tq:(part + 1) * tq]
                buf[4 * t + part] = jnp.where(diag, s, NEG_INF) if diagonal else s

    def update(buf, j):
        for idx in range(2 * HEADS):
            h = idx // 2
            _flash_update(buf[idx], v_ref[0, _kv_rows(j, FLASH_TILE), LANES * h:LANES * (h + 1)], m_ref, acc_ref, idx)

    _pipelined_sweep(qi, _diag_then_past(qi), scores, update, sa_ref, sb_ref)

    lp = lam_ref[...]
    lam = (jnp.exp(jnp.sum(lp[0:1] * lp[1:2], axis=-1, keepdims=True))
           - jnp.exp(jnp.sum(lp[2:3] * lp[3:4], axis=-1, keepdims=True)) + lambda_init)
    outs = []
    for h in range(HEADS):
        o = _normalized(acc_ref[2 * h]) - lam * _normalized(acc_ref[2 * h + 1])
        ms = jnp.sum(jnp.where(lane < HEAD_DIM, o * o, 0.0), axis=-1, keepdims=True) * (1.0 / HEAD_DIM)
        outs.append(o * lax.rsqrt(ms + RMS_EPS) * g_ref[...] * (1.0 - lambda_init))
    _store_pairs(o_ref, outs)


def _sb_kernel(q_ref, k_ref, v_ref, o_ref, cum_ref, acc_ref, za_ref, zb_ref, qs_ref):
    qi = pl.program_id(1)
    tq, tk = SB_TQ, SB_TK
    lane = _lane_iota((tq, LANES))
    zero = jnp.zeros((), BF16)
    cum_ref[...] = jnp.zeros(cum_ref.shape, F32)
    acc_ref[...] = jnp.zeros(acc_ref.shape, F32)
    for t in range(2):
        qt = q_ref[0, :, LANES * t:LANES * (t + 1)]
        qs_ref[t, 0:tq, :] = jnp.where(lane < HEAD_DIM, qt, zero)
        qs_ref[t, tq:2 * tq, :] = jnp.where(lane >= HEAD_DIM, qt, zero)
    row = lax.broadcasted_iota(jnp.int32, (tk, tk), 0)
    col = lax.broadcasted_iota(jnp.int32, (tk, tk), 1)
    u_tri = jnp.where(row > col, 1.0, 0.0).astype(BF16)
    row2 = lax.broadcasted_iota(jnp.int32, (2 * tq, tk), 0)
    q_pos = qi * tq + jnp.where(row2 >= tq, row2 - tq, row2)
    k_off = lax.broadcasted_iota(jnp.int32, (2 * tq, tk), 1)
    n_blocks = (tq // tk) * (qi + 1)

    def scores(step, overlaps_tile, buf):
        j = n_blocks - 1 - step
        for t in range(2):
            z = _dot_nt(qs_ref[t], k_ref[0, _kv_rows(j, tk), LANES * t:LANES * (t + 1)])
            buf[t] = jnp.where(j * tk + k_off < q_pos, z, NEG_INF) if overlaps_tile else z

    def update(buf, step):
        j = n_blocks - 1 - step
        for t in range(2):
            z = buf[t]
            sp = jnp.maximum(z, 0.0) + jnp.log(1.0 + jnp.exp(-jnp.abs(z)))
            later = _dot(sp.astype(BF16), u_tri)
            cum = cum_ref[t]
            a = jnp.concatenate(
                [jnp.exp(z[:, LANES * n:LANES * (n + 1)] - sp[:, LANES * n:LANES * (n + 1)]
                         - later[:, LANES * n:LANES * (n + 1)] - cum) for n in range(tk // LANES)], axis=1)
            acc_ref[t] += _dot(a.astype(BF16), v_ref[0, _kv_rows(j, tk), LANES * t:LANES * (t + 1)])
            cum_ref[t] = cum + jnp.sum(sp, axis=-1, keepdims=True)

    scores(0, True, za_ref)
    scores(1, True, zb_ref)
    update(za_ref, 0)

    def body(i, carry):
        step = 2 * i + 1
        scores(step + 1, False, za_ref)
        update(zb_ref, step)
        scores(step + 2, False, zb_ref)
        update(za_ref, step + 1)
        return carry

    lax.fori_loop(0, qi, body, 0)
    update(zb_ref, n_blocks - 1)
    for t in range(2):
        acc = acc_ref[t]
        o_ref[0, :, LANES * t:LANES * (t + 1)] = jnp.where(lane < HEAD_DIM, acc[0:tq], acc[tq:2 * tq]).astype(o_ref.dtype)


def _attn_call(kernel, name, tile, proj3, q_off, k_off, v_off, q_width, k_width, v_width, scratch,
               extra_in=(), extra_specs=()):
    bsz, seq, _ = proj3.shape
    return pl.pallas_call(
        kernel,
        grid=(bsz, seq // tile),
        in_specs=[pl.BlockSpec((1, tile, q_width), lambda b, i: (b, i, q_off // q_width)),
                  pl.BlockSpec((1, seq, k_width), lambda b, i: (b, 0, k_off // k_width)),
                  pl.BlockSpec((1, seq, v_width), lambda b, i: (b, 0, v_off // v_width))] + list(extra_specs),
        out_specs=pl.BlockSpec((1, tile, GROUP_WIDTH), lambda b, i: (b, i, 0)),
        out_shape=jax.ShapeDtypeStruct((bsz, seq, GROUP_WIDTH), BF16),
        scratch_shapes=scratch,
        compiler_params=pltpu.CompilerParams(dimension_semantics=("parallel", "parallel"),
                                             vmem_limit_bytes=VMEM_LIMIT_BYTES),
        name=name,
    )(proj3, proj3, proj3, *extra_in)


def _state(n, rows):
    return pltpu.VMEM((n, rows, LANES), F32)


def _staging(n, rows, tk):
    return [pltpu.VMEM((n, rows, tk), F32), pltpu.VMEM((n, rows, tk), F32)]


def _memkv_kernel(mem_ref, wk_ref, wv_ref, k_ref, v_ref):
    mb = mem_ref[...].astype(BF16)
    k_ref[...] = _dot(mb, wk_ref[...]).astype(BF16)
    v_ref[...] = _dot(mb, wv_ref[...]).astype(BF16)


def _memkv_call(mem2, wk, wv):
    rows = mem2.shape[0]
    full = pl.BlockSpec((D_MODEL, D_MODEL), lambda i: (0, 0))
    blk = pl.BlockSpec((N_MEM, D_MODEL), lambda i: (i, 0))
    return pl.pallas_call(
        _memkv_kernel,
        grid=(rows // N_MEM,),
        in_specs=[blk, full, full],
        out_specs=[blk, blk],
        out_shape=[jax.ShapeDtypeStruct((rows, D_MODEL), BF16)] * 2,
        compiler_params=pltpu.CompilerParams(dimension_semantics=("parallel",),
                                             vmem_limit_bytes=VMEM_LIMIT_BYTES),
        name="memkv",
    )(mem2, wk, wv)


def _post_kernel(x_ref, oa_ref, ob_ref, oc_ref, od_ref, wout_ref, g1_ref, b1_ref,
                 wq_ref, km_ref, vm_ref, wo_ref, g2_ref, b2_ref, wrh_ref, wrl_ref, br_ref,
                 x2r_ref, counts_ref, oh_ref, cnt_ref):
    @pl.when(pl.program_id(0) == 0)
    def _():
        cnt_ref[...] = jnp.zeros_like(cnt_ref)

    for r0 in range(0, POST_TM, POST_CHUNK):
        _post_rows(slice(r0, r0 + POST_CHUNK), x_ref, oa_ref, ob_ref, oc_ref, od_ref, wout_ref, g1_ref, b1_ref,
                   wq_ref, km_ref, vm_ref, wo_ref, g2_ref, b2_ref, wrh_ref, wrl_ref, br_ref, x2r_ref, oh_ref, cnt_ref)
    counts_ref[...] = cnt_ref[...]


def _post_rows(rows, x_ref, oa_ref, ob_ref, oc_ref, od_ref, wout_ref, g1_ref, b1_ref,
               wq_ref, km_ref, vm_ref, wo_ref, g2_ref, b2_ref, wrh_ref, wrl_ref, br_ref, x2r_ref, oh_ref, cnt_ref):
    mix = (_dot(oa_ref[rows, :], wout_ref[0:256, :]) + _dot(ob_ref[rows, :], wout_ref[256:512, :])
           + _dot(oc_ref[rows, :], wout_ref[512:768, :]) + _dot(od_ref[rows, :], wout_ref[768:1024, :]))
    x1 = _layer_norm(DEEPNORM_ALPHA * x_ref[rows, :] + mix, g1_ref[...], b1_ref[...])

    q = (_dot(x1.astype(BF16), wq_ref[...]) * (XATTN_HEAD_DIM ** -0.5)).astype(BF16)
    for h in range(XATTN_HEADS):
        c = slice(XATTN_HEAD_DIM * h, XATTN_HEAD_DIM * (h + 1))
        s = _dot_nt(q[:, c], km_ref[0, :, c])
        p = jnp.exp(s - jnp.max(s, axis=-1, keepdims=True))
        o = _dot(p.astype(BF16), vm_ref[0, :, c]) / jnp.sum(p, axis=-1, keepdims=True)
        oh_ref[rows, c] = o.astype(BF16)
    xa = _dot(oh_ref[rows, :], wo_ref[...])
    x2 = _layer_norm(DEEPNORM_ALPHA * x1 + xa, g2_ref[...], b2_ref[...])
    x2r_ref[rows, 0:D_MODEL] = x2

    hi, lo = _split_bf16(x2)
    logits = _dot(hi, wrh_ref[...]) + _dot(hi, wrl_ref[...]) + _dot(lo, wrh_ref[...]) + br_ref[...]
    lane = _lane_iota(logits.shape)
    gl = jnp.where(lane < N_GROUPS, logits, NEG_INF)
    gmx = jnp.max(gl, axis=-1, keepdims=True)
    gidx = jnp.min(jnp.where(gl == gmx, lane, ROUTER_LANES), axis=-1, keepdims=True)
    gw = 1.0 / jnp.sum(jnp.exp(gl - gmx), axis=-1, keepdims=True)
    e0 = N_GROUPS + EXPERTS_PER_GROUP * gidx
    el = jnp.where((lane >= e0) & (lane < e0 + EXPERTS_PER_GROUP), logits, NEG_INF)
    m1 = jnp.max(el, axis=-1, keepdims=True)
    i1 = jnp.min(jnp.where(el == m1, lane, ROUTER_LANES), axis=-1, keepdims=True)
    el = jnp.where(lane == i1, NEG_INF, el)
    m2 = jnp.max(el, axis=-1, keepdims=True)
    i2 = jnp.min(jnp.where(el == m2, lane, ROUTER_LANES), axis=-1, keepdims=True)
    e = jnp.exp(m2 - m1)
    w1 = gw / (1.0 + e)
    w2 = gw * e / (1.0 + e)

    n = logits.shape[0]
    onehot = jnp.where(lane == gidx, 1.0, 0.0)
    earlier = (lax.broadcasted_iota(jnp.int32, (n, n), 1) < lax.broadcasted_iota(jnp.int32, (n, n), 0))
    before = _dot(jnp.where(earlier, 1.0, 0.0).astype(BF16), onehot.astype(BF16)) + cnt_ref[...]
    rank = jnp.sum(onehot * before, axis=-1, keepdims=True)
    cnt_ref[...] += jnp.sum(onehot, axis=0, keepdims=True)
    x2r_ref[rows, D_MODEL:] = (jnp.where(lane == i1, w1, 0.0) + jnp.where(lane == i2, w2, 0.0)
                               + jnp.where(lane == 0, gidx.astype(F32), 0.0) + jnp.where(lane == 1, rank, 0.0))


def _post_call(x, oa, ob, oc, od, wout, g1, b1, wq, kmem, vmem, wo, g2, b2, wrh, wrl, br, seq):
    t_tokens = x.shape[0]
    tm = POST_TM
    per_seq = seq // tm
    row = lambda w: pl.BlockSpec((tm, w), lambda i: (i, 0))
    full = lambda shape: pl.BlockSpec(shape, lambda i: (0,) * len(shape))
    mem_spec = pl.BlockSpec((1, N_MEM, D_MODEL), lambda i: (i // per_seq, 0, 0))
    sq = (D_MODEL, D_MODEL)
    return pl.pallas_call(
        _post_kernel,
        grid=(t_tokens // tm,),
        in_specs=[row(D_MODEL), row(256), row(256), row(256), row(256),
                  full(sq), full((1, D_MODEL)), full((1, D_MODEL)),
                  full(sq), mem_spec, mem_spec, full(sq), full((1, D_MODEL)), full((1, D_MODEL)),
                  full((D_MODEL, ROUTER_LANES)), full((D_MODEL, ROUTER_LANES)), full((1, ROUTER_LANES))],
        out_specs=[row(MOE_ROW_WIDTH), full((1, ROUTER_LANES))],
        out_shape=[jax.ShapeDtypeStruct((t_tokens, MOE_ROW_WIDTH), F32),
                   jax.ShapeDtypeStruct((1, ROUTER_LANES), F32)],
        scratch_shapes=[pltpu.VMEM((tm, D_MODEL), BF16), pltpu.VMEM((1, ROUTER_LANES), F32)],
        compiler_params=pltpu.CompilerParams(dimension_semantics=("arbitrary",),
                                             vmem_limit_bytes=VMEM_LIMIT_BYTES),
        name="post",
    )(x, oa, ob, oc, od, wout, g1, b1, wq, kmem, vmem, wo, g2, b2, wrh, wrl, br)


def _row_dma_wait(src_ref, dst_ref, sem, rows):
    pltpu.make_async_copy(src_ref.at[pl.ds(0, rows)], dst_ref.at[pl.ds(0, rows)], sem).wait()


def _issue_row_dmas(copy_of_row, sems):
    for r in range(MOE_ROWS_PER_STEP):
        copy_of_row(r, sems[r % 2]).start(priority=r % 2)


def _gather_kernel(pos_ref, x_ref, xs_init_ref, xs_ref, sem0, sem1):
    del xs_init_ref
    _issue_row_dmas(lambda r, sem: pltpu.make_async_copy(
        x_ref.at[pl.ds(r, 1)], xs_ref.at[pl.ds(pos_ref[0, 0, r], 1)], sem), (sem0, sem1))
    for sem in (sem0, sem1):
        _row_dma_wait(x_ref, xs_ref, sem, MOE_ROWS_PER_STEP // 2)


def _scatter_kernel(pos_ref, ys_ref, y_ref, sem0, sem1):
    _issue_row_dmas(lambda r, sem: pltpu.make_async_copy(
        ys_ref.at[pl.ds(pos_ref[0, 0, r], 1)], y_ref.at[pl.ds(r, 1)], sem), (sem0, sem1))
    for sem in (sem0, sem1):
        _row_dma_wait(ys_ref, y_ref, sem, MOE_ROWS_PER_STEP // 2)


def _pos_spec():
    return pl.BlockSpec((1, 1, MOE_ROWS_PER_STEP), lambda i: (i, 0, 0), memory_space=pltpu.SMEM)


def _token_rows_spec(width):
    return pl.BlockSpec((MOE_ROWS_PER_STEP, width), lambda i: (i, 0))


def _gather_call(pos3, x2r, sorted_buf):
    n_steps = pos3.shape[0]
    any_spec = pl.BlockSpec(memory_space=pl.ANY)
    return pl.pallas_call(
        _gather_kernel,
        grid=(n_steps,),
        in_specs=[_pos_spec(), _token_rows_spec(x2r.shape[1]), any_spec],
        out_specs=any_spec,
        out_shape=jax.ShapeDtypeStruct(sorted_buf.shape, sorted_buf.dtype),
        input_output_aliases={2: 0},
        scratch_shapes=[pltpu.SemaphoreType.DMA(()), pltpu.SemaphoreType.DMA(())],
        compiler_params=pltpu.CompilerParams(dimension_semantics=("arbitrary",), has_side_effects=True),
        name="moe_gather",
    )(pos3, x2r, sorted_buf)


def _scatter_call(pos3, ys, t_tokens):
    n_steps = pos3.shape[0]
    any_spec = pl.BlockSpec(memory_space=pl.ANY)
    return pl.pallas_call(
        _scatter_kernel,
        grid=(n_steps,),
        in_specs=[_pos_spec(), any_spec],
        out_specs=_token_rows_spec(ys.shape[1]),
        out_shape=jax.ShapeDtypeStruct((t_tokens, ys.shape[1]), ys.dtype),
        scratch_shapes=[pltpu.SemaphoreType.DMA(()), pltpu.SemaphoreType.DMA(())],
        compiler_params=pltpu.CompilerParams(dimension_semantics=("arbitrary",)),
        name="moe_scatter",
    )(pos3, ys)


def _moe_kernel(tile_grp_ref, n_used_ref, xs_ref, wg_ref, wu_ref, wd_ref, g_ref, b_ref, o_ref):
    i = pl.program_id(0)

    @pl.when(i < n_used_ref[0])
    def _():
        grp = tile_grp_ref[i]
        x2 = xs_ref[:, 0:D_MODEL]
        xb = x2.astype(BF16)
        route = xs_ref[:, D_MODEL:]
        lane = _lane_iota(route.shape)
        acc = jnp.zeros(x2.shape, F32)
        for j in range(EXPERTS_PER_GROUP):
            e = N_GROUPS + EXPERTS_PER_GROUP * grp + j
            gate_col = jnp.sum(jnp.where(lane == e, route, 0.0), axis=-1, keepdims=True)
            gg = _dot(xb, wg_ref[j])
            uu = _dot(xb, wu_ref[j])
            hid = gg * (1.0 / (1.0 + jnp.exp(-gg))) * uu * gate_col
            acc = acc + _dot(hid.astype(BF16), wd_ref[j])
        o_ref[...] = _layer_norm(DEEPNORM_ALPHA * x2 + acc, g_ref[...], b_ref[...])

    @pl.when(i >= n_used_ref[0])
    def _():
        o_ref[...] = jnp.zeros_like(o_ref)


def _moe_call(tile_grp, n_used, xs, wg, wu, wd, g, b):
    cap_rows = xs.shape[0]
    tm = MOE_TM
    row = lambda w: pl.BlockSpec((tm, w), lambda i, tg, nu: (i, 0))
    vec = pl.BlockSpec((1, D_MODEL), lambda i, tg, nu: (0, 0))
    wspec = lambda shape: pl.BlockSpec(shape, lambda i, tg, nu: (tg[i], 0, 0))
    return pl.pallas_call(
        _moe_kernel,
        grid_spec=pltpu.PrefetchScalarGridSpec(
            num_scalar_prefetch=2,
            grid=(cap_rows // tm,),
            in_specs=[row(MOE_ROW_WIDTH),
                      wspec((EXPERTS_PER_GROUP, D_MODEL, EXPERT_FF)),
                      wspec((EXPERTS_PER_GROUP, D_MODEL, EXPERT_FF)),
                      wspec((EXPERTS_PER_GROUP, EXPERT_FF, D_MODEL)),
                      vec, vec],
            out_specs=row(D_MODEL)),
        out_shape=jax.ShapeDtypeStruct((cap_rows, D_MODEL), F32),
        compiler_params=pltpu.CompilerParams(dimension_semantics=("arbitrary",),
                                             vmem_limit_bytes=VMEM_LIMIT_BYTES),
        name="moe",
    )(tile_grp, n_used, xs, wg, wu, wd, g, b)


def _moe_capacity_rows(t_tokens):
    return (t_tokens // MOE_TM + N_GROUPS) * MOE_TM


def _moe_layer(x2r, counts, sorted_buf, wg, wu, wd, g, b):
    t_tokens = x2r.shape[0]
    tm = MOE_TM
    n_tiles = sorted_buf.shape[0] // tm
    gid = x2r[:, D_MODEL].astype(jnp.int32)
    rank = x2r[:, D_MODEL + 1].astype(jnp.int32)
    cnt = counts[0, :N_GROUPS].astype(jnp.int32)
    tiles_per_grp = (cnt + tm - 1) // tm
    tile_end = jnp.cumsum(tiles_per_grp)
    pos = ((tile_end - tiles_per_grp) * tm)[gid] + rank
    tile_ids = jnp.arange(n_tiles, dtype=jnp.int32)
    tile_grp = jnp.minimum(jnp.sum((tile_ids[:, None] >= tile_end[None, :]).astype(jnp.int32), axis=1),
                           N_GROUPS - 1)
    n_used = tile_end[-1:].astype(jnp.int32)
    pos3 = pos.reshape(t_tokens // MOE_ROWS_PER_STEP, 1, MOE_ROWS_PER_STEP)

    xs = _gather_call(pos3, x2r, sorted_buf)
    ys = _moe_call(tile_grp, n_used, xs, wg, wu, wd, g, b)
    return _scatter_call(pos3, ys, t_tokens), xs


def _rope_lane_tables(positions, rot, period, offset):
    half = rot // 2
    inv = ROPE_THETA ** (-jnp.arange(0, rot, 2, dtype=F32) / rot)
    ang = positions.astype(F32).reshape(-1)[:, None] * inv
    cos, sin = jnp.cos(ang), jnp.sin(ang)
    p = jnp.arange(LANES) % period - offset
    first = (p >= 0) & (p < half)
    second = (p >= half) & (p < rot)
    idx = jnp.clip(jnp.where(second, p - half, p), 0, half - 1)
    cg, sg = cos[:, idx], sin[:, idx]
    c = jnp.where(first | second, cg, 1.0)
    sa = jnp.where(first, -sg, 0.0)
    sb = jnp.where(second, sg, 0.0)
    return c, sa, sb


def _prep_proj_weights(w_in, w_uq, w_ukv, q_norm, kv_norm):
    q_lat, kv_lat, k_rope = w_in[:, 0:192], w_in[:, 192:320], w_in[:, 320:352]
    sb, mb, df = w_in[:, 352:1120], w_in[:, 1120:1888], w_in[:, 1888:2656]
    zeros = lambda n: jnp.zeros((D_MODEL, n), w_in.dtype)

    def head_tiles(w):
        w = w.reshape(w.shape[0], HEADS, HEAD_DIM)
        return jnp.pad(w, ((0, 0), (0, 0), (0, LANES - HEAD_DIM))).reshape(w.shape[0], HEADS * LANES)

    w1 = jnp.concatenate([mb, df, q_lat, zeros(64), kv_lat, zeros(64), k_rope, zeros(32), sb], axis=1)
    wuq = w_uq.reshape(MLA_Q_LORA, HEADS, MLA_NOPE + MLA_ROPE)
    wuq = jnp.pad(wuq, ((0, 256 - MLA_Q_LORA), (0, 0), (0, LANES - MLA_NOPE - MLA_ROPE))).reshape(256, HEADS * LANES)
    wukv = w_ukv.reshape(MLA_KV_LORA, HEADS, 2, HEAD_DIM)
    wk = head_tiles(wukv[:, :, 0].reshape(MLA_KV_LORA, HEADS * HEAD_DIM))
    wv = head_tiles(wukv[:, :, 1].reshape(MLA_KV_LORA, HEADS * HEAD_DIM))
    gq = jnp.pad(q_norm, (0, 256 - MLA_Q_LORA)).reshape(1, 256)
    return (w1.astype(BF16), wuq.astype(BF16), jnp.concatenate([wk, wv], axis=1).astype(BF16),
            gq, kv_norm.reshape(1, MLA_KV_LORA))


def kernel(x, mem, positions, w_in, mla_q_norm, w_uq, mla_kv_norm, w_ukv, diff_lambda, diff_subln,
           w_out, ln_mix_g, ln_mix_b, xattn_wq, xattn_wk, xattn_wv, xattn_wo, ln_mem_g, ln_mem_b,
           router_group_w, router_group_b, router_expert_w, router_expert_b,
           expert_w_gate, expert_w_up, expert_w_down, ln_ffn_g, ln_ffn_b):
    bsz, seq, _ = x.shape
    t_tokens = bsz * seq
    n_blk = seq // MOBA_BLOCK
    assert seq % PROJ_TM == 0 and seq % POST_TM == 0 and seq % FLASH_TILE == 0 and seq % SB_TQ == 0 and SB_TQ == 2 * SB_TK
    assert t_tokens % MOE_TM == 0 and t_tokens % MOE_ROWS_PER_STEP == 0
    assert mem.shape[1] == N_MEM and n_blk <= LANES - HEAD_DIM

    tables = (_rope_lane_tables(positions, MOBA_ROT, HEAD_DIM, 0)
              + _rope_lane_tables(positions, DIFF_ROT, DIFF_HALF, 0)
              + _rope_lane_tables(positions, MLA_ROPE, LANES, MLA_NOPE))
    mem2 = mem.reshape(bsz * N_MEM, D_MODEL)
    row = lambda v: v.reshape(1, -1)
    xf = x.reshape(t_tokens, D_MODEL)
    sorted_buf = jnp.zeros((_moe_capacity_rows(t_tokens), MOE_ROW_WIDTH), F32)

    for l in range(DEPTH):
        lambda_init = 0.8 - 0.6 * math.exp(-0.3 * l)
        w1, wuq, wukv, gq, gkv = _prep_proj_weights(w_in[l], w_uq[l], w_ukv[l], mla_q_norm[l], mla_kv_norm[l])
        proj, kmean = _proj_call(xf, w1, wuq, wukv, gq, gkv, tables, n_blk)
        proj3 = proj.reshape(bsz, seq, PROJ_OUT_COLS)
        kmean3 = kmean.reshape(bsz, n_blk, HEADS * LANES)

        ft, st = FLASH_TILE, SB_TQ
        o_mla = _attn_call(_mla_kernel, "mla_attn", ft, proj3, O_MLQ, O_MLK, O_MLV, 512, 512, 512,
                           [_state(HEADS, ft), _state(HEADS, ft)] + _staging(HEADS, ft, ft))
        o_sb = _attn_call(_sb_kernel, "sb_attn", st, proj3, O_SBQ, O_SBK, O_SBV, 256, 256, 256,
                          [_state(2, 2 * st), _state(2, 2 * st)]
                          + _staging(2, 2 * st, SB_TK) + [pltpu.VMEM((2, 2 * st, LANES), BF16)])
        o_mb = _attn_call(_moba_kernel, "moba_attn", ft, proj3, O_MBQ, O_MBK, O_MBV, 512, 512, 512,
                          [_state(HEADS, ft), _state(HEADS, ft)] + _staging(HEADS, ft, ft)
                          + [pltpu.VMEM((HEADS, ft, LANES), BF16)],
                          extra_in=(kmean3,),
                          extra_specs=(pl.BlockSpec((1, n_blk, HEADS * LANES), lambda b, i: (b, 0, 0)),))
        subln = jnp.tile(diff_subln[l], LANES // HEAD_DIM).reshape(1, LANES)
        o_df = _attn_call(functools.partial(_diff_kernel, lambda_init), "diff_attn", ft, proj3,
                          O_DFQ, O_DFK, O_DFV, 256, 256, 512,
                          [_state(2 * HEADS, ft), _state(2 * HEADS, ft)] + _staging(2 * HEADS, ft, ft)
                          + [pltpu.VMEM((2, 4 * ft, LANES), BF16)],
                          extra_in=(diff_lambda[l], subln),
                          extra_specs=(pl.BlockSpec((4, DIFF_HALF), lambda b, i: (0, 0)),
                                       pl.BlockSpec((1, LANES), lambda b, i: (0, 0))))

        kmem, vmem = _memkv_call(mem2, xattn_wk[l].astype(BF16), xattn_wv[l].astype(BF16))
        wr = jnp.concatenate([router_group_w[l], router_expert_w[l],
                              jnp.zeros((D_MODEL, ROUTER_LANES - N_GROUPS - N_EXPERTS), F32)], axis=1)
        wr_hi = wr.astype(BF16)
        wr_lo = (wr - wr_hi.astype(F32)).astype(BF16)
        br = jnp.concatenate([router_group_b[l], router_expert_b[l],
                              jnp.zeros((ROUTER_LANES - N_GROUPS - N_EXPERTS,), F32)]).reshape(1, ROUTER_LANES)
        flat = lambda o: o.reshape(t_tokens, GROUP_WIDTH)
        x2r, counts = _post_call(
            xf, flat(o_mla), flat(o_sb), flat(o_mb), flat(o_df), w_out[l].astype(BF16),
            row(ln_mix_g[l]), row(ln_mix_b[l]), xattn_wq[l].astype(BF16),
            kmem.reshape(bsz, N_MEM, D_MODEL), vmem.reshape(bsz, N_MEM, D_MODEL), xattn_wo[l].astype(BF16),
            row(ln_mem_g[l]), row(ln_mem_b[l]), wr_hi, wr_lo, br, seq)

        xf, sorted_buf = _moe_layer(x2r, counts, sorted_buf, expert_w_gate[l].astype(BF16),
                                    expert_w_up[l].astype(BF16), expert_w_down[l].astype(BF16),
                                    row(ln_ffn_g[l]), row(ln_ffn_b[l]))
    return xf.reshape(bsz, seq, D_MODEL)
```

```python
import functools
import math

import jax
import jax.numpy as jnp
from jax import lax
from jax.experimental import pallas as pl
from jax.experimental.pallas import tpu as pltpu

F32 = jnp.float32
BF16 = jnp.bfloat16
NEG_INF = float("-inf")

D_MODEL = 1024
DEPTH = 4
N_MEM = 256
HEAD_DIM = 64
GROUP_WIDTH = 256
HEADS = 4
ROPE_THETA = 500000.0
MLA_Q_LORA = 192
MLA_KV_LORA = 128
MLA_NOPE = 64
MLA_ROPE = 32
MOBA_BLOCK = 256
MOBA_TOPK = 3
MOBA_ROT = 16
DIFF_HALF = 32
DIFF_ROT = 8
XATTN_HEADS = 4
XATTN_HEAD_DIM = 256
N_GROUPS = 8
EXPERTS_PER_GROUP = 4
N_EXPERTS = 32
EXPERT_FF = 256
DEEPNORM_ALPHA = (2 * DEPTH) ** 0.25
LN_EPS = 1e-5
RMS_EPS = 1e-6

LANES = 128
VMEM_LIMIT_BYTES = 56 * 1024 * 1024
PROJ_TM = 512
POST_TM = 1024
POST_CHUNK = 512
SB_TQ = 512
SB_TK = 256
FLASH_TILE = 512
MOE_TM = 512
MOE_ROWS_PER_STEP = 512

C_MB, C_DF, C_LAT, C_SB = 0, 768, 1536, 2048
PROJ_IN_COLS = 2816
O_MBQ, O_MBK, O_MBV, O_DFV, O_MLQ, O_MLK, O_MLV = 0, 512, 1024, 1536, 2048, 2560, 3072
O_SBQ, O_SBK, O_SBV, O_DFQ, O_DFK = 3584, 3840, 4096, 4352, 4608
PROJ_OUT_COLS = 4864
ROUTER_LANES = 128
MOE_ROW_WIDTH = D_MODEL + ROUTER_LANES
LOG2E = 1.4426950408889634
MOBA_MASK_BIAS = 2.0 ** 100


def _dot(a, b):
    return jnp.dot(a, b, preferred_element_type=F32)


def _dot_nt(a, b):
    return lax.dot_general(a, b, (((1,), (1,)), ((), ())), preferred_element_type=F32)


def _split_bf16(x):
    hi = x.astype(BF16)
    lo = (x - hi.astype(F32)).astype(BF16)
    return hi, lo


def _layer_norm(x, g, b):
    mu = jnp.mean(x, axis=-1, keepdims=True)
    xc = x - mu
    var = jnp.mean(xc * xc, axis=-1, keepdims=True)
    return xc * lax.rsqrt(var + LN_EPS) * g + b


def _lane_iota(shape):
    return lax.broadcasted_iota(jnp.int32, shape, 1)


def _rope128(t, c, sa, sb, half):
    nxt = pltpu.roll(t, LANES - half, axis=1)
    prv = pltpu.roll(t, half, axis=1)
    return t * c + nxt * sa + prv * sb


def _proj_kernel(n_blk, x_ref, w1_ref, wuq_ref, wukv_ref, gq_ref, gkv_ref,
                 mbc_ref, mbsa_ref, mbsb_ref, dfc_ref, dfsa_ref, dfsb_ref,
                 mlc_ref, mlsa_ref, mlsb_ref, out_ref, kmean_ref):
    xb = x_ref[...].astype(BF16)
    tm = xb.shape[0]
    lane = _lane_iota((tm, LANES))
    upper = lane >= HEAD_DIM

    def mm(c0, width):
        return _dot(xb, w1_ref[:, c0:c0 + width])

    def put(c0, val):
        out_ref[:, c0:c0 + val.shape[1]] = val.astype(BF16)

    def with_ones(v):
        return jnp.where(upper, 1.0, v)

    def head_tile(pair, odd):
        return pltpu.roll(pair, HEAD_DIM, axis=1) if odd else pair

    mb = mm(C_MB, 768)
    mbc, mbsa, mbsb = mbc_ref[...], mbsa_ref[...], mbsb_ref[...]
    per_step = tm // MOBA_BLOCK
    base_blk = lax.rem(pl.program_id(0) * per_step, n_blk)
    row_blk = jnp.right_shift(lax.broadcasted_iota(jnp.int32, (tm, LANES), 0), int(math.log2(MOBA_BLOCK)))
    onehot = lane == HEAD_DIM + base_blk + row_blk
    for t in range(2):
        c = slice(LANES * t, LANES * (t + 1))
        qp = _rope128(mb[:, c], mbc, mbsa, mbsb, MOBA_ROT // 2) * (HEAD_DIM ** -0.5 * LOG2E)
        kp = _rope128(mb[:, 256 + LANES * t:256 + LANES * (t + 1)], mbc, mbsa, mbsb, MOBA_ROT // 2)
        vp = mb[:, 512 + LANES * t:512 + LANES * (t + 1)]
        for odd in range(2):
            h = 2 * t + odd
            put(O_MBQ + LANES * h, jnp.where(upper, 0.0, head_tile(qp, odd)))
            k = jnp.where(upper, 0.0, head_tile(kp, odd))
            for r in range(per_step):
                kmean_ref[0, r:r + 1, LANES * h:LANES * (h + 1)] = jnp.mean(
                    k[r * MOBA_BLOCK:(r + 1) * MOBA_BLOCK], axis=0, keepdims=True)
            put(O_MBK + LANES * h, jnp.where(onehot, 1.0, k))
            put(O_MBV + LANES * h, with_ones(head_tile(vp, odd)))

    df = mm(C_DF, 768)
    dfc, dfsa, dfsb = dfc_ref[...], dfsa_ref[...], dfsb_ref[...]
    for t in range(2):
        c = slice(LANES * t, LANES * (t + 1))
        put(O_DFQ + LANES * t, _rope128(df[:, c], dfc, dfsa, dfsb, DIFF_ROT // 2) * (DIFF_HALF ** -0.5 * LOG2E))
        put(O_DFK + LANES * t, _rope128(df[:, 256 + LANES * t:256 + LANES * (t + 1)], dfc, dfsa, dfsb,
                                        DIFF_ROT // 2))
        for odd in range(2):
            put(O_DFV + LANES * (2 * t + odd),
                with_ones(head_tile(df[:, 512 + LANES * t:512 + LANES * (t + 1)], odd)))

    sb = mm(C_SB, 768)
    put(O_SBQ, sb[:, 0:256] * 0.125)
    put(O_SBK, sb[:, 256:768])

    lat = mm(C_LAT, 512)
    mlc, mlsa, mlsb = mlc_ref[...], mlsa_ref[...], mlsb_ref[...]
    ql = lat[:, 0:256]
    qn = ql * lax.rsqrt(jnp.sum(ql * ql, axis=-1, keepdims=True) * (1.0 / MLA_Q_LORA) + RMS_EPS) * gq_ref[...]
    qm = _dot(qn.astype(BF16), wuq_ref[...])
    kvl = lat[:, 256:384]
    kvn = kvl * lax.rsqrt(jnp.mean(kvl * kvl, axis=-1, keepdims=True) + RMS_EPS) * gkv_ref[...]
    kv = _dot(kvn.astype(BF16), wukv_ref[...])
    kpe = _rope128(lat[:, 384:512], mlc, mlsa, mlsb, MLA_ROPE // 2)
    mscale = (MLA_NOPE + MLA_ROPE) ** -0.5 * LOG2E
    for h in range(HEADS):
        c = slice(LANES * h, LANES * (h + 1))
        put(O_MLQ + LANES * h, _rope128(qm[:, c], mlc, mlsa, mlsb, MLA_ROPE // 2) * mscale)
        put(O_MLK + LANES * h, kv[:, c] + kpe)
        put(O_MLV + LANES * h, with_ones(kv[:, 512 + LANES * h:512 + LANES * (h + 1)]))


def _proj_call(x, w1, wuq, wukv, gq, gkv, tables, n_blk):
    t_tokens = x.shape[0]
    n_steps = t_tokens // PROJ_TM
    full = lambda shape: pl.BlockSpec(shape, lambda i: (0,) * len(shape))
    tab_spec = pl.BlockSpec((PROJ_TM, LANES), lambda i: (i, 0))
    return pl.pallas_call(
        functools.partial(_proj_kernel, n_blk),
        grid=(n_steps,),
        in_specs=[pl.BlockSpec((PROJ_TM, D_MODEL), lambda i: (i, 0)),
                  full((D_MODEL, PROJ_IN_COLS)), full((256, 512)), full((128, 1024)),
                  full((1, 256)), full((1, 128))] + [tab_spec] * 9,
        out_specs=[pl.BlockSpec((PROJ_TM, PROJ_OUT_COLS), lambda i: (i, 0)),
                   pl.BlockSpec((1, PROJ_TM // MOBA_BLOCK, 512), lambda i: (i, 0, 0))],
        out_shape=[jax.ShapeDtypeStruct((t_tokens, PROJ_OUT_COLS), BF16),
                   jax.ShapeDtypeStruct((n_steps, PROJ_TM // MOBA_BLOCK, 512), F32)],
        compiler_params=pltpu.CompilerParams(dimension_semantics=("parallel",),
                                             vmem_limit_bytes=VMEM_LIMIT_BYTES),
        name="proj",
    )(x, w1, wuq, wukv, gq, gkv, *tables)


def _causal_mask(tq, tk):
    return lax.broadcasted_iota(jnp.int32, (tq, tk), 1) <= lax.broadcasted_iota(jnp.int32, (tq, tk), 0)


def _kv_rows(j, tile):
    return pl.ds(pl.multiple_of(j * tile, tile), tile)


def _flash_update(s, v_tile, m_ref, acc_ref, idx):
    m_prev = m_ref[idx]
    m_new = jnp.maximum(m_prev, jnp.max(s, axis=-1, keepdims=True))
    p = jnp.concatenate([jnp.exp2(s[:, LANES * t:LANES * (t + 1)] - m_new) for t in range(s.shape[1] // LANES)],
                        axis=1)
    acc_ref[idx] = jnp.exp2(m_prev - m_new) * acc_ref[idx] + _dot(p.astype(BF16), v_tile)
    m_ref[idx] = m_new


def _flash_init(m_ref, acc_ref):
    m_ref[...] = jnp.full(m_ref.shape, NEG_INF, F32)
    acc_ref[...] = jnp.zeros(acc_ref.shape, F32)


def _normalized(acc):
    return acc / pltpu.roll(acc, HEAD_DIM, axis=1)


def _store_pairs(o_ref, outs):
    lane = _lane_iota(outs[0].shape)
    for t in range(2):
        pair = jnp.where(lane < HEAD_DIM, outs[2 * t], pltpu.roll(outs[2 * t + 1], HEAD_DIM, axis=1))
        o_ref[0, :, LANES * t:LANES * (t + 1)] = pair.astype(o_ref.dtype)


def _pipelined_sweep(qi, block_of, scores, update, buf_a, buf_b):
    scores(block_of(0), True, buf_a)

    def body(p, carry):
        t = 2 * p
        scores(block_of(t + 1), False, buf_b)
        update(buf_a, block_of(t))
        scores(block_of(t + 2), False, buf_a)
        update(buf_b, block_of(t + 1))
        return carry

    lax.fori_loop(0, jnp.right_shift(qi, 1), body, 0)
    odd = lax.rem(qi, 2)

    @pl.when(odd == 1)
    def _():
        scores(block_of(qi), False, buf_b)
        update(buf_a, block_of(qi - 1))
        update(buf_b, block_of(qi))

    @pl.when(odd == 0)
    def _():
        update(buf_a, block_of(qi))


def _diag_then_past(qi):
    return lambda step: jnp.where(step == 0, qi, step - 1)


def _mla_kernel(q_ref, k_ref, v_ref, o_ref, m_ref, acc_ref, sa_ref, sb_ref):
    qi = pl.program_id(1)
    diag = _causal_mask(FLASH_TILE, FLASH_TILE)
    _flash_init(m_ref, acc_ref)

    def scores(j, diagonal, buf):
        for h in range(HEADS):
            c = slice(LANES * h, LANES * (h + 1))
            s = _dot_nt(q_ref[0, :, c], k_ref[0, _kv_rows(j, FLASH_TILE), c])
            buf[h] = jnp.where(diag, s, NEG_INF) if diagonal else s

    def update(buf, j):
        for h in range(HEADS):
            _flash_update(buf[h], v_ref[0, _kv_rows(j, FLASH_TILE), LANES * h:LANES * (h + 1)], m_ref, acc_ref, h)

    _pipelined_sweep(qi, _diag_then_past(qi), scores, update, sa_ref, sb_ref)
    _store_pairs(o_ref, [_normalized(acc_ref[h]) for h in range(HEADS)])


def _moba_kernel(q_ref, k_ref, v_ref, kmean_ref, o_ref, m_ref, acc_ref, sa_ref, sb_ref, qx_ref):
    qi = pl.program_id(1)
    tq = FLASH_TILE
    n_blk = kmean_ref.shape[1]
    diag = _causal_mask(tq, FLASH_TILE)
    _flash_init(m_ref, acc_ref)
    blk = lax.broadcasted_iota(jnp.int32, (n_blk, tq), 0)
    own = (tq // MOBA_BLOCK) * qi + jnp.right_shift(lax.broadcasted_iota(jnp.int32, (n_blk, tq), 1),
                                                    int(math.log2(MOBA_BLOCK)))
    lane = _lane_iota((tq, LANES))
    slot = (lane >= HEAD_DIM) & (lane < HEAD_DIM + n_blk)
    place = jnp.where(lax.broadcasted_iota(jnp.int32, (n_blk, LANES), 1)
                      == HEAD_DIM + lax.broadcasted_iota(jnp.int32, (n_blk, LANES), 0), 1.0, 0.0).astype(BF16)
    for h in range(HEADS):
        c = slice(LANES * h, LANES * (h + 1))
        qh = q_ref[0, :, c]
        km_hi, km_lo = _split_bf16(kmean_ref[0, :, c])
        gate = _dot_nt(km_hi, qh) + _dot_nt(km_lo, qh)
        gate = jnp.where(blk < own, gate, NEG_INF)
        sel = blk == own
        for _ in range(MOBA_TOPK):
            mx = jnp.max(gate, axis=0, keepdims=True)
            first_idx = jnp.min(jnp.where(gate == mx, blk, n_blk), axis=0, keepdims=True)
            pick = (blk == first_idx) & (mx > NEG_INF)
            sel = sel | pick
            gate = jnp.where(pick, NEG_INF, gate)
        placed = lax.dot_general(jnp.where(sel, 1.0, 0.0).astype(BF16), place, (((0,), (0,)), ((), ())),
                                 preferred_element_type=F32)
        bias = ((placed - 1.0) * MOBA_MASK_BIAS).astype(BF16)
        qx_ref[h] = jnp.where(slot, bias, qh)

    def scores(j, diagonal, buf):
        for h in range(HEADS):
            s = _dot_nt(qx_ref[h], k_ref[0, _kv_rows(j, FLASH_TILE), LANES * h:LANES * (h + 1)])
            buf[h] = jnp.where(diag, s, NEG_INF) if diagonal else s

    def update(buf, j):
        for h in range(HEADS):
            _flash_update(buf[h], v_ref[0, _kv_rows(j, FLASH_TILE), LANES * h:LANES * (h + 1)], m_ref, acc_ref, h)

    _pipelined_sweep(qi, _diag_then_past(qi), scores, update, sa_ref, sb_ref)
    _store_pairs(o_ref, [_normalized(acc_ref[h]) for h in range(HEADS)])


def _diff_kernel(lambda_init, q_ref, k_ref, v_ref, lam_ref, g_ref, o_ref, m_ref, acc_ref, sa_ref, sb_ref, qs_ref):
    qi = pl.program_id(1)
    tq = FLASH_TILE
    diag = _causal_mask(tq, FLASH_TILE)
    _flash_init(m_ref, acc_ref)
    lane = _lane_iota((tq, LANES))
    zero = jnp.zeros((), BF16)
    for t in range(2):
        qt = q_ref[0, :, LANES * t:LANES * (t + 1)]
        for part in range(4):
            keep = (lane >= DIFF_HALF * part) & (lane < DIFF_HALF * (part + 1))
            qs_ref[t, part * tq:(part + 1) * tq, :] = jnp.where(keep, qt, zero)

    def scores(j, diagonal, buf):
        for t in range(2):
            s_all = _dot_nt(qs_ref[t], k_ref[0, _kv_rows(j, FLASH_TILE), LANES * t:LANES * (t + 1)])
            for part in range(4):
                s = s_all[part * tq:(part + 1) * tq]
                buf[4 * t + part] = jnp.where(diag, s, NEG_INF) if diagonal else s

    def update(buf, j):
        for idx in range(2 * HEADS):
            h = idx // 2
            _flash_update(buf[idx], v_ref[0, _kv_rows(j, FLASH_TILE), LANES * h:LANES * (h + 1)], m_ref, acc_ref, idx)

    _pipelined_sweep(qi, _diag_then_past(qi), scores, update, sa_ref, sb_ref)

    lp = lam_ref[...]
    lam = (jnp.exp(jnp.sum(lp[0:1] * lp[1:2], axis=-1, keepdims=True))
           - jnp.exp(jnp.sum(lp[2:3] * lp[3:4], axis=-1, keepdims=True)) + lambda_init)
    outs = []
    for h in range(HEADS):
        o = _normalized(acc_ref[2 * h]) - lam * _normalized(acc_ref[2 * h + 1])
        ms = jnp.sum(jnp.where(lane < HEAD_DIM, o * o, 0.0), axis=-1, keepdims=True) * (1.0 / HEAD_DIM)
        outs.append(o * lax.rsqrt(ms + RMS_EPS) * g_ref[...] * (1.0 - lambda_init))
    _store_pairs(o_ref, outs)


def _sb_kernel(q_ref, k_ref, v_ref, o_ref, cum_ref, acc_ref, za_ref, zb_ref, qs_ref):
    qi = pl.program_id(1)
    tq, tk = SB_TQ, SB_TK
    lane = _lane_iota((tq, LANES))
    zero = jnp.zeros((), BF16)
    cum_ref[...] = jnp.zeros(cum_ref.shape, F32)
    acc_ref[...] = jnp.zeros(acc_ref.shape, F32)
    for t in range(2):
        qt = q_ref[0, :, LANES * t:LANES * (t + 1)]
        qs_ref[t, 0:tq, :] = jnp.where(lane < HEAD_DIM, qt, zero)
        qs_ref[t, tq:2 * tq, :] = jnp.where(lane >= HEAD_DIM, qt, zero)
    row = lax.broadcasted_iota(jnp.int32, (tk, tk), 0)
    col = lax.broadcasted_iota(jnp.int32, (tk, tk), 1)
    u_tri = jnp.where(row > col, 1.0, 0.0).astype(BF16)
    row2 = lax.broadcasted_iota(jnp.int32, (2 * tq, tk), 0)
    q_pos = qi * tq + jnp.where(row2 >= tq, row2 - tq, row2)
    k_off = lax.broadcasted_iota(jnp.int32, (2 * tq, tk), 1)
    n_blocks = (tq // tk) * (qi + 1)

    def scores(step, overlaps_tile, buf):
        j = n_blocks - 1 - step
        for t in range(2):
            z = _dot_nt(qs_ref[t], k_ref[0, _kv_rows(j, tk), LANES * t:LANES * (t + 1)])
            buf[t] = jnp.where(j * tk + k_off < q_pos, z, NEG_INF) if overlaps_tile else z

    def update(buf, step):
        j = n_blocks - 1 - step
        for t in range(2):
            z = buf[t]
            sp = jnp.maximum(z, 0.0) + jnp.log(1.0 + jnp.exp(-jnp.abs(z)))
            later = _dot(sp.astype(BF16), u_tri)
            cum = cum_ref[t]
            a = jnp.concatenate(
                [jnp.exp(z[:, LANES * n:LANES * (n + 1)] - sp[:, LANES * n:LANES * (n + 1)]
                         - later[:, LANES * n:LANES * (n + 1)] - cum) for n in range(tk // LANES)], axis=1)
            acc_ref[t] += _dot(a.astype(BF16), v_ref[0, _kv_rows(j, tk), LANES * t:LANES * (t + 1)])
            cum_ref[t] = cum + jnp.sum(sp, axis=-1, keepdims=True)

    scores(0, True, za_ref)
    scores(1, True, zb_ref)
    update(za_ref, 0)

    def body(i, carry):
        step = 2 * i + 1
        scores(step + 1, False, za_ref)
        update(zb_ref, step)
        scores(step + 2, False, zb_ref)
        update(za_ref, step + 1)
        return carry

    lax.fori_loop(0, qi, body, 0)
    update(zb_ref, n_blocks - 1)
    for t in range(2):
        acc = acc_ref[t]
        o_ref[0, :, LANES * t:LANES * (t + 1)] = jnp.where(lane < HEAD_DIM, acc[0:tq], acc[tq:2 * tq]).astype(o_ref.dtype)


def _attn_call(kernel, name, tile, proj3, q_off, k_off, v_off, q_width, k_width, v_width, scratch,
               extra_in=(), extra_specs=()):
    bsz, seq, _ = proj3.shape
    return pl.pallas_call(
        kernel,
        grid=(bsz, seq // tile),
        in_specs=[pl.BlockSpec((1, tile, q_width), lambda b, i: (b, i, q_off // q_width)),
                  pl.BlockSpec((1, seq, k_width), lambda b, i: (b, 0, k_off // k_width)),
                  pl.BlockSpec((1, seq, v_width), lambda b, i: (b, 0, v_off // v_width))] + list(extra_specs),
        out_specs=pl.BlockSpec((1, tile, GROUP_WIDTH), lambda b, i: (b, i, 0)),
        out_shape=jax.ShapeDtypeStruct((bsz, seq, GROUP_WIDTH), BF16),
        scratch_shapes=scratch,
        compiler_params=pltpu.CompilerParams(dimension_semantics=("parallel", "parallel"),
                                             vmem_limit_bytes=VMEM_LIMIT_BYTES),
        name=name,
    )(proj3, proj3, proj3, *extra_in)


def _state(n, rows):
    return pltpu.VMEM((n, rows, LANES), F32)


def _staging(n, rows, tk):
    return [pltpu.VMEM((n, rows, tk), F32), pltpu.VMEM((n, rows, tk), F32)]


def _memkv_kernel(mem_ref, wk_ref, wv_ref, k_ref, v_ref):
    mb = mem_ref[...].astype(BF16)
    k_ref[...] = _dot(mb, wk_ref[...]).astype(BF16)
    v_ref[...] = _dot(mb, wv_ref[...]).astype(BF16)


def _memkv_call(mem2, wk, wv):
    rows = mem2.shape[0]
    full = pl.BlockSpec((D_MODEL, D_MODEL), lambda i: (0, 0))
    blk = pl.BlockSpec((N_MEM, D_MODEL), lambda i: (i, 0))
    return pl.pallas_call(
        _memkv_kernel,
        grid=(rows // N_MEM,),
        in_specs=[blk, full, full],
        out_specs=[blk, blk],
        out_shape=[jax.ShapeDtypeStruct((rows, D_MODEL), BF16)] * 2,
        compiler_params=pltpu.CompilerParams(dimension_semantics=("parallel",),
                                             vmem_limit_bytes=VMEM_LIMIT_BYTES),
        name="memkv",
    )(mem2, wk, wv)


def _post_kernel(x_ref, oa_ref, ob_ref, oc_ref, od_ref, wout_ref, g1_ref, b1_ref,
                 wq_ref, km_ref, vm_ref, wo_ref, g2_ref, b2_ref, wrh_ref, wrl_ref, br_ref,
                 x2r_ref, route_ref, counts_ref, oh_ref, cnt_ref):
    @pl.when(pl.program_id(0) == 0)
    def _():
        cnt_ref[...] = jnp.zeros_like(cnt_ref)

    for r0 in range(0, POST_TM, POST_CHUNK):
        _post_rows(slice(r0, r0 + POST_CHUNK), x_ref, oa_ref, ob_ref, oc_ref, od_ref, wout_ref, g1_ref, b1_ref,
                   wq_ref, km_ref, vm_ref, wo_ref, g2_ref, b2_ref, wrh_ref, wrl_ref, br_ref, x2r_ref, route_ref,
                   oh_ref, cnt_ref)
    counts_ref[...] = cnt_ref[...]


def _post_rows(rows, x_ref, oa_ref, ob_ref, oc_ref, od_ref, wout_ref, g1_ref, b1_ref,
               wq_ref, km_ref, vm_ref, wo_ref, g2_ref, b2_ref, wrh_ref, wrl_ref, br_ref, x2r_ref, route_ref,
               oh_ref, cnt_ref):
    mix = (_dot(oa_ref[rows, :], wout_ref[0:256, :]) + _dot(ob_ref[rows, :], wout_ref[256:512, :])
           + _dot(oc_ref[rows, :], wout_ref[512:768, :]) + _dot(od_ref[rows, :], wout_ref[768:1024, :]))
    x1 = _layer_norm(DEEPNORM_ALPHA * x_ref[rows, :] + mix, g1_ref[...], b1_ref[...])

    q = (_dot(x1.astype(BF16), wq_ref[...]) * (XATTN_HEAD_DIM ** -0.5)).astype(BF16)
    for h in range(XATTN_HEADS):
        c = slice(XATTN_HEAD_DIM * h, XATTN_HEAD_DIM * (h + 1))
        s = _dot_nt(q[:, c], km_ref[0, :, c])
        p = jnp.exp(s - jnp.max(s, axis=-1, keepdims=True))
        o = _dot(p.astype(BF16), vm_ref[0, :, c]) / jnp.sum(p, axis=-1, keepdims=True)
        oh_ref[rows, c] = o.astype(BF16)
    xa = _dot(oh_ref[rows, :], wo_ref[...])
    x2 = _layer_norm(DEEPNORM_ALPHA * x1 + xa, g2_ref[...], b2_ref[...])
    x2r_ref[rows, 0:D_MODEL] = x2

    hi, lo = _split_bf16(x2)
    logits = _dot(hi, wrh_ref[...]) + _dot(hi, wrl_ref[...]) + _dot(lo, wrh_ref[...]) + br_ref[...]
    lane = _lane_iota(logits.shape)
    gl = jnp.where(lane < N_GROUPS, logits, NEG_INF)
    gmx = jnp.max(gl, axis=-1, keepdims=True)
    gidx = jnp.min(jnp.where(gl == gmx, lane, ROUTER_LANES), axis=-1, keepdims=True)
    gw = 1.0 / jnp.sum(jnp.exp(gl - gmx), axis=-1, keepdims=True)
    e0 = N_GROUPS + EXPERTS_PER_GROUP * gidx
    el = jnp.where((lane >= e0) & (lane < e0 + EXPERTS_PER_GROUP), logits, NEG_INF)
    m1 = jnp.max(el, axis=-1, keepdims=True)
    i1 = jnp.min(jnp.where(el == m1, lane, ROUTER_LANES), axis=-1, keepdims=True)
    el = jnp.where(lane == i1, NEG_INF, el)
    m2 = jnp.max(el, axis=-1, keepdims=True)
    i2 = jnp.min(jnp.where(el == m2, lane, ROUTER_LANES), axis=-1, keepdims=True)
    e = jnp.exp(m2 - m1)
    w1 = gw / (1.0 + e)
    w2 = gw * e / (1.0 + e)

    n = logits.shape[0]
    onehot = jnp.where(lane == gidx, 1.0, 0.0)
    earlier = (lax.broadcasted_iota(jnp.int32, (n, n), 1) < lax.broadcasted_iota(jnp.int32, (n, n), 0))
    before = _dot(jnp.where(earlier, 1.0, 0.0).astype(BF16), onehot.astype(BF16)) + cnt_ref[...]
    rank = jnp.sum(onehot * before, axis=-1, keepdims=True)
    cnt_ref[...] += jnp.sum(onehot, axis=0, keepdims=True)
    route = (jnp.where(lane == i1, w1, 0.0) + jnp.where(lane == i2, w2, 0.0)
             + jnp.where(lane == 0, gidx.astype(F32), 0.0) + jnp.where(lane == 1, rank, 0.0))
    x2r_ref[rows, D_MODEL:] = route
    route_ref[rows, :] = route


def _post_call(x, oa, ob, oc, od, wout, g1, b1, wq, kmem, vmem, wo, g2, b2, wrh, wrl, br, seq):
    t_tokens = x.shape[0]
    tm = POST_TM
    per_seq = seq // tm
    row = lambda w: pl.BlockSpec((tm, w), lambda i: (i, 0))
    full = lambda shape: pl.BlockSpec(shape, lambda i: (0,) * len(shape))
    mem_spec = pl.BlockSpec((1, N_MEM, D_MODEL), lambda i: (i // per_seq, 0, 0))
    sq = (D_MODEL, D_MODEL)
    return pl.pallas_call(
        _post_kernel,
        grid=(t_tokens // tm,),
        in_specs=[row(D_MODEL), row(256), row(256), row(256), row(256),
                  full(sq), full((1, D_MODEL)), full((1, D_MODEL)),
                  full(sq), mem_spec, mem_spec, full(sq), full((1, D_MODEL)), full((1, D_MODEL)),
                  full((D_MODEL, ROUTER_LANES)), full((D_MODEL, ROUTER_LANES)), full((1, ROUTER_LANES))],
        out_specs=[row(MOE_ROW_WIDTH), row(ROUTER_LANES), full((1, ROUTER_LANES))],
        out_shape=[jax.ShapeDtypeStruct((t_tokens, MOE_ROW_WIDTH), F32),
                   jax.ShapeDtypeStruct((t_tokens, ROUTER_LANES), F32),
                   jax.ShapeDtypeStruct((1, ROUTER_LANES), F32)],
        scratch_shapes=[pltpu.VMEM((tm, D_MODEL), BF16), pltpu.VMEM((1, ROUTER_LANES), F32)],
        compiler_params=pltpu.CompilerParams(dimension_semantics=("arbitrary",),
                                             vmem_limit_bytes=VMEM_LIMIT_BYTES),
        name="post",
    )(x, oa, ob, oc, od, wout, g1, b1, wq, kmem, vmem, wo, g2, b2, wrh, wrl, br)


def _row_dma_wait(src_ref, dst_ref, sem, rows):
    pltpu.make_async_copy(src_ref.at[pl.ds(0, rows)], dst_ref.at[pl.ds(0, rows)], sem).wait()


def _issue_row_dmas(copy_of_row, sems):
    for r in range(MOE_ROWS_PER_STEP):
        copy_of_row(r, sems[r % 2]).start(priority=r % 2)


def _gather_kernel(pos_ref, x_ref, xs_init_ref, xs_ref, sem0, sem1):
    del xs_init_ref
    _issue_row_dmas(lambda r, sem: pltpu.make_async_copy(
        x_ref.at[pl.ds(r, 1)], xs_ref.at[pl.ds(pos_ref[0, 0, r], 1)], sem), (sem0, sem1))
    for sem in (sem0, sem1):
        _row_dma_wait(x_ref, xs_ref, sem, MOE_ROWS_PER_STEP // 2)


def _scatter_kernel(pos_ref, ys_ref, y_ref, sem0, sem1):
    _issue_row_dmas(lambda r, sem: pltpu.make_async_copy(
        ys_ref.at[pl.ds(pos_ref[0, 0, r], 1)], y_ref.at[pl.ds(r, 1)], sem), (sem0, sem1))
    for sem in (sem0, sem1):
        _row_dma_wait(ys_ref, y_ref, sem, MOE_ROWS_PER_STEP // 2)


def _pos_spec():
    return pl.BlockSpec((1, 1, MOE_ROWS_PER_STEP), lambda i: (i, 0, 0), memory_space=pltpu.SMEM)


def _token_rows_spec(width):
    return pl.BlockSpec((MOE_ROWS_PER_STEP, width), lambda i: (i, 0))


def _gather_call(pos3, x2r, sorted_buf):
    n_steps = pos3.shape[0]
    any_spec = pl.BlockSpec(memory_space=pl.ANY)
    return pl.pallas_call(
        _gather_kernel,
        grid=(n_steps,),
        in_specs=[_pos_spec(), _token_rows_spec(x2r.shape[1]), any_spec],
        out_specs=any_spec,
        out_shape=jax.ShapeDtypeStruct(sorted_buf.shape, sorted_buf.dtype),
        input_output_aliases={2: 0},
        scratch_shapes=[pltpu.SemaphoreType.DMA(()), pltpu.SemaphoreType.DMA(())],
        compiler_params=pltpu.CompilerParams(dimension_semantics=("arbitrary",), has_side_effects=True),
        name="moe_gather",
    )(pos3, x2r, sorted_buf)


def _scatter_call(pos3, ys, t_tokens):
    n_steps = pos3.shape[0]
    any_spec = pl.BlockSpec(memory_space=pl.ANY)
    return pl.pallas_call(
        _scatter_kernel,
        grid=(n_steps,),
        in_specs=[_pos_spec(), any_spec],
        out_specs=_token_rows_spec(ys.shape[1]),
        out_shape=jax.ShapeDtypeStruct((t_tokens, ys.shape[1]), ys.dtype),
        scratch_shapes=[pltpu.SemaphoreType.DMA(()), pltpu.SemaphoreType.DMA(())],
        compiler_params=pltpu.CompilerParams(dimension_semantics=("arbitrary",)),
        name="moe_scatter",
    )(pos3, ys)


def _moe_kernel(tile_grp_ref, n_used_ref, xs_ref, wg_ref, wu_ref, wd_ref, g_ref, b_ref, o_ref):
    i = pl.program_id(0)

    @pl.when(i < n_used_ref[0])
    def _():
        grp = tile_grp_ref[i]
        x2 = xs_ref[:, 0:D_MODEL]
        xb = x2.astype(BF16)
        route = xs_ref[:, D_MODEL:]
        lane = _lane_iota(route.shape)
        acc = jnp.zeros(x2.shape, F32)
        for j in range(EXPERTS_PER_GROUP):
            e = N_GROUPS + EXPERTS_PER_GROUP * grp + j
            gate_col = jnp.sum(jnp.where(lane == e, route, 0.0), axis=-1, keepdims=True)
            gg = _dot(xb, wg_ref[j])
            uu = _dot(xb, wu_ref[j])
            hid = gg * (1.0 / (1.0 + jnp.exp(-gg))) * uu * gate_col
            acc = acc + _dot(hid.astype(BF16), wd_ref[j])
        o_ref[...] = _layer_norm(DEEPNORM_ALPHA * x2 + acc, g_ref[...], b_ref[...])

    @pl.when(i >= n_used_ref[0])
    def _():
        o_ref[...] = jnp.zeros_like(o_ref)


def _moe_call(tile_grp, n_used, xs, wg, wu, wd, g, b):
    cap_rows = xs.shape[0]
    tm = MOE_TM
    row = lambda w: pl.BlockSpec((tm, w), lambda i, tg, nu: (i, 0))
    vec = pl.BlockSpec((1, D_MODEL), lambda i, tg, nu: (0, 0))
    wspec = lambda shape: pl.BlockSpec(shape, lambda i, tg, nu: (tg[i], 0, 0))
    return pl.pallas_call(
        _moe_kernel,
        grid_spec=pltpu.PrefetchScalarGridSpec(
            num_scalar_prefetch=2,
            grid=(cap_rows // tm,),
            in_specs=[row(MOE_ROW_WIDTH),
                      wspec((EXPERTS_PER_GROUP, D_MODEL, EXPERT_FF)),
                      wspec((EXPERTS_PER_GROUP, D_MODEL, EXPERT_FF)),
                      wspec((EXPERTS_PER_GROUP, EXPERT_FF, D_MODEL)),
                      vec, vec],
            out_specs=row(D_MODEL)),
        out_shape=jax.ShapeDtypeStruct((cap_rows, D_MODEL), F32),
        compiler_params=pltpu.CompilerParams(dimension_semantics=("arbitrary",),
                                             vmem_limit_bytes=VMEM_LIMIT_BYTES),
        name="moe",
    )(tile_grp, n_used, xs, wg, wu, wd, g, b)


def _moe_capacity_rows(t_tokens):
    return (t_tokens // MOE_TM + N_GROUPS) * MOE_TM


def _moe_layer(x2r, route, counts, sorted_buf, wg, wu, wd, g, b):
    t_tokens = x2r.shape[0]
    tm = MOE_TM
    n_tiles = sorted_buf.shape[0] // tm
    gid = route[:, 0].astype(jnp.int32)
    rank = route[:, 1].astype(jnp.int32)
    cnt = counts[0, :N_GROUPS].astype(jnp.int32)
    tiles_per_grp = (cnt + tm - 1) // tm
    tile_end = jnp.cumsum(tiles_per_grp)
    pos = ((tile_end - tiles_per_grp) * tm)[gid] + rank
    tile_ids = jnp.arange(n_tiles, dtype=jnp.int32)
    tile_grp = jnp.minimum(jnp.sum((tile_ids[:, None] >= tile_end[None, :]).astype(jnp.int32), axis=1),
                           N_GROUPS - 1)
    n_used = tile_end[-1:].astype(jnp.int32)
    pos3 = pos.reshape(t_tokens // MOE_ROWS_PER_STEP, 1, MOE_ROWS_PER_STEP)

    xs = _gather_call(pos3, x2r, sorted_buf)
    ys = _moe_call(tile_grp, n_used, xs, wg, wu, wd, g, b)
    return _scatter_call(pos3, ys, t_tokens), xs


def _rope_lane_tables(positions, rot, period, offset):
    half = rot // 2
    inv = ROPE_THETA ** (-jnp.arange(0, rot, 2, dtype=F32) / rot)
    ang = positions.astype(F32).reshape(-1)[:, None] * inv
    cos, sin = jnp.cos(ang), jnp.sin(ang)
    p = jnp.arange(LANES) % period - offset
    first = (p >= 0) & (p < half)
    second = (p >= half) & (p < rot)
    idx = jnp.clip(jnp.where(second, p - half, p), 0, half - 1)
    cg, sg = cos[:, idx], sin[:, idx]
    c = jnp.where(first | second, cg, 1.0)
    sa = jnp.where(first, -sg, 0.0)
    sb = jnp.where(second, sg, 0.0)
    return c, sa, sb


def _prep_proj_weights(w_in, w_uq, w_ukv, q_norm, kv_norm):
    q_lat, kv_lat, k_rope = w_in[:, 0:192], w_in[:, 192:320], w_in[:, 320:352]
    sb, mb, df = w_in[:, 352:1120], w_in[:, 1120:1888], w_in[:, 1888:2656]
    zeros = lambda n: jnp.zeros((D_MODEL, n), w_in.dtype)

    def head_tiles(w):
        w = w.reshape(w.shape[0], HEADS, HEAD_DIM)
        return jnp.pad(w, ((0, 0), (0, 0), (0, LANES - HEAD_DIM))).reshape(w.shape[0], HEADS * LANES)

    w1 = jnp.concatenate([mb, df, q_lat, zeros(64), kv_lat, zeros(64), k_rope, zeros(32), sb], axis=1)
    wuq = w_uq.reshape(MLA_Q_LORA, HEADS, MLA_NOPE + MLA_ROPE)
    wuq = jnp.pad(wuq, ((0, 256 - MLA_Q_LORA), (0, 0), (0, LANES - MLA_NOPE - MLA_ROPE))).reshape(256, HEADS * LANES)
    wukv = w_ukv.reshape(MLA_KV_LORA, HEADS, 2, HEAD_DIM)
    wk = head_tiles(wukv[:, :, 0].reshape(MLA_KV_LORA, HEADS * HEAD_DIM))
    wv = head_tiles(wukv[:, :, 1].reshape(MLA_KV_LORA, HEADS * HEAD_DIM))
    gq = jnp.pad(q_norm, (0, 256 - MLA_Q_LORA)).reshape(1, 256)
    return (w1.astype(BF16), wuq.astype(BF16), jnp.concatenate([wk, wv], axis=1).astype(BF16),
            gq, kv_norm.reshape(1, MLA_KV_LORA))


def kernel(x, mem, positions, w_in, mla_q_norm, w_uq, mla_kv_norm, w_ukv, diff_lambda, diff_subln,
           w_out, ln_mix_g, ln_mix_b, xattn_wq, xattn_wk, xattn_wv, xattn_wo, ln_mem_g, ln_mem_b,
           router_group_w, router_group_b, router_expert_w, router_expert_b,
           expert_w_gate, expert_w_up, expert_w_down, ln_ffn_g, ln_ffn_b):
    bsz, seq, _ = x.shape
    t_tokens = bsz * seq
    n_blk = seq // MOBA_BLOCK
    assert seq % PROJ_TM == 0 and seq % POST_TM == 0 and seq % FLASH_TILE == 0 and seq % SB_TQ == 0 and SB_TQ == 2 * SB_TK
    assert t_tokens % MOE_TM == 0 and t_tokens % MOE_ROWS_PER_STEP == 0
    assert mem.shape[1] == N_MEM and n_blk <= LANES - HEAD_DIM

    tables = (_rope_lane_tables(positions, MOBA_ROT, HEAD_DIM, 0)
              + _rope_lane_tables(positions, DIFF_ROT, DIFF_HALF, 0)
              + _rope_lane_tables(positions, MLA_ROPE, LANES, MLA_NOPE))
    mem2 = mem.reshape(bsz * N_MEM, D_MODEL)
    row = lambda v: v.reshape(1, -1)
    xf = x.reshape(t_tokens, D_MODEL)
    sorted_buf = jnp.zeros((_moe_capacity_rows(t_tokens), MOE_ROW_WIDTH), F32)

    for l in range(DEPTH):
        lambda_init = 0.8 - 0.6 * math.exp(-0.3 * l)
        w1, wuq, wukv, gq, gkv = _prep_proj_weights(w_in[l], w_uq[l], w_ukv[l], mla_q_norm[l], mla_kv_norm[l])
        proj, kmean = _proj_call(xf, w1, wuq, wukv, gq, gkv, tables, n_blk)
        proj3 = proj.reshape(bsz, seq, PROJ_OUT_COLS)
        kmean3 = kmean.reshape(bsz, n_blk, HEADS * LANES)

        ft, st = FLASH_TILE, SB_TQ
        per_head, per_pair = HEADS * LANES, GROUP_WIDTH
        o_mla = _attn_call(_mla_kernel, "mla_attn", ft, proj3, O_MLQ, O_MLK, O_MLV, per_head, per_head, per_head,
                           [_state(HEADS, ft), _state(HEADS, ft)] + _staging(HEADS, ft, ft))
        o_sb = _attn_call(_sb_kernel, "sb_attn", st, proj3, O_SBQ, O_SBK, O_SBV, per_pair, per_pair, per_pair,
                          [_state(2, 2 * st), _state(2, 2 * st)]
                          + _staging(2, 2 * st, SB_TK) + [pltpu.VMEM((2, 2 * st, LANES), BF16)])
        o_mb = _attn_call(_moba_kernel, "moba_attn", ft, proj3, O_MBQ, O_MBK, O_MBV, per_head, per_head, per_head,
                          [_state(HEADS, ft), _state(HEADS, ft)] + _staging(HEADS, ft, ft)
                          + [pltpu.VMEM((HEADS, ft, LANES), BF16)],
                          extra_in=(kmean3,),
                          extra_specs=(pl.BlockSpec((1, n_blk, HEADS * LANES), lambda b, i: (b, 0, 0)),))
        subln = jnp.tile(diff_subln[l], LANES // HEAD_DIM).reshape(1, LANES)
        o_df = _attn_call(functools.partial(_diff_kernel, lambda_init), "diff_attn", ft, proj3,
                          O_DFQ, O_DFK, O_DFV, per_pair, per_pair, per_head,
                          [_state(2 * HEADS, ft), _state(2 * HEADS, ft)] + _staging(2 * HEADS, ft, ft)
                          + [pltpu.VMEM((2, 4 * ft, LANES), BF16)],
                          extra_in=(diff_lambda[l], subln),
                          extra_specs=(pl.BlockSpec((4, DIFF_HALF), lambda b, i: (0, 0)),
                                       pl.BlockSpec((1, LANES), lambda b, i: (0, 0))))

        kmem, vmem = _memkv_call(mem2, xattn_wk[l].astype(BF16), xattn_wv[l].astype(BF16))
        wr = jnp.concatenate([router_group_w[l], router_expert_w[l],
                              jnp.zeros((D_MODEL, ROUTER_LANES - N_GROUPS - N_EXPERTS), F32)], axis=1)
        wr_hi = wr.astype(BF16)
        wr_lo = (wr - wr_hi.astype(F32)).astype(BF16)
        br = jnp.concatenate([router_group_b[l], router_expert_b[l],
                              jnp.zeros((ROUTER_LANES - N_GROUPS - N_EXPERTS,), F32)]).reshape(1, ROUTER_LANES)
        flat = lambda o: o.reshape(t_tokens, GROUP_WIDTH)
        x2r, route, counts = _post_call(
            xf, flat(o_mla), flat(o_sb), flat(o_mb), flat(o_df), w_out[l].astype(BF16),
            row(ln_mix_g[l]), row(ln_mix_b[l]), xattn_wq[l].astype(BF16),
            kmem.reshape(bsz, N_MEM, D_MODEL), vmem.reshape(bsz, N_MEM, D_MODEL), xattn_wo[l].astype(BF16),
            row(ln_mem_g[l]), row(ln_mem_b[l]), wr_hi, wr_lo, br, seq)

        xf, sorted_buf = _moe_layer(x2r, route, counts, sorted_buf, expert_w_gate[l].astype(BF16),
                                    expert_w_up[l].astype(BF16), expert_w_down[l].astype(BF16),
                                    row(ln_ffn_g[l]), row(ln_ffn_b[l]))
    return xf.reshape(bsz, seq, D_MODEL)
```

```python
import functools
import math

import jax
import jax.numpy as jnp
from jax import lax
from jax.experimental import pallas as pl
from jax.experimental.pallas import tpu as pltpu

F32 = jnp.float32
BF16 = jnp.bfloat16
NEG_INF = float("-inf")

D_MODEL = 1024
DEPTH = 4
N_MEM = 256
HEAD_DIM = 64
GROUP_WIDTH = 256
HEADS = 4
ROPE_THETA = 500000.0
MLA_Q_LORA = 192
MLA_KV_LORA = 128
MLA_NOPE = 64
MLA_ROPE = 32
MOBA_BLOCK = 256
MOBA_TOPK = 3
MOBA_ROT = 16
DIFF_HALF = 32
DIFF_ROT = 8
XATTN_HEADS = 4
XATTN_HEAD_DIM = 256
N_GROUPS = 8
EXPERTS_PER_GROUP = 4
N_EXPERTS = 32
EXPERT_FF = 256
DEEPNORM_ALPHA = (2 * DEPTH) ** 0.25
LN_EPS = 1e-5
RMS_EPS = 1e-6

LANES = 128
VMEM_LIMIT_BYTES = 56 * 1024 * 1024
PROJ_TM = 512
POST_TM = 1024
POST_CHUNK = 512
SB_TQ = 512
SB_TK = 256
FLASH_TILE = 512
MOE_TM = 512
MOE_ROWS_PER_STEP = 1024

C_MB, C_DF, C_LAT, C_SB = 0, 768, 1536, 2048
PROJ_IN_COLS = 2816
O_MBQ, O_MBK, O_MBV, O_DFV, O_MLQ, O_MLK, O_MLV = 0, 512, 1024, 1536, 2048, 2560, 3072
O_SBQ, O_SBK, O_SBV, O_DFQ, O_DFK = 3584, 3840, 4096, 4352, 4608
PROJ_OUT_COLS = 4864
ROUTER_LANES = 128
MOE_ROW_WIDTH = D_MODEL + ROUTER_LANES
LOG2E = 1.4426950408889634
MOBA_MASK_BIAS = 2.0 ** 100


def _dot(a, b):
    return jnp.dot(a, b, preferred_element_type=F32)


def _dot_nt(a, b):
    return lax.dot_general(a, b, (((1,), (1,)), ((), ())), preferred_element_type=F32)


def _split_bf16(x):
    hi = x.astype(BF16)
    lo = (x - hi.astype(F32)).astype(BF16)
    return hi, lo


def _layer_norm(x, g, b):
    mu = jnp.mean(x, axis=-1, keepdims=True)
    xc = x - mu
    var = jnp.mean(xc * xc, axis=-1, keepdims=True)
    return xc * lax.rsqrt(var + LN_EPS) * g + b


def _lane_iota(shape):
    return lax.broadcasted_iota(jnp.int32, shape, 1)


def _rope128(t, c, sa, sb, half):
    nxt = pltpu.roll(t, LANES - half, axis=1)
    prv = pltpu.roll(t, half, axis=1)
    return t * c + nxt * sa + prv * sb


def _proj_kernel(n_blk, x_ref, w1_ref, wuq_ref, wukv_ref, gq_ref, gkv_ref,
                 mbc_ref, mbsa_ref, mbsb_ref, dfc_ref, dfsa_ref, dfsb_ref,
                 mlc_ref, mlsa_ref, mlsb_ref, out_ref, kmean_ref):
    xb = x_ref[...].astype(BF16)
    tm = xb.shape[0]
    lane = _lane_iota((tm, LANES))
    upper = lane >= HEAD_DIM

    def mm(c0, width):
        return _dot(xb, w1_ref[:, c0:c0 + width])

    def put(c0, val):
        out_ref[:, c0:c0 + val.shape[1]] = val.astype(BF16)

    def with_ones(v):
        return jnp.where(upper, 1.0, v)

    def head_tile(pair, odd):
        return pltpu.roll(pair, HEAD_DIM, axis=1) if odd else pair

    mb = mm(C_MB, 768)
    mbc, mbsa, mbsb = mbc_ref[...], mbsa_ref[...], mbsb_ref[...]
    per_step = tm // MOBA_BLOCK
    base_blk = lax.rem(pl.program_id(0) * per_step, n_blk)
    row_blk = jnp.right_shift(lax.broadcasted_iota(jnp.int32, (tm, LANES), 0), int(math.log2(MOBA_BLOCK)))
    onehot = lane == HEAD_DIM + base_blk + row_blk
    for t in range(2):
        c = slice(LANES * t, LANES * (t + 1))
        qp = _rope128(mb[:, c], mbc, mbsa, mbsb, MOBA_ROT // 2) * (HEAD_DIM ** -0.5 * LOG2E)
        kp = _rope128(mb[:, 256 + LANES * t:256 + LANES * (t + 1)], mbc, mbsa, mbsb, MOBA_ROT // 2)
        vp = mb[:, 512 + LANES * t:512 + LANES * (t + 1)]
        for odd in range(2):
            h = 2 * t + odd
            put(O_MBQ + LANES * h, jnp.where(upper, 0.0, head_tile(qp, odd)))
            k = jnp.where(upper, 0.0, head_tile(kp, odd))
            for r in range(per_step):
                kmean_ref[0, r:r + 1, LANES * h:LANES * (h + 1)] = jnp.mean(
                    k[r * MOBA_BLOCK:(r + 1) * MOBA_BLOCK], axis=0, keepdims=True)
            put(O_MBK + LANES * h, jnp.where(onehot, 1.0, k))
            put(O_MBV + LANES * h, with_ones(head_tile(vp, odd)))

    df = mm(C_DF, 768)
    dfc, dfsa, dfsb = dfc_ref[...], dfsa_ref[...], dfsb_ref[...]
    for t in range(2):
        c = slice(LANES * t, LANES * (t + 1))
        put(O_DFQ + LANES * t, _rope128(df[:, c], dfc, dfsa, dfsb, DIFF_ROT // 2) * (DIFF_HALF ** -0.5 * LOG2E))
        put(O_DFK + LANES * t, _rope128(df[:, 256 + LANES * t:256 + LANES * (t + 1)], dfc, dfsa, dfsb,
                                        DIFF_ROT // 2))
        for odd in range(2):
            put(O_DFV + LANES * (2 * t + odd),
                with_ones(head_tile(df[:, 512 + LANES * t:512 + LANES * (t + 1)], odd)))

    sb = mm(C_SB, 768)
    put(O_SBQ, sb[:, 0:256] * 0.125)
    put(O_SBK, sb[:, 256:768])

    lat = mm(C_LAT, 512)
    mlc, mlsa, mlsb = mlc_ref[...], mlsa_ref[...], mlsb_ref[...]
    ql = lat[:, 0:256]
    qn = ql * lax.rsqrt(jnp.sum(ql * ql, axis=-1, keepdims=True) * (1.0 / MLA_Q_LORA) + RMS_EPS) * gq_ref[...]
    qm = _dot(qn.astype(BF16), wuq_ref[...])
    kvl = lat[:, 256:384]
    kvn = kvl * lax.rsqrt(jnp.mean(kvl * kvl, axis=-1, keepdims=True) + RMS_EPS) * gkv_ref[...]
    kv = _dot(kvn.astype(BF16), wukv_ref[...])
    kpe = _rope128(lat[:, 384:512], mlc, mlsa, mlsb, MLA_ROPE // 2)
    mscale = (MLA_NOPE + MLA_ROPE) ** -0.5 * LOG2E
    for h in range(HEADS):
        c = slice(LANES * h, LANES * (h + 1))
        put(O_MLQ + LANES * h, _rope128(qm[:, c], mlc, mlsa, mlsb, MLA_ROPE // 2) * mscale)
        put(O_MLK + LANES * h, kv[:, c] + kpe)
        put(O_MLV + LANES * h, with_ones(kv[:, 512 + LANES * h:512 + LANES * (h + 1)]))


def _proj_call(x, w1, wuq, wukv, gq, gkv, tables, n_blk):
    t_tokens = x.shape[0]
    n_steps = t_tokens // PROJ_TM
    full = lambda shape: pl.BlockSpec(shape, lambda i: (0,) * len(shape))
    tab_spec = pl.BlockSpec((PROJ_TM, LANES), lambda i: (i, 0))
    return pl.pallas_call(
        functools.partial(_proj_kernel, n_blk),
        grid=(n_steps,),
        in_specs=[pl.BlockSpec((PROJ_TM, D_MODEL), lambda i: (i, 0)),
                  full((D_MODEL, PROJ_IN_COLS)), full((256, 512)), full((128, 1024)),
                  full((1, 256)), full((1, 128))] + [tab_spec] * 9,
        out_specs=[pl.BlockSpec((PROJ_TM, PROJ_OUT_COLS), lambda i: (i, 0)),
                   pl.BlockSpec((1, PROJ_TM // MOBA_BLOCK, 512), lambda i: (i, 0, 0))],
        out_shape=[jax.ShapeDtypeStruct((t_tokens, PROJ_OUT_COLS), BF16),
                   jax.ShapeDtypeStruct((n_steps, PROJ_TM // MOBA_BLOCK, 512), F32)],
        compiler_params=pltpu.CompilerParams(dimension_semantics=("parallel",),
                                             vmem_limit_bytes=VMEM_LIMIT_BYTES),
        name="proj",
    )(x, w1, wuq, wukv, gq, gkv, *tables)


def _causal_mask(tq, tk):
    return lax.broadcasted_iota(jnp.int32, (tq, tk), 1) <= lax.broadcasted_iota(jnp.int32, (tq, tk), 0)


def _kv_rows(j, tile):
    return pl.ds(pl.multiple_of(j * tile, tile), tile)


def _flash_update(s, v_tile, m_ref, acc_ref, idx):
    m_prev = m_ref[idx]
    m_new = jnp.maximum(m_prev, jnp.max(s, axis=-1, keepdims=True))
    p = jnp.concatenate([jnp.exp2(s[:, LANES * t:LANES * (t + 1)] - m_new) for t in range(s.shape[1] // LANES)],
                        axis=1)
    acc_ref[idx] = jnp.exp2(m_prev - m_new) * acc_ref[idx] + _dot(p.astype(BF16), v_tile)
    m_ref[idx] = m_new


def _flash_init(m_ref, acc_ref):
    m_ref[...] = jnp.full(m_ref.shape, NEG_INF, F32)
    acc_ref[...] = jnp.zeros(acc_ref.shape, F32)


def _normalized(acc):
    return acc / pltpu.roll(acc, HEAD_DIM, axis=1)


def _store_pairs(o_ref, outs):
    lane = _lane_iota(outs[0].shape)
    for t in range(2):
        pair = jnp.where(lane < HEAD_DIM, outs[2 * t], pltpu.roll(outs[2 * t + 1], HEAD_DIM, axis=1))
        o_ref[0, :, LANES * t:LANES * (t + 1)] = pair.astype(o_ref.dtype)


def _pipelined_sweep(qi, block_of, scores, update, buf_a, buf_b):
    scores(block_of(0), True, buf_a)

    def body(p, carry):
        t = 2 * p
        scores(block_of(t + 1), False, buf_b)
        update(buf_a, block_of(t))
        scores(block_of(t + 2), False, buf_a)
        update(buf_b, block_of(t + 1))
        return carry

    lax.fori_loop(0, jnp.right_shift(qi, 1), body, 0)
    odd = lax.rem(qi, 2)

    @pl.when(odd == 1)
    def _():
        scores(block_of(qi), False, buf_b)
        update(buf_a, block_of(qi - 1))
        update(buf_b, block_of(qi))

    @pl.when(odd == 0)
    def _():
        update(buf_a, block_of(qi))


def _diag_then_past(qi):
    return lambda step: jnp.where(step == 0, qi, step - 1)


def _mla_kernel(q_ref, k_ref, v_ref, o_ref, m_ref, acc_ref, sa_ref, sb_ref):
    qi = pl.program_id(1)
    diag = _causal_mask(FLASH_TILE, FLASH_TILE)
    _flash_init(m_ref, acc_ref)

    def scores(j, diagonal, buf):
        for h in range(HEADS):
            c = slice(LANES * h, LANES * (h + 1))
            s = _dot_nt(q_ref[0, :, c], k_ref[0, _kv_rows(j, FLASH_TILE), c])
            buf[h] = jnp.where(diag, s, NEG_INF) if diagonal else s

    def update(buf, j):
        for h in range(HEADS):
            _flash_update(buf[h], v_ref[0, _kv_rows(j, FLASH_TILE), LANES * h:LANES * (h + 1)], m_ref, acc_ref, h)

    _pipelined_sweep(qi, _diag_then_past(qi), scores, update, sa_ref, sb_ref)
    _store_pairs(o_ref, [_normalized(acc_ref[h]) for h in range(HEADS)])


def _moba_kernel(q_ref, k_ref, v_ref, kmean_ref, o_ref, m_ref, acc_ref, sa_ref, sb_ref, qx_ref):
    qi = pl.program_id(1)
    tq = FLASH_TILE
    n_blk = kmean_ref.shape[1]
    diag = _causal_mask(tq, FLASH_TILE)
    _flash_init(m_ref, acc_ref)
    blk = lax.broadcasted_iota(jnp.int32, (n_blk, tq), 0)
    own = (tq // MOBA_BLOCK) * qi + jnp.right_shift(lax.broadcasted_iota(jnp.int32, (n_blk, tq), 1),
                                                    int(math.log2(MOBA_BLOCK)))
    lane = _lane_iota((tq, LANES))
    slot = (lane >= HEAD_DIM) & (lane < HEAD_DIM + n_blk)
    place = jnp.where(lax.broadcasted_iota(jnp.int32, (n_blk, LANES), 1)
                      == HEAD_DIM + lax.broadcasted_iota(jnp.int32, (n_blk, LANES), 0), 1.0, 0.0).astype(BF16)
    for h in range(HEADS):
        c = slice(LANES * h, LANES * (h + 1))
        qh = q_ref[0, :, c]
        km_hi, km_lo = _split_bf16(kmean_ref[0, :, c])
        gate = _dot_nt(km_hi, qh) + _dot_nt(km_lo, qh)
        gate = jnp.where(blk < own, gate, NEG_INF)
        sel = blk == own
        for _ in range(MOBA_TOPK):
            mx = jnp.max(gate, axis=0, keepdims=True)
            first_idx = jnp.min(jnp.where(gate == mx, blk, n_blk), axis=0, keepdims=True)
            pick = (blk == first_idx) & (mx > NEG_INF)
            sel = sel | pick
            gate = jnp.where(pick, NEG_INF, gate)
        placed = lax.dot_general(jnp.where(sel, 1.0, 0.0).astype(BF16), place, (((0,), (0,)), ((), ())),
                                 preferred_element_type=F32)
        bias = ((placed - 1.0) * MOBA_MASK_BIAS).astype(BF16)
        qx_ref[h] = jnp.where(slot, bias, qh)

    def scores(j, diagonal, buf):
        for h in range(HEADS):
            s = _dot_nt(qx_ref[h], k_ref[0, _kv_rows(j, FLASH_TILE), LANES * h:LANES * (h + 1)])
            buf[h] = jnp.where(diag, s, NEG_INF) if diagonal else s

    def update(buf, j):
        for h in range(HEADS):
            _flash_update(buf[h], v_ref[0, _kv_rows(j, FLASH_TILE), LANES * h:LANES * (h + 1)], m_ref, acc_ref, h)

    _pipelined_sweep(qi, _diag_then_past(qi), scores, update, sa_ref, sb_ref)
    _store_pairs(o_ref, [_normalized(acc_ref[h]) for h in range(HEADS)])


def _diff_kernel(lambda_init, q_ref, k_ref, v_ref, lam_ref, g_ref, o_ref, m_ref, acc_ref, sa_ref, sb_ref, qs_ref):
    qi = pl.program_id(1)
    tq = FLASH_TILE
    diag = _causal_mask(tq, FLASH_TILE)
    _flash_init(m_ref, acc_ref)
    lane = _lane_iota((tq, LANES))
    zero = jnp.zeros((), BF16)
    for t in range(2):
        qt = q_ref[0, :, LANES * t:LANES * (t + 1)]
        for part in range(4):
            keep = (lane >= DIFF_HALF * part) & (lane < DIFF_HALF * (part + 1))
            qs_ref[t, part * tq:(part + 1) * tq, :] = jnp.where(keep, qt, zero)

    def scores(j, diagonal, buf):
        for t in range(2):
            s_all = _dot_nt(qs_ref[t], k_ref[0, _kv_rows(j, FLASH_TILE), LANES * t:LANES * (t + 1)])
            for part in range(4):
                s = s_all[part * tq:(part + 1) * tq]
                buf[4 * t + part] = jnp.where(diag, s, NEG_INF) if diagonal else s

    def update(buf, j):
        for idx in range(2 * HEADS):
            h = idx // 2
            _flash_update(buf[idx], v_ref[0, _kv_rows(j, FLASH_TILE), LANES * h:LANES * (h + 1)], m_ref, acc_ref, idx)

    _pipelined_sweep(qi, _diag_then_past(qi), scores, update, sa_ref, sb_ref)

    lp = lam_ref[...]
    lam = (jnp.exp(jnp.sum(lp[0:1] * lp[1:2], axis=-1, keepdims=True))
           - jnp.exp(jnp.sum(lp[2:3] * lp[3:4], axis=-1, keepdims=True)) + lambda_init)
    outs = []
    for h in range(HEADS):
        o = _normalized(acc_ref[2 * h]) - lam * _normalized(acc_ref[2 * h + 1])
        ms = jnp.sum(jnp.where(lane < HEAD_DIM, o * o, 0.0), axis=-1, keepdims=True) * (1.0 / HEAD_DIM)
        outs.append(o * lax.rsqrt(ms + RMS_EPS) * g_ref[...] * (1.0 - lambda_init))
    _store_pairs(o_ref, outs)


def _sb_kernel(q_ref, k_ref, v_ref, o_ref, cum_ref, acc_ref, za_ref, zb_ref, qs_ref):
    qi = pl.program_id(1)
    tq, tk = SB_TQ, SB_TK
    lane = _lane_iota((tq, LANES))
    zero = jnp.zeros((), BF16)
    cum_ref[...] = jnp.zeros(cum_ref.shape, F32)
    acc_ref[...] = jnp.zeros(acc_ref.shape, F32)
    for t in range(2):
        qt = q_ref[0, :, LANES * t:LANES * (t + 1)]
        qs_ref[t, 0:tq, :] = jnp.where(lane < HEAD_DIM, qt, zero)
        qs_ref[t, tq:2 * tq, :] = jnp.where(lane >= HEAD_DIM, qt, zero)
    row = lax.broadcasted_iota(jnp.int32, (tk, tk), 0)
    col = lax.broadcasted_iota(jnp.int32, (tk, tk), 1)
    u_tri = jnp.where(row > col, 1.0, 0.0).astype(BF16)
    row2 = lax.broadcasted_iota(jnp.int32, (2 * tq, tk), 0)
    q_pos = qi * tq + jnp.where(row2 >= tq, row2 - tq, row2)
    k_off = lax.broadcasted_iota(jnp.int32, (2 * tq, tk), 1)
    n_blocks = (tq // tk) * (qi + 1)

    def scores(step, overlaps_tile, buf):
        j = n_blocks - 1 - step
        for t in range(2):
            z = _dot_nt(qs_ref[t], k_ref[0, _kv_rows(j, tk), LANES * t:LANES * (t + 1)])
            buf[t] = jnp.where(j * tk + k_off < q_pos, z, NEG_INF) if overlaps_tile else z

    def update(buf, step):
        j = n_blocks - 1 - step
        for t in range(2):
            z = buf[t]
            sp = jnp.maximum(z, 0.0) + jnp.log(1.0 + jnp.exp(-jnp.abs(z)))
            later = _dot(sp.astype(BF16), u_tri)
            cum = cum_ref[t]
            a = jnp.concatenate(
                [jnp.exp(z[:, LANES * n:LANES * (n + 1)] - sp[:, LANES * n:LANES * (n + 1)]
                         - later[:, LANES * n:LANES * (n + 1)] - cum) for n in range(tk // LANES)], axis=1)
            acc_ref[t] += _dot(a.astype(BF16), v_ref[0, _kv_rows(j, tk), LANES * t:LANES * (t + 1)])
            cum_ref[t] = cum + jnp.sum(sp, axis=-1, keepdims=True)

    scores(0, True, za_ref)
    scores(1, True, zb_ref)
    update(za_ref, 0)

    def body(i, carry):
        step = 2 * i + 1
        scores(step + 1, False, za_ref)
        update(zb_ref, step)
        scores(step + 2, False, zb_ref)
        update(za_ref, step + 1)
        return carry

    lax.fori_loop(0, qi, body, 0)
    update(zb_ref, n_blocks - 1)
    for t in range(2):
        acc = acc_ref[t]
        o_ref[0, :, LANES * t:LANES * (t + 1)] = jnp.where(lane < HEAD_DIM, acc[0:tq], acc[tq:2 * tq]).astype(o_ref.dtype)


def _attn_call(kernel, name, tile, proj3, q_off, k_off, v_off, q_width, k_width, v_width, scratch,
               extra_in=(), extra_specs=()):
    bsz, seq, _ = proj3.shape
    return pl.pallas_call(
        kernel,
        grid=(bsz, seq // tile),
        in_specs=[pl.BlockSpec((1, tile, q_width), lambda b, i: (b, i, q_off // q_width)),
                  pl.BlockSpec((1, seq, k_width), lambda b, i: (b, 0, k_off // k_width)),
                  pl.BlockSpec((1, seq, v_width), lambda b, i: (b, 0, v_off // v_width))] + list(extra_specs),
        out_specs=pl.BlockSpec((1, tile, GROUP_WIDTH), lambda b, i: (b, i, 0)),
        out_shape=jax.ShapeDtypeStruct((bsz, seq, GROUP_WIDTH), BF16),
        scratch_shapes=scratch,
        compiler_params=pltpu.CompilerParams(dimension_semantics=("parallel", "parallel"),
                                             vmem_limit_bytes=VMEM_LIMIT_BYTES),
        name=name,
    )(proj3, proj3, proj3, *extra_in)


def _state(n, rows):
    return pltpu.VMEM((n, rows, LANES), F32)


def _staging(n, rows, tk):
    return [pltpu.VMEM((n, rows, tk), F32), pltpu.VMEM((n, rows, tk), F32)]


def _memkv_kernel(mem_ref, wk_ref, wv_ref, k_ref, v_ref):
    mb = mem_ref[...].astype(BF16)
    k_ref[...] = _dot(mb, wk_ref[...]).astype(BF16)
    v_ref[...] = _dot(mb, wv_ref[...]).astype(BF16)


def _memkv_call(mem2, wk, wv):
    rows = mem2.shape[0]
    full = pl.BlockSpec((D_MODEL, D_MODEL), lambda i: (0, 0))
    blk = pl.BlockSpec((N_MEM, D_MODEL), lambda i: (i, 0))
    return pl.pallas_call(
        _memkv_kernel,
        grid=(rows // N_MEM,),
        in_specs=[blk, full, full],
        out_specs=[blk, blk],
        out_shape=[jax.ShapeDtypeStruct((rows, D_MODEL), BF16)] * 2,
        compiler_params=pltpu.CompilerParams(dimension_semantics=("parallel",),
                                             vmem_limit_bytes=VMEM_LIMIT_BYTES),
        name="memkv",
    )(mem2, wk, wv)


def _post_kernel(x_ref, oa_ref, ob_ref, oc_ref, od_ref, wout_ref, g1_ref, b1_ref,
                 wq_ref, km_ref, vm_ref, wo_ref, g2_ref, b2_ref, wrh_ref, wrl_ref, br_ref,
                 x2r_ref, route_ref, counts_ref, oh_ref, cnt_ref):
    @pl.when(pl.program_id(0) == 0)
    def _():
        cnt_ref[...] = jnp.zeros_like(cnt_ref)

    for r0 in range(0, POST_TM, POST_CHUNK):
        _post_rows(slice(r0, r0 + POST_CHUNK), x_ref, oa_ref, ob_ref, oc_ref, od_ref, wout_ref, g1_ref, b1_ref,
                   wq_ref, km_ref, vm_ref, wo_ref, g2_ref, b2_ref, wrh_ref, wrl_ref, br_ref, x2r_ref, route_ref,
                   oh_ref, cnt_ref)
    counts_ref[...] = cnt_ref[...]


def _post_rows(rows, x_ref, oa_ref, ob_ref, oc_ref, od_ref, wout_ref, g1_ref, b1_ref,
               wq_ref, km_ref, vm_ref, wo_ref, g2_ref, b2_ref, wrh_ref, wrl_ref, br_ref, x2r_ref, route_ref,
               oh_ref, cnt_ref):
    mix = (_dot(oa_ref[rows, :], wout_ref[0:256, :]) + _dot(ob_ref[rows, :], wout_ref[256:512, :])
           + _dot(oc_ref[rows, :], wout_ref[512:768, :]) + _dot(od_ref[rows, :], wout_ref[768:1024, :]))
    x1 = _layer_norm(DEEPNORM_ALPHA * x_ref[rows, :] + mix, g1_ref[...], b1_ref[...])

    q = (_dot(x1.astype(BF16), wq_ref[...]) * (XATTN_HEAD_DIM ** -0.5)).astype(BF16)
    for h in range(XATTN_HEADS):
        c = slice(XATTN_HEAD_DIM * h, XATTN_HEAD_DIM * (h + 1))
        s = _dot_nt(q[:, c], km_ref[0, :, c])
        p = jnp.exp(s - jnp.max(s, axis=-1, keepdims=True))
        o = _dot(p.astype(BF16), vm_ref[0, :, c]) / jnp.sum(p, axis=-1, keepdims=True)
        oh_ref[rows, c] = o.astype(BF16)
    xa = _dot(oh_ref[rows, :], wo_ref[...])
    x2 = _layer_norm(DEEPNORM_ALPHA * x1 + xa, g2_ref[...], b2_ref[...])
    x2r_ref[rows, 0:D_MODEL] = x2

    hi, lo = _split_bf16(x2)
    logits = _dot(hi, wrh_ref[...]) + _dot(hi, wrl_ref[...]) + _dot(lo, wrh_ref[...]) + br_ref[...]
    lane = _lane_iota(logits.shape)
    gl = jnp.where(lane < N_GROUPS, logits, NEG_INF)
    gmx = jnp.max(gl, axis=-1, keepdims=True)
    gidx = jnp.min(jnp.where(gl == gmx, lane, ROUTER_LANES), axis=-1, keepdims=True)
    gw = 1.0 / jnp.sum(jnp.exp(gl - gmx), axis=-1, keepdims=True)
    e0 = N_GROUPS + EXPERTS_PER_GROUP * gidx
    el = jnp.where((lane >= e0) & (lane < e0 + EXPERTS_PER_GROUP), logits, NEG_INF)
    m1 = jnp.max(el, axis=-1, keepdims=True)
    i1 = jnp.min(jnp.where(el == m1, lane, ROUTER_LANES), axis=-1, keepdims=True)
    el = jnp.where(lane == i1, NEG_INF, el)
    m2 = jnp.max(el, axis=-1, keepdims=True)
    i2 = jnp.min(jnp.where(el == m2, lane, ROUTER_LANES), axis=-1, keepdims=True)
    e = jnp.exp(m2 - m1)
    w1 = gw / (1.0 + e)
    w2 = gw * e / (1.0 + e)

    n = logits.shape[0]
    onehot = jnp.where(lane == gidx, 1.0, 0.0)
    earlier = (lax.broadcasted_iota(jnp.int32, (n, n), 1) < lax.broadcasted_iota(jnp.int32, (n, n), 0))
    before = _dot(jnp.where(earlier, 1.0, 0.0).astype(BF16), onehot.astype(BF16)) + cnt_ref[...]
    rank = jnp.sum(onehot * before, axis=-1, keepdims=True)
    cnt_ref[...] += jnp.sum(onehot, axis=0, keepdims=True)
    route = (jnp.where(lane == i1, w1, 0.0) + jnp.where(lane == i2, w2, 0.0)
             + jnp.where(lane == 0, gidx.astype(F32), 0.0) + jnp.where(lane == 1, rank, 0.0))
    x2r_ref[rows, D_MODEL:] = route
    route_ref[rows, :] = route


def _post_call(x, oa, ob, oc, od, wout, g1, b1, wq, kmem, vmem, wo, g2, b2, wrh, wrl, br, seq):
    t_tokens = x.shape[0]
    tm = POST_TM
    per_seq = seq // tm
    row = lambda w: pl.BlockSpec((tm, w), lambda i: (i, 0))
    full = lambda shape: pl.BlockSpec(shape, lambda i: (0,) * len(shape))
    mem_spec = pl.BlockSpec((1, N_MEM, D_MODEL), lambda i: (i // per_seq, 0, 0))
    sq = (D_MODEL, D_MODEL)
    return pl.pallas_call(
        _post_kernel,
        grid=(t_tokens // tm,),
        in_specs=[row(D_MODEL), row(256), row(256), row(256), row(256),
                  full(sq), full((1, D_MODEL)), full((1, D_MODEL)),
                  full(sq), mem_spec, mem_spec, full(sq), full((1, D_MODEL)), full((1, D_MODEL)),
                  full((D_MODEL, ROUTER_LANES)), full((D_MODEL, ROUTER_LANES)), full((1, ROUTER_LANES))],
        out_specs=[row(MOE_ROW_WIDTH), row(ROUTER_LANES), full((1, ROUTER_LANES))],
        out_shape=[jax.ShapeDtypeStruct((t_tokens, MOE_ROW_WIDTH), F32),
                   jax.ShapeDtypeStruct((t_tokens, ROUTER_LANES), F32),
                   jax.ShapeDtypeStruct((1, ROUTER_LANES), F32)],
        scratch_shapes=[pltpu.VMEM((tm, D_MODEL), BF16), pltpu.VMEM((1, ROUTER_LANES), F32)],
        compiler_params=pltpu.CompilerParams(dimension_semantics=("arbitrary",),
                                             vmem_limit_bytes=VMEM_LIMIT_BYTES),
        name="post",
    )(x, oa, ob, oc, od, wout, g1, b1, wq, kmem, vmem, wo, g2, b2, wrh, wrl, br)


def _row_dma_wait(src_ref, dst_ref, sem, rows):
    pltpu.make_async_copy(src_ref.at[pl.ds(0, rows)], dst_ref.at[pl.ds(0, rows)], sem).wait()


def _issue_row_dmas(copy_of_row, sems):
    for r in range(MOE_ROWS_PER_STEP):
        copy_of_row(r, sems[r % 2]).start(priority=r % 2)


def _gather_kernel(pos_ref, x_ref, xs_init_ref, xs_ref, sem0, sem1):
    del xs_init_ref
    _issue_row_dmas(lambda r, sem: pltpu.make_async_copy(
        x_ref.at[pl.ds(r, 1)], xs_ref.at[pl.ds(pos_ref[0, 0, r], 1)], sem), (sem0, sem1))
    for sem in (sem0, sem1):
        _row_dma_wait(x_ref, xs_ref, sem, MOE_ROWS_PER_STEP // 2)


def _scatter_kernel(pos_ref, ys_ref, y_ref, sem0, sem1):
    _issue_row_dmas(lambda r, sem: pltpu.make_async_copy(
        ys_ref.at[pl.ds(pos_ref[0, 0, r], 1)], y_ref.at[pl.ds(r, 1)], sem), (sem0, sem1))
    for sem in (sem0, sem1):
        _row_dma_wait(ys_ref, y_ref, sem, MOE_ROWS_PER_STEP // 2)


def _pos_spec():
    return pl.BlockSpec((1, 1, MOE_ROWS_PER_STEP), lambda i: (i, 0, 0), memory_space=pltpu.SMEM)


def _token_rows_spec(width):
    return pl.BlockSpec((MOE_ROWS_PER_STEP, width), lambda i: (i, 0))


def _gather_call(pos3, x2r, sorted_buf):
    n_steps = pos3.shape[0]
    any_spec = pl.BlockSpec(memory_space=pl.ANY)
    return pl.pallas_call(
        _gather_kernel,
        grid=(n_steps,),
        in_specs=[_pos_spec(), _token_rows_spec(x2r.shape[1]), any_spec],
        out_specs=any_spec,
        out_shape=jax.ShapeDtypeStruct(sorted_buf.shape, sorted_buf.dtype),
        input_output_aliases={2: 0},
        scratch_shapes=[pltpu.SemaphoreType.DMA(()), pltpu.SemaphoreType.DMA(())],
        compiler_params=pltpu.CompilerParams(dimension_semantics=("arbitrary",), has_side_effects=True),
        name="moe_gather",
    )(pos3, x2r, sorted_buf)


def _scatter_call(pos3, ys, t_tokens):
    n_steps = pos3.shape[0]
    any_spec = pl.BlockSpec(memory_space=pl.ANY)
    return pl.pallas_call(
        _scatter_kernel,
        grid=(n_steps,),
        in_specs=[_pos_spec(), any_spec],
        out_specs=_token_rows_spec(ys.shape[1]),
        out_shape=jax.ShapeDtypeStruct((t_tokens, ys.shape[1]), ys.dtype),
        scratch_shapes=[pltpu.SemaphoreType.DMA(()), pltpu.SemaphoreType.DMA(())],
        compiler_params=pltpu.CompilerParams(dimension_semantics=("arbitrary",)),
        name="moe_scatter",
    )(pos3, ys)


def _moe_kernel(tile_grp_ref, n_used_ref, xs_ref, wg_ref, wu_ref, wd_ref, g_ref, b_ref, o_ref):
    i = pl.program_id(0)

    @pl.when(i < n_used_ref[0])
    def _():
        grp = tile_grp_ref[i]
        x2 = xs_ref[:, 0:D_MODEL]
        xb = x2.astype(BF16)
        route = xs_ref[:, D_MODEL:]
        lane = _lane_iota(route.shape)
        acc = jnp.zeros(x2.shape, F32)
        for j in range(EXPERTS_PER_GROUP):
            e = N_GROUPS + EXPERTS_PER_GROUP * grp + j
            gate_col = jnp.sum(jnp.where(lane == e, route, 0.0), axis=-1, keepdims=True)
            gg = _dot(xb, wg_ref[j])
            uu = _dot(xb, wu_ref[j])
            hid = gg * (1.0 / (1.0 + jnp.exp(-gg))) * uu * gate_col
            acc = acc + _dot(hid.astype(BF16), wd_ref[j])
        o_ref[...] = _layer_norm(DEEPNORM_ALPHA * x2 + acc, g_ref[...], b_ref[...])

    @pl.when(i >= n_used_ref[0])
    def _():
        o_ref[...] = jnp.zeros_like(o_ref)


def _moe_call(tile_grp, n_used, xs, wg, wu, wd, g, b):
    cap_rows = xs.shape[0]
    tm = MOE_TM
    row = lambda w: pl.BlockSpec((tm, w), lambda i, tg, nu: (i, 0))
    vec = pl.BlockSpec((1, D_MODEL), lambda i, tg, nu: (0, 0))
    wspec = lambda shape: pl.BlockSpec(shape, lambda i, tg, nu: (tg[i], 0, 0))
    return pl.pallas_call(
        _moe_kernel,
        grid_spec=pltpu.PrefetchScalarGridSpec(
            num_scalar_prefetch=2,
            grid=(cap_rows // tm,),
            in_specs=[row(MOE_ROW_WIDTH),
                      wspec((EXPERTS_PER_GROUP, D_MODEL, EXPERT_FF)),
                      wspec((EXPERTS_PER_GROUP, D_MODEL, EXPERT_FF)),
                      wspec((EXPERTS_PER_GROUP, EXPERT_FF, D_MODEL)),
                      vec, vec],
            out_specs=row(D_MODEL)),
        out_shape=jax.ShapeDtypeStruct((cap_rows, D_MODEL), F32),
        compiler_params=pltpu.CompilerParams(dimension_semantics=("arbitrary",),
                                             vmem_limit_bytes=VMEM_LIMIT_BYTES),
        name="moe",
    )(tile_grp, n_used, xs, wg, wu, wd, g, b)


def _moe_capacity_rows(t_tokens):
    return (t_tokens // MOE_TM + N_GROUPS) * MOE_TM


def _moe_layer(x2r, route, counts, sorted_buf, wg, wu, wd, g, b):
    t_tokens = x2r.shape[0]
    tm = MOE_TM
    n_tiles = sorted_buf.shape[0] // tm
    gid = route[:, 0].astype(jnp.int32)
    rank = route[:, 1].astype(jnp.int32)
    cnt = counts[0, :N_GROUPS].astype(jnp.int32)
    tiles_per_grp = (cnt + tm - 1) // tm
    tile_end = jnp.cumsum(tiles_per_grp)
    pos = ((tile_end - tiles_per_grp) * tm)[gid] + rank
    tile_ids = jnp.arange(n_tiles, dtype=jnp.int32)
    tile_grp = jnp.minimum(jnp.sum((tile_ids[:, None] >= tile_end[None, :]).astype(jnp.int32), axis=1),
                           N_GROUPS - 1)
    n_used = tile_end[-1:].astype(jnp.int32)
    pos3 = pos.reshape(t_tokens // MOE_ROWS_PER_STEP, 1, MOE_ROWS_PER_STEP)

    xs = _gather_call(pos3, x2r, sorted_buf)
    ys = _moe_call(tile_grp, n_used, xs, wg, wu, wd, g, b)
    return _scatter_call(pos3, ys, t_tokens), xs


def _rope_lane_tables(positions, rot, period, offset):
    half = rot // 2
    inv = ROPE_THETA ** (-jnp.arange(0, rot, 2, dtype=F32) / rot)
    ang = positions.astype(F32).reshape(-1)[:, None] * inv
    cos, sin = jnp.cos(ang), jnp.sin(ang)
    p = jnp.arange(LANES) % period - offset
    first = (p >= 0) & (p < half)
    second = (p >= half) & (p < rot)
    idx = jnp.clip(jnp.where(second, p - half, p), 0, half - 1)
    cg, sg = cos[:, idx], sin[:, idx]
    c = jnp.where(first | second, cg, 1.0)
    sa = jnp.where(first, -sg, 0.0)
    sb = jnp.where(second, sg, 0.0)
    return c, sa, sb


def _prep_proj_weights(w_in, w_uq, w_ukv, q_norm, kv_norm):
    q_lat, kv_lat, k_rope = w_in[:, 0:192], w_in[:, 192:320], w_in[:, 320:352]
    sb, mb, df = w_in[:, 352:1120], w_in[:, 1120:1888], w_in[:, 1888:2656]
    zeros = lambda n: jnp.zeros((D_MODEL, n), w_in.dtype)

    def head_tiles(w):
        w = w.reshape(w.shape[0], HEADS, HEAD_DIM)
        return jnp.pad(w, ((0, 0), (0, 0), (0, LANES - HEAD_DIM))).reshape(w.shape[0], HEADS * LANES)

    w1 = jnp.concatenate([mb, df, q_lat, zeros(64), kv_lat, zeros(64), k_rope, zeros(32), sb], axis=1)
    wuq = w_uq.reshape(MLA_Q_LORA, HEADS, MLA_NOPE + MLA_ROPE)
    wuq = jnp.pad(wuq, ((0, 256 - MLA_Q_LORA), (0, 0), (0, LANES - MLA_NOPE - MLA_ROPE))).reshape(256, HEADS * LANES)
    wukv = w_ukv.reshape(MLA_KV_LORA, HEADS, 2, HEAD_DIM)
    wk = head_tiles(wukv[:, :, 0].reshape(MLA_KV_LORA, HEADS * HEAD_DIM))
    wv = head_tiles(wukv[:, :, 1].reshape(MLA_KV_LORA, HEADS * HEAD_DIM))
    gq = jnp.pad(q_norm, (0, 256 - MLA_Q_LORA)).reshape(1, 256)
    return (w1.astype(BF16), wuq.astype(BF16), jnp.concatenate([wk, wv], axis=1).astype(BF16),
            gq, kv_norm.reshape(1, MLA_KV_LORA))


def kernel(x, mem, positions, w_in, mla_q_norm, w_uq, mla_kv_norm, w_ukv, diff_lambda, diff_subln,
           w_out, ln_mix_g, ln_mix_b, xattn_wq, xattn_wk, xattn_wv, xattn_wo, ln_mem_g, ln_mem_b,
           router_group_w, router_group_b, router_expert_w, router_expert_b,
           expert_w_gate, expert_w_up, expert_w_down, ln_ffn_g, ln_ffn_b):
    bsz, seq, _ = x.shape
    t_tokens = bsz * seq
    n_blk = seq // MOBA_BLOCK
    assert seq % PROJ_TM == 0 and seq % POST_TM == 0 and seq % FLASH_TILE == 0 and seq % SB_TQ == 0 and SB_TQ == 2 * SB_TK
    assert t_tokens % MOE_TM == 0 and t_tokens % MOE_ROWS_PER_STEP == 0
    assert mem.shape[1] == N_MEM and n_blk <= LANES - HEAD_DIM

    tables = (_rope_lane_tables(positions, MOBA_ROT, HEAD_DIM, 0)
              + _rope_lane_tables(positions, DIFF_ROT, DIFF_HALF, 0)
              + _rope_lane_tables(positions, MLA_ROPE, LANES, MLA_NOPE))
    mem2 = mem.reshape(bsz * N_MEM, D_MODEL)
    row = lambda v: v.reshape(1, -1)
    xf = x.reshape(t_tokens, D_MODEL)
    sorted_buf = jnp.zeros((_moe_capacity_rows(t_tokens), MOE_ROW_WIDTH), F32)

    for l in range(DEPTH):
        lambda_init = 0.8 - 0.6 * math.exp(-0.3 * l)
        w1, wuq, wukv, gq, gkv = _prep_proj_weights(w_in[l], w_uq[l], w_ukv[l], mla_q_norm[l], mla_kv_norm[l])
        proj, kmean = _proj_call(xf, w1, wuq, wukv, gq, gkv, tables, n_blk)
        proj3 = proj.reshape(bsz, seq, PROJ_OUT_COLS)
        kmean3 = kmean.reshape(bsz, n_blk, HEADS * LANES)

        ft, st = FLASH_TILE, SB_TQ
        per_head, per_pair = HEADS * LANES, GROUP_WIDTH
        o_mla = _attn_call(_mla_kernel, "mla_attn", ft, proj3, O_MLQ, O_MLK, O_MLV, per_head, per_head, per_head,
                           [_state(HEADS, ft), _state(HEADS, ft)] + _staging(HEADS, ft, ft))
        o_sb = _attn_call(_sb_kernel, "sb_attn", st, proj3, O_SBQ, O_SBK, O_SBV, per_pair, per_pair, per_pair,
                          [_state(2, 2 * st), _state(2, 2 * st)]
                          + _staging(2, 2 * st, SB_TK) + [pltpu.VMEM((2, 2 * st, LANES), BF16)])
        o_mb = _attn_call(_moba_kernel, "moba_attn", ft, proj3, O_MBQ, O_MBK, O_MBV, per_head, per_head, per_head,
                          [_state(HEADS, ft), _state(HEADS, ft)] + _staging(HEADS, ft, ft)
                          + [pltpu.VMEM((HEADS, ft, LANES), BF16)],
                          extra_in=(kmean3,),
                          extra_specs=(pl.BlockSpec((1, n_blk, HEADS * LANES), lambda b, i: (b, 0, 0)),))
        subln = jnp.tile(diff_subln[l], LANES // HEAD_DIM).reshape(1, LANES)
        o_df = _attn_call(functools.partial(_diff_kernel, lambda_init), "diff_attn", ft, proj3,
                          O_DFQ, O_DFK, O_DFV, per_pair, per_pair, per_head,
                          [_state(2 * HEADS, ft), _state(2 * HEADS, ft)] + _staging(2 * HEADS, ft, ft)
                          + [pltpu.VMEM((2, 4 * ft, LANES), BF16)],
                          extra_in=(diff_lambda[l], subln),
                          extra_specs=(pl.BlockSpec((4, DIFF_HALF), lambda b, i: (0, 0)),
                                       pl.BlockSpec((1, LANES), lambda b, i: (0, 0))))

        kmem, vmem = _memkv_call(mem2, xattn_wk[l].astype(BF16), xattn_wv[l].astype(BF16))
        wr = jnp.concatenate([router_group_w[l], router_expert_w[l],
                              jnp.zeros((D_MODEL, ROUTER_LANES - N_GROUPS - N_EXPERTS), F32)], axis=1)
        wr_hi = wr.astype(BF16)
        wr_lo = (wr - wr_hi.astype(F32)).astype(BF16)
        br = jnp.concatenate([router_group_b[l], router_expert_b[l],
                              jnp.zeros((ROUTER_LANES - N_GROUPS - N_EXPERTS,), F32)]).reshape(1, ROUTER_LANES)
        flat = lambda o: o.reshape(t_tokens, GROUP_WIDTH)
        x2r, route, counts = _post_call(
            xf, flat(o_mla), flat(o_sb), flat(o_mb), flat(o_df), w_out[l].astype(BF16),
            row(ln_mix_g[l]), row(ln_mix_b[l]), xattn_wq[l].astype(BF16),
            kmem.reshape(bsz, N_MEM, D_MODEL), vmem.reshape(bsz, N_MEM, D_MODEL), xattn_wo[l].astype(BF16),
            row(ln_mem_g[l]), row(ln_mem_b[l]), wr_hi, wr_lo, br, seq)

        xf, sorted_buf = _moe_layer(x2r, route, counts, sorted_buf, expert_w_gate[l].astype(BF16),
                                    expert_w_up[l].astype(BF16), expert_w_down[l].astype(BF16),
                                    row(ln_ffn_g[l]), row(ln_ffn_b[l]))
    return xf.reshape(bsz, seq, D_MODEL)
```
